```python
import math
import jax
import jax.numpy as jnp
from jax import lax
import numpy as np

D_MODEL = 2048
BATCH = 1
SEQ = 16384
DEPTH = 4

GRID_W = 64
CTX_LEN = 256
N_MIXERS = 3
N_A = (DEPTH + 2) // 3
N_B = (DEPTH + 1) // 3
N_C = DEPTH // 3

A_HEADS = 32
A_KV_HEADS = 4
A_HEAD_DIM = 64
WINDOW = 128
BLOCK = 128
B_HEADS = 32
B_HEAD_DIM = 64
NB_KH = 8
NB_KW = 16
NA_ROWS_PER_BLOCK = 2
D_RNN = 2560
RNN_BLOCKS = 10
RNN_BW = D_RNN // RNN_BLOCKS
CONV_W = 4
CONV_LEFT = 2
LRU_C = 8.0
N_GROUPS = 4
EXP_PER_GROUP = 8
N_EXPERTS = N_GROUPS * EXP_PER_GROUP
TOP_K = 2
D_EXPERT = 768
MOE_BLOCK = 128

ROPE_BASE = 10000.0
EPS = 1e-6
NEG = -1e30

kernel_name = "hybrid_dit_window_natten_rglru_hmoe"


def rmsnorm(x, g):
    xf = x.astype(jnp.float32)
    y = xf * lax.rsqrt(jnp.mean(xf * xf, axis=-1, keepdims=True) + EPS)
    return (y * g.astype(jnp.float32)).astype(x.dtype)


def modulate(h, shift, scale):
    return h * (1 + scale[:, None]) + shift[:, None]


def axial_rope(x, rows, cols):
    half = x.shape[-1] // 2
    quarter = half // 2
    inv = ROPE_BASE ** (-jnp.arange(quarter, dtype=jnp.float32) / quarter)

    def rot(xa, pos):
        ang = pos.astype(jnp.float32)[:, None] * inv
        cos = jnp.cos(ang)[None, :, None]
        sin = jnp.sin(ang)[None, :, None]
        x1, x2 = xa[..., :quarter], xa[..., quarter:]
        return jnp.concatenate([x1 * cos - x2 * sin, x2 * cos + x1 * sin], axis=-1)

    xf = x.astype(jnp.float32)
    return jnp.concatenate([rot(xf[..., :half], rows), rot(xf[..., half:], cols)], axis=-1).astype(x.dtype)


def softmax_with_sink(s, sink):
    m = jnp.maximum(jnp.max(s, axis=-1, keepdims=True), sink)
    e = jnp.exp(s - m)
    return e / (jnp.sum(e, axis=-1, keepdims=True) + jnp.exp(sink - m))


def ctx_self_attention(q, k, v, sink):
    B, C = q.shape[:2]
    s = jnp.einsum('bqkgd,bskd->bkgqs', q, k, preferred_element_type=jnp.float32) * (q.shape[-1] ** -0.5)
    p = softmax_with_sink(s, sink) if sink is not None else jax.nn.softmax(s, axis=-1)
    o = jnp.einsum('bkgqs,bskd->bqkgd', p.astype(v.dtype), v)
    return o.reshape(B, C, -1)


def window_attention(h_ctx, h_lat, w_qkv, w_o, sink, rows, cols, ctx_out):
    B, L, _ = h_lat.shape
    G = A_HEADS // A_KV_HEADS
    nq, nkv = A_HEADS * A_HEAD_DIM, A_KV_HEADS * A_HEAD_DIM

    def project(h):
        S = h.shape[1]
        qkv = h @ w_qkv
        q = qkv[..., :nq].reshape(B, S, A_HEADS, A_HEAD_DIM)
        k = qkv[..., nq:nq + nkv].reshape(B, S, A_KV_HEADS, A_HEAD_DIM)
        v = qkv[..., nq + nkv:].reshape(B, S, A_KV_HEADS, A_HEAD_DIM)
        return q, k, v

    qc, kc, vc = project(h_ctx)
    ql, kl, vl = project(h_lat)
    ql = axial_rope(ql, rows, cols).reshape(B, L, A_KV_HEADS, G, A_HEAD_DIM)
    kl = axial_rope(kl, rows, cols)
    sink_g = sink.astype(jnp.float32).reshape(A_KV_HEADS, G)[None, :, :, None, None]
    scale = A_HEAD_DIM ** -0.5
    nb = L // BLOCK

    def band(t):
        tp = jnp.pad(t, ((0, 0), (WINDOW, WINDOW), (0, 0), (0, 0))).reshape(B, nb + 2, BLOCK, A_KV_HEADS, A_HEAD_DIM)
        return jnp.concatenate([tp[:, :-2], tp[:, 1:-1], tp[:, 2:]], axis=2).swapaxes(0, 1)

    q_blocks = ql.reshape(B, nb, BLOCK, A_KV_HEADS, G, A_HEAD_DIM).swapaxes(0, 1)
    rel = jnp.arange(BLOCK)[:, None] + WINDOW - jnp.arange(3 * BLOCK)[None, :]

    def block(args):
        bi, qb, kb, vb = args
        key_pos = bi * BLOCK - WINDOW + jnp.arange(3 * BLOCK)
        ok = (jnp.abs(rel) <= WINDOW) & (key_pos >= 0)[None] & (key_pos < L)[None]
        s_loc = jnp.einsum('bqkgd,bskd->bkgqs', qb, kb, preferred_element_type=jnp.float32) * scale
        s_loc = jnp.where(ok, s_loc, NEG)
        s_ctx = jnp.einsum('bqkgd,bskd->bkgqs', qb, kc, preferred_element_type=jnp.float32) * scale
        p = softmax_with_sink(jnp.concatenate([s_loc, s_ctx], axis=-1), sink_g).astype(vb.dtype)
        return (jnp.einsum('bkgqs,bskd->bqkgd', p[..., :3 * BLOCK], vb)
                + jnp.einsum('bkgqs,bskd->bqkgd', p[..., 3 * BLOCK:], vc))

    o = lax.map(block, (jnp.arange(nb), q_blocks, band(kl), band(vl)))
    y_lat = o.swapaxes(0, 1).reshape(B, L, nq) @ w_o
    y_ctx = None
    if ctx_out:
        q_ctx = qc.reshape(B, qc.shape[1], A_KV_HEADS, G, A_HEAD_DIM)
        y_ctx = ctx_self_attention(q_ctx, kc, vc, sink_g) @ w_o
    return y_ctx, y_lat


def neighbourhood_indices(rows_n):
    kh, kw = min(NB_KH, rows_n), NB_KW
    r = jnp.arange(rows_n, dtype=jnp.int32)
    c = jnp.arange(GRID_W, dtype=jnp.int32)
    rs = jnp.clip(r - kh // 2, 0, rows_n - kh)
    cs = jnp.clip(c - kw // 2, 0, GRID_W - kw)
    krow = rs[:, None] + jnp.arange(kh, dtype=jnp.int32)
    kcol = cs[:, None] + jnp.arange(kw, dtype=jnp.int32)
    L = rows_n * GRID_W
    key_idx = (krow[:, None, :, None] * GRID_W + kcol[None, :, None, :]).reshape(L, kh * kw)
    drow = krow - r[:, None] + (NB_KH - 1)
    dcol = kcol - c[:, None] + (NB_KW - 1)
    bias_idx = (drow[:, None, :, None] * (2 * NB_KW - 1) + dcol[None, :, None, :]).reshape(L, kh * kw)
    return key_idx, bias_idx


def neighbourhood_attention(h_ctx, h_lat, w_qkv, w_o, rpb, key_idx, bias_idx, ctx_out):
    B, L, _ = h_lat.shape
    H, dh = B_HEADS, B_HEAD_DIM

    def project(h):
        qkv = (h @ w_qkv).reshape(B, h.shape[1], 3, H, dh)
        return qkv[:, :, 0], qkv[:, :, 1], qkv[:, :, 2]

    qc, kc, vc = project(h_ctx)
    ql, kl, vl = project(h_lat)
    scale = dh ** -0.5
    QB = NA_ROWS_PER_BLOCK * GRID_W
    nb = L // QB
    K = key_idx.shape[-1]
    rpb_flat = rpb.astype(jnp.float32).reshape(H, -1)

    def block(args):
        qb, kidx, bidx = args
        kg = kl[:, kidx]
        vg = vl[:, kidx]
        s_loc = jnp.einsum('bqhd,bqkhd->bhqk', qb, kg, preferred_element_type=jnp.float32) * scale
        s_loc = s_loc + rpb_flat[:, bidx][None]
        s_ctx = jnp.einsum('bqhd,bshd->bhqs', qb, kc, preferred_element_type=jnp.float32) * scale
        p = jax.nn.softmax(jnp.concatenate([s_loc, s_ctx], axis=-1), axis=-1).astype(vg.dtype)
        return (jnp.einsum('bhqk,bqkhd->bqhd', p[..., :K], vg)
                + jnp.einsum('bhqs,bshd->bqhd', p[..., K:], vc))

    o = lax.map(block, (ql.reshape(B, nb, QB, H, dh).swapaxes(0, 1),
                        key_idx.reshape(nb, QB, K), bias_idx.reshape(nb, QB, K)))
    y_lat = o.swapaxes(0, 1).reshape(B, L, H * dh) @ w_o
    y_ctx = None
    if ctx_out:
        y_ctx = ctx_self_attention(qc[:, :, :, None], kc, vc, None) @ w_o
    return y_ctx, y_lat


def depthwise_conv(x, w, b):
    y = lax.conv_general_dilated(x, w[:, None, :], window_strides=(1,),
                                 padding=[(CONV_LEFT, CONV_W - 1 - CONV_LEFT)],
                                 dimension_numbers=('NWC', 'WIO', 'NWC'),
                                 feature_group_count=x.shape[-1])
    return y + b


def lru_coeffs(x, wa, ba, wx, bx, lam):
    xb = x.reshape(x.shape[0], x.shape[1], RNN_BLOCKS, RNN_BW)
    r = jax.nn.sigmoid(jnp.einsum('bsnk,nkj->bsnj', xb, wa.astype(jnp.float32)).reshape(x.shape) + ba.astype(jnp.float32))
    ig = jax.nn.sigmoid(jnp.einsum('bsnk,nkj->bsnj', xb, wx.astype(jnp.float32)).reshape(x.shape) + bx.astype(jnp.float32))
    log_a = -LRU_C * r * jax.nn.softplus(-lam.astype(jnp.float32))
    return jnp.exp(log_a), jnp.sqrt(-jnp.expm1(2.0 * log_a)) * (ig * x)


def _lin_combine(left, right):
    return left[0] * right[0], right[0] * left[1] + right[1]


def linear_scan(a, b, h0, reverse):
    if reverse:
        a, b = jnp.flip(a, 1), jnp.flip(b, 1)
    b = b.at[:, 0].add(a[:, 0] * h0)
    _, h = lax.associative_scan(_lin_combine, (a, b), axis=1)
    return jnp.flip(h, 1) if reverse else h


def rglru_mixer(h_ctx, h_lat, w_in, conv_w, conv_b, wa, ba, wx, bx, lam, w_out, ctx_out):
    u_ctx = h_ctx @ w_in
    u_lat = h_lat @ w_in
    xc = depthwise_conv(u_ctx[..., D_RNN:], conv_w, conv_b).astype(jnp.float32)
    xl = depthwise_conv(u_lat[..., D_RNN:], conv_w, conv_b).astype(jnp.float32)
    B = h_lat.shape[0]
    h0 = jnp.zeros((B, D_RNN), jnp.float32)
    sum_c, sum_l = 0.0, 0.0
    for d in range(2):
        rev = d == 1
        a_c, b_c = lru_coeffs(xc, wa[d], ba[d], wx[d], bx[d], lam[d])
        a_l, b_l = lru_coeffs(xl, wa[d], ba[d], wx[d], bx[d], lam[d])
        hc = linear_scan(a_c, b_c, h0, rev)
        hc_final = hc[:, 0] if rev else hc[:, -1]
        hl = linear_scan(a_l, b_l, hc_final, rev)
        sum_c = sum_c + hc
        sum_l = sum_l + hl
    y_lat = (jax.nn.gelu(u_lat[..., :D_RNN]) * sum_l.astype(h_lat.dtype)) @ w_out
    y_ctx = None
    if ctx_out:
        y_ctx = (jax.nn.gelu(u_ctx[..., :D_RNN]) * sum_c.astype(h_ctx.dtype)) @ w_out
    return y_ctx, y_lat


def expert_dispatch(h, e_idx, e_w, w_gu, w_down):
    N, D = h.shape
    A = N * TOP_K
    flat_e = e_idx.reshape(A)
    flat_tok = jnp.arange(A, dtype=jnp.int32) // TOP_K
    order = jnp.argsort(flat_e)
    sorted_e = flat_e[order]
    counts = jnp.bincount(flat_e, length=N_EXPERTS)
    padded = (counts + MOE_BLOCK - 1) // MOE_BLOCK * MOE_BLOCK
    start_sorted = jnp.cumsum(counts) - counts
    end_pad = jnp.cumsum(padded)
    start_pad = end_pad - padded
    dest = start_pad[sorted_e] + jnp.arange(A, dtype=jnp.int32) - start_sorted[sorted_e]
    n_blocks = -(-(A + N_EXPERTS * (MOE_BLOCK - 1)) // MOE_BLOCK)
    P = n_blocks * MOE_BLOCK
    slot_tok = jnp.full((P,), N, jnp.int32).at[dest].set(flat_tok[order])
    slot_w = jnp.zeros((P,), jnp.float32).at[dest].set(e_w.reshape(A)[order])
    block_e = jnp.minimum(jnp.searchsorted(end_pad, jnp.arange(n_blocks, dtype=jnp.int32) * MOE_BLOCK, side='right'),
                          N_EXPERTS - 1)
    h_pad = jnp.concatenate([h, jnp.zeros((1, D), h.dtype)], axis=0)
    xb = h_pad[slot_tok].reshape(n_blocks, MOE_BLOCK, D)

    def block_ffn(args):
        xblk, e = args
        g, u = jnp.split(xblk @ w_gu[e], 2, axis=-1)
        return (jax.nn.silu(g) * u) @ w_down[e]

    yb = lax.map(block_ffn, (xb, block_e)).reshape(P, D)
    y = jnp.zeros((N + 1, D), h.dtype).at[slot_tok].add(yb * slot_w[:, None].astype(h.dtype))
    return y[:N]


def hier_moe(h, wg, bg, we, be, w_gu, w_down):
    N = h.shape[0]
    hf = h.astype(jnp.float32)
    g_prob = jax.nn.softmax(hf @ wg.astype(jnp.float32) + bg.astype(jnp.float32), axis=-1)
    g_sel = jnp.argmax(g_prob, axis=-1)
    rows = jnp.arange(N)
    g_w = g_prob[rows, g_sel]
    e_logits = (hf @ we.astype(jnp.float32) + be.astype(jnp.float32)).reshape(N, N_GROUPS, EXP_PER_GROUP)
    top_v, top_i = lax.top_k(e_logits[rows, g_sel], TOP_K)
    e_w = jax.nn.softmax(top_v, axis=-1) * g_w[:, None]
    e_idx = g_sel[:, None].astype(jnp.int32) * EXP_PER_GROUP + top_i.astype(jnp.int32)
    return expert_dispatch(h, e_idx, e_w, w_gu, w_down)


def setup_inputs(seed: int = 0) -> dict:
    key = jax.random.key(seed)
    ks = iter(jax.random.split(key, 40))
    f32 = jnp.float32
    D = D_MODEL

    def nrm(shape, fan_in):
        return jax.random.normal(next(ks), shape, f32) * (fan_in ** -0.5)

    def small(shape, s):
        return jax.random.normal(next(ks), shape, f32) * s

    u = jax.random.uniform(next(ks), (N_C, 2, D_RNN), f32, minval=0.9, maxval=0.999)
    a0 = u ** (1.0 / LRU_C)
    return {
        "x": jax.random.normal(next(ks), (BATCH, SEQ, D), f32),
        "c": jax.random.normal(next(ks), (BATCH, D), f32),
        "ctx": jax.random.normal(next(ks), (BATCH, CTX_LEN, D), f32),
        "c_ctx": jax.random.normal(next(ks), (D,), f32),
        "ada_w": nrm((DEPTH, D, 6 * D), D),
        "ada_b": small((DEPTH, 6 * D), 0.02),
        "norm_mix_g": 1.0 + small((DEPTH, D), 0.02),
        "norm_ffn_g": 1.0 + small((DEPTH, D), 0.02),
        "router_group_w": nrm((DEPTH, D, N_GROUPS), D),
        "router_group_b": small((DEPTH, N_GROUPS), 0.01),
        "router_expert_w": nrm((DEPTH, D, N_EXPERTS), D),
        "router_expert_b": small((DEPTH, N_EXPERTS), 0.01),
        "moe_w_gu": nrm((DEPTH, N_EXPERTS, D, 2 * D_EXPERT), D),
        "moe_w_down": nrm((DEPTH, N_EXPERTS, D_EXPERT, D), D_EXPERT),
        "attn_w_qkv": nrm((N_A, D, (A_HEADS + 2 * A_KV_HEADS) * A_HEAD_DIM), D),
        "attn_w_o": nrm((N_A, A_HEADS * A_HEAD_DIM, D), A_HEADS * A_HEAD_DIM),
        "attn_sink": small((N_A, A_HEADS), 1.0),
        "na_w_qkv": nrm((N_B, D, 3 * B_HEADS * B_HEAD_DIM), D),
        "na_w_o": nrm((N_B, B_HEADS * B_HEAD_DIM, D), B_HEADS * B_HEAD_DIM),
        "na_rpb": small((N_B, B_HEADS, 2 * NB_KH - 1, 2 * NB_KW - 1), 0.5),
        "rnn_w_in": nrm((N_C, D, 2 * D_RNN), D),
        "rnn_conv_w": nrm((N_C, CONV_W, D_RNN), CONV_W),
        "rnn_conv_b": small((N_C, D_RNN), 0.02),
        "rnn_wa": nrm((N_C, 2, RNN_BLOCKS, RNN_BW, RNN_BW), RNN_BW),
        "rnn_ba": small((N_C, 2, D_RNN), 0.1),
        "rnn_wx": nrm((N_C, 2, RNN_BLOCKS, RNN_BW, RNN_BW), RNN_BW),
        "rnn_bx": small((N_C, 2, D_RNN), 0.1),
        "rnn_lam": jnp.log(a0) - jnp.log1p(-a0),
        "rnn_w_out": nrm((N_C, D_RNN, D), D_RNN),
        "final_norm_g": 1.0 + small((D,), 0.02),
    }


def reference(x, c, ctx, c_ctx, ada_w, ada_b, norm_mix_g, norm_ffn_g,
              router_group_w, router_group_b, router_expert_w, router_expert_b,
              moe_w_gu, moe_w_down, attn_w_qkv, attn_w_o, attn_sink,
              na_w_qkv, na_w_o, na_rpb,
              rnn_w_in, rnn_conv_w, rnn_conv_b, rnn_wa, rnn_ba, rnn_wx, rnn_bx, rnn_lam, rnn_w_out,
              final_norm_g):
    B, L, D = x.shape
    C = ctx.shape[1]
    rows_n = L // GRID_W
    pos = jnp.arange(L, dtype=jnp.int32)
    rows, cols = pos // GRID_W, pos % GRID_W
    nb_key_idx, nb_bias_idx = neighbourhood_indices(rows_n)
    cond_l = jax.nn.silu(c)
    cond_c = jax.nn.silu(c_ctx)[None]
    h_lat, h_ctx = x, ctx
    for i in range(DEPTH):
        kind, j = i % N_MIXERS, i // N_MIXERS
        need_ctx = i < DEPTH - 1
        ml = (cond_l @ ada_w[i] + ada_b[i]).reshape(B, 6, D)
        mc = (cond_c @ ada_w[i] + ada_b[i]).reshape(1, 6, D)
        a_lat = modulate(rmsnorm(h_lat, norm_mix_g[i]), ml[:, 0], ml[:, 1])
        a_ctx = modulate(rmsnorm(h_ctx, norm_mix_g[i]), mc[:, 0], mc[:, 1])
        if kind == 0:
            y_ctx, y_lat = window_attention(a_ctx, a_lat, attn_w_qkv[j], attn_w_o[j], attn_sink[j],
                                            rows, cols, need_ctx)
        elif kind == 1:
            y_ctx, y_lat = neighbourhood_attention(a_ctx, a_lat, na_w_qkv[j], na_w_o[j], na_rpb[j],
                                                   nb_key_idx, nb_bias_idx, need_ctx)
        else:
            y_ctx, y_lat = rglru_mixer(a_ctx, a_lat, rnn_w_in[j], rnn_conv_w[j], rnn_conv_b[j],
                                       rnn_wa[j], rnn_ba[j], rnn_wx[j], rnn_bx[j], rnn_lam[j],
                                       rnn_w_out[j], need_ctx)
        h_lat = h_lat + ml[:, 2, None] * y_lat
        f_lat = modulate(rmsnorm(h_lat, norm_ffn_g[i]), ml[:, 3], ml[:, 4])
        if need_ctx:
            h_ctx = h_ctx + mc[:, 2, None] * y_ctx
            f_ctx = modulate(rmsnorm(h_ctx, norm_ffn_g[i]), mc[:, 3], mc[:, 4])
            tokens = jnp.concatenate([f_ctx, f_lat], axis=1).reshape(-1, D)
            out = hier_moe(tokens, router_group_w[i], router_group_b[i], router_expert_w[i],
                           router_expert_b[i], moe_w_gu[i], moe_w_down[i]).reshape(B, C + L, D)
            h_ctx = h_ctx + mc[:, 5, None] * out[:, :C]
            h_lat = h_lat + ml[:, 5, None] * out[:, C:]
        else:
            out = hier_moe(f_lat.reshape(-1, D), router_group_w[i], router_group_b[i], router_expert_w[i],
                           router_expert_b[i], moe_w_gu[i], moe_w_down[i]).reshape(B, L, D)
            h_lat = h_lat + ml[:, 5, None] * out
    return rmsnorm(h_lat, final_norm_g)
```

```python
import functools

import jax
import jax.numpy as jnp
from jax import lax
from jax.experimental import pallas as pl
from jax.experimental.pallas import tpu as pltpu

F32 = jnp.float32
BF16 = jnp.bfloat16

D = 2048
DEPTH = 4
GRID_W = 64
CTX = 256
HEADS = 32
HEAD_DIM = 64
A_KV_HEADS = 4
A_GROUP = HEADS // A_KV_HEADS
WINDOW = 128
NB_KH = 8
NB_KW = 16
D_RNN = 2560
RNN_BLOCKS = 10
RNN_BW = D_RNN // RNN_BLOCKS
CONV_W = 4
CONV_LEFT = 2
LRU_C = 8.0
N_GROUPS = 4
EXP_PER_GROUP = 8
N_EXPERTS = N_GROUPS * EXP_PER_GROUP
TOP_K = 2
D_EXPERT = 768
ROPE_BASE = 10000.0
EPS = 1e-6
NEG = -1e30

LANES = 128
SUBLANES = 8
MIB = 1024 * 1024

TM = 256
QB = 128
HP = HEADS // 2
MOE_MB = 256
SCAN_TC = 256
NA_KBLK = 5
SQRT_SCALE = HEAD_DIM ** -0.5


def _cparams(vmem_mib, sem=("arbitrary",)):
    return pltpu.CompilerParams(dimension_semantics=sem, vmem_limit_bytes=int(vmem_mib * MIB))


def _resident(block_shape, index_map):
    return pl.BlockSpec(block_shape, index_map, pipeline_mode=pl.Buffered(1))


def _mod_spec():
    return pl.BlockSpec((1, 3, D), lambda i: (jnp.minimum(i, 1), 0, 0))


def _norm_mod(x, g, shift, scale):
    ms = jnp.mean(x * x, axis=-1, keepdims=True)
    y = x * lax.rsqrt(ms + EPS) * g
    return y * (1.0 + scale) + shift


ADA_TN = 1024


def _ada_body(c_ref, w_ref, b_ref, o_ref):
    c = c_ref[...]
    cs = c * jax.nn.sigmoid(c)
    for r in range(2):
        cb = jnp.broadcast_to(cs[:, r:r + 1], (D, LANES))
        outs = []
        for j in range(ADA_TN // LANES):
            w = w_ref[0, :, j * LANES:(j + 1) * LANES]
            p = (w * cb).reshape(D // SUBLANES, SUBLANES, LANES).sum(axis=0)
            outs.append(p.sum(axis=0, keepdims=True))
        o_ref[0, r:r + 1, :] = jnp.concatenate(outs, axis=1) + b_ref[0]


def _ada_call(c2, ada_w, ada_b):
    return pl.pallas_call(
        _ada_body,
        grid=(DEPTH, 6 * D // ADA_TN),
        in_specs=[pl.BlockSpec((D, 2), lambda l, j: (0, 0)),
                  pl.BlockSpec((1, D, ADA_TN), lambda l, j: (l, 0, j)),
                  pl.BlockSpec((1, 1, ADA_TN), lambda l, j: (l, 0, j))],
        out_specs=pl.BlockSpec((1, 2, ADA_TN), lambda l, j: (l, 0, j)),
        out_shape=jax.ShapeDtypeStruct((DEPTH, 2, 6 * D), F32),
        compiler_params=_cparams(32, ("arbitrary", "arbitrary")),
        name="ada_mod",
    )(c2, ada_w, ada_b.reshape(DEPTH, 1, 6 * D))


PROJ_CH = 512


def _rope_piece(piece, cos, sin):
    lane = lax.broadcasted_iota(jnp.int32, piece.shape, 1)
    first = (lane & 16) == 0
    partner = jnp.where(first, pltpu.roll(piece, LANES - 16, 1), pltpu.roll(piece, 16, 1))
    return piece * cos + partner * sin


def _proj_body(*refs, mode, n_rope, n_q):
    if mode == "rope":
        x_ref, g_ref, mod_ref, w_ref, cos_ref, sin_ref, o_ref = refs
    else:
        x_ref, g_ref, mod_ref, w_ref, o_ref = refs
    a = _norm_mod(x_ref[...], g_ref[...], mod_ref[0, 0:1, :], mod_ref[0, 1:2, :]).astype(BF16)
    nout = w_ref.shape[1]
    for c in range(nout // PROJ_CH):
        acc = jnp.dot(a, w_ref[:, c * PROJ_CH:(c + 1) * PROJ_CH], preferred_element_type=F32)
        if mode == "plain":
            o_ref[:, c * PROJ_CH:(c + 1) * PROJ_CH] = acc
            continue
        for k in range(PROJ_CH // LANES):
            col0 = c * PROJ_CH + k * LANES
            piece = acc[:, k * LANES:(k + 1) * LANES]
            if col0 < n_rope:
                piece = _rope_piece(piece, cos_ref[...], sin_ref[...])
            if col0 < n_q:
                piece = piece * SQRT_SCALE
            o_ref[col0 // LANES] = piece.astype(BF16)


def _proj_call(h, g, mod, w, *, mode, rope=None, n_rope=0, n_q=0, name):
    n = h.shape[0]
    nout = w.shape[1]
    in_specs = [pl.BlockSpec((TM, D), lambda i: (i, 0)),
                pl.BlockSpec((1, D), lambda i: (0, 0)),
                _mod_spec(),
                _resident((D, nout), lambda i: (0, 0))]
    args = [h, g.reshape(1, D), mod[:, 0:3], w]
    if mode == "rope":
        in_specs += [pl.BlockSpec((TM, LANES), lambda i: (i, 0))] * 2
        args += list(rope)
    if mode == "plain":
        out_spec = pl.BlockSpec((TM, nout), lambda i: (i, 0))
        out_shape = jax.ShapeDtypeStruct((n, nout), F32)
        out_bytes = TM * nout * 4
    else:
        out_spec = pl.BlockSpec((nout // LANES, TM, LANES), lambda i: (0, i, 0))
        out_shape = jax.ShapeDtypeStruct((nout // LANES, n, LANES), BF16)
        out_bytes = TM * nout * 2
    vmem = (D * nout * 2 + 2 * TM * D * 4 + 2 * out_bytes) / MIB + 12
    return pl.pallas_call(
        functools.partial(_proj_body, mode=mode, n_rope=n_rope, n_q=n_q),
        grid=(n // TM,),
        in_specs=in_specs,
        out_specs=out_spec,
        out_shape=out_shape,
        compiler_params=_cparams(vmem),
        name=name,
    )(*args)


def _lane_lo(shape):
    return lax.broadcasted_iota(jnp.int32, shape, len(shape) - 1) < HEAD_DIM


def _attn_a_body(sink_ref, q_ref, kp_ref, kc_ref, kn_ref, kx_ref, vp_ref, vc_ref, vn_ref, vx_ref, o_ref, *, nb):
    b = pl.program_id(0)
    first_lat = CTX // QB
    is_lat = b >= first_lat
    prev_ok = b >= first_lat + 1
    next_ok = jnp.logical_and(is_lat, b <= nb - 2)
    nloc = 3 * QB
    nkeys = nloc + CTX
    qi = lax.broadcasted_iota(jnp.int32, (QB, nkeys), 0)
    kj = lax.broadcasted_iota(jnp.int32, (QB, nkeys), 1)
    rel = qi + WINDOW - kj
    band = jnp.abs(rel) <= WINDOW
    seg_ok = jnp.where(kj < QB, prev_ok.astype(jnp.int32),
                       jnp.where(kj < 2 * QB, is_lat.astype(jnp.int32), next_ok.astype(jnp.int32)))
    ok = jnp.logical_or(kj >= nloc, jnp.logical_and(band, seg_ok > 0))
    mask_add = jnp.where(ok, 0.0, NEG)
    lo = _lane_lo((QB, LANES))
    zero = jnp.zeros((QB, LANES), BF16)
    for kvh in range(A_KV_HEADS):
        keys = jnp.concatenate([kp_ref[kvh], kc_ref[kvh], kn_ref[kvh], kx_ref[kvh]], axis=0)
        vals = jnp.concatenate([vp_ref[kvh], vc_ref[kvh], vn_ref[kvh], vx_ref[kvh]], axis=0)
        parts = []
        sinks = []
        for p in range(A_GROUP // 2):
            q2 = q_ref[kvh * (A_GROUP // 2) + p]
            parts += [jnp.where(lo, q2, zero), jnp.where(lo, zero, q2)]
            for s in range(2):
                sinks.append(jnp.full((1, QB, 1), sink_ref[kvh * A_GROUP + 2 * p + s], F32))
        qs = jnp.concatenate(parts, axis=0)
        sink = jnp.concatenate(sinks, axis=0)
        s3 = lax.dot_general(qs, keys, (((1,), (1,)), ((), ())), preferred_element_type=F32)
        s3 = s3.reshape(A_GROUP, QB, nkeys) + mask_add[None]
        m = jnp.maximum(jnp.max(s3, axis=-1, keepdims=True), sink)
        e = jnp.exp(s3 - m)
        denom = jnp.sum(e, axis=-1, keepdims=True) + jnp.exp(sink - m)
        o = jnp.dot(e.astype(BF16).reshape(A_GROUP * QB, nkeys), vals, preferred_element_type=F32)
        o = o.reshape(A_GROUP, QB, LANES) / denom
        for p in range(A_GROUP // 2):
            o_ref[kvh * (A_GROUP // 2) + p] = jnp.where(lo, o[2 * p], o[2 * p + 1]).astype(BF16)


def _attn_a_call(qkv3, sink):
    n = qkv3.shape[1]
    nb = n // QB
    first_lat = CTX // QB
    kblk, vblk = HP // A_KV_HEADS, HP // A_KV_HEADS + 1

    def kv_specs(blk):
        return [pl.BlockSpec((A_KV_HEADS, QB, LANES), lambda b: (blk, jnp.maximum(b - 1, first_lat), 0)),
                pl.BlockSpec((A_KV_HEADS, QB, LANES), lambda b: (blk, b, 0)),
                pl.BlockSpec((A_KV_HEADS, QB, LANES), lambda b: (blk, jnp.minimum(b + 1, nb - 1), 0)),
                pl.BlockSpec((A_KV_HEADS, CTX, LANES), lambda b: (blk, 0, 0))]

    return pl.pallas_call(
        functools.partial(_attn_a_body, nb=nb),
        grid=(nb,),
        in_specs=[pl.BlockSpec(memory_space=pltpu.SMEM),
                  pl.BlockSpec((HP, QB, LANES), lambda b: (0, b, 0))] + kv_specs(kblk) + kv_specs(vblk),
        out_specs=pl.BlockSpec((HP, QB, LANES), lambda b: (0, b, 0)),
        out_shape=jax.ShapeDtypeStruct((HP, n, LANES), BF16),
        compiler_params=_cparams(40),
        name="attn_window",
    )(sink, qkv3, *([qkv3] * 8))


NA_TAB = 2 * NB_KH
RPB_W = 2 * NB_KW - 1
RPB_H = 2 * NB_KH - 1


def _na_table_body(rpb_ref, o_ref):
    h = pl.program_id(0)
    shape = (GRID_W, LANES)
    c = lax.broadcasted_iota(jnp.int32, shape, 0)
    lane = lax.broadcasted_iota(jnp.int32, shape, 1)
    kc = lane & (GRID_W - 1)
    hi = lane >= GRID_W
    cs = jnp.clip(c - NB_KW // 2, 0, GRID_W - NB_KW)
    colok = jnp.logical_and(kc >= cs, kc < cs + NB_KW)
    diff = kc - c + (NB_KW - 1)
    neg = jnp.full(shape, NEG, F32)
    rows = [neg]
    for d in range(RPB_H):
        acc = neg
        for j in range(RPB_W):
            acc = jnp.where(diff == j, rpb_ref[h * (RPB_H * RPB_W) + d * RPB_W + j], acc)
        rows.append(jnp.where(colok, acc, NEG))
    rows.append(neg)
    for t in range(NA_TAB):
        o_ref[0, t] = jnp.where(hi, rows[t + 1], rows[t])


def _na_table_call(rpb):
    return pl.pallas_call(
        _na_table_body,
        grid=(HEADS,),
        in_specs=[pl.BlockSpec(memory_space=pltpu.SMEM)],
        out_specs=pl.BlockSpec((1, NA_TAB, GRID_W, LANES), lambda h: (h, 0, 0, 0)),
        out_shape=jax.ShapeDtypeStruct((HEADS, NA_TAB, GRID_W, LANES), F32),
        compiler_params=_cparams(16),
        name="na_bias_table",
    )(rpb.reshape(-1))


def _attn_b_body(tab_ref, q_ref, k0, k1, k2, k3, k4, kx_ref, v0, v1, v2, v3, v4, vx_ref, o_ref, *, nbl):
    b = pl.program_id(0)
    first_lat = CTX // QB
    is_lat = b >= first_lat
    bl = jnp.maximum(b - first_lat, 0)
    ws = jnp.clip(bl - 2, 0, nbl - NA_KBLK)
    n_rows = 2 * nbl
    nloc = NA_KBLK * QB
    rs = [jnp.clip(2 * bl + qr - NB_KH // 2, 0, n_rows - NB_KH) for qr in range(2)]
    tidx = [[jnp.clip(2 * (ws + j) - (2 * bl + qr) + (NB_KH - 1), -1, NA_TAB - 2) + 1 for j in range(NA_KBLK)]
            for qr in range(2)]
    qrow = lax.broadcasted_iota(jnp.int32, (QB, nloc), 0)
    kcol = lax.broadcasted_iota(jnp.int32, (QB, nloc), 1)
    krow = 2 * ws + jnp.right_shift(kcol, 6)
    rs_q = jnp.where(qrow < GRID_W, rs[0], rs[1])
    row_ok = jnp.logical_and(jnp.logical_and(krow >= rs_q, krow < rs_q + NB_KH), is_lat)
    row_ok2 = jnp.concatenate([row_ok, row_ok], axis=0)
    lo = _lane_lo((QB, LANES))
    zero = jnp.zeros((QB, LANES), BF16)
    kl = (k0, k1, k2, k3, k4)
    vl = (v0, v1, v2, v3, v4)

    def pair(hp, carry):
        q2 = q_ref[hp]
        qs = jnp.concatenate([jnp.where(lo, q2, zero), jnp.where(lo, zero, q2)], axis=0)
        keys = jnp.concatenate([r[hp] for r in kl] + [kx_ref[hp]], axis=0)
        vals = jnp.concatenate([r[hp] for r in vl] + [vx_ref[hp]], axis=0)
        s = lax.dot_general(qs, keys, (((1,), (1,)), ((), ())), preferred_element_type=F32)
        bias = jnp.concatenate(
            [jnp.concatenate([tab_ref[2 * hp + hh, tidx[qr][j]] for j in range(NA_KBLK)], axis=1)
             for hh in range(2) for qr in range(2)], axis=0)
        s_loc = jnp.where(row_ok2, s[:, :nloc] + bias, NEG)
        s_ctx = s[:, nloc:]
        m = jnp.maximum(jnp.max(s_loc, axis=-1, keepdims=True), jnp.max(s_ctx, axis=-1, keepdims=True))
        e_loc = jnp.exp(s_loc - m)
        e_ctx = jnp.exp(s_ctx - m)
        denom = jnp.sum(e_loc, axis=-1, keepdims=True) + jnp.sum(e_ctx, axis=-1, keepdims=True)
        p = jnp.concatenate([e_loc, e_ctx], axis=1).astype(BF16)
        o = jnp.dot(p, vals, preferred_element_type=F32) / denom
        o_ref[hp] = jnp.where(lo, o[:QB], o[QB:]).astype(BF16)
        return carry

    lax.fori_loop(0, HP, pair, 0)


def _attn_b_call(qkv3, table):
    n = qkv3.shape[1]
    nb = n // QB
    first_lat = CTX // QB
    nbl = nb - first_lat

    def win(blk, i):
        return pl.BlockSpec(
            (HP, QB, LANES),
            lambda b: (blk, first_lat + jnp.clip(jnp.maximum(b - first_lat, 0) - 2, 0, nbl - NA_KBLK) + i, 0))

    def kv_specs(blk):
        return [win(blk, i) for i in range(NA_KBLK)] + [pl.BlockSpec((HP, CTX, LANES), lambda b: (blk, 0, 0))]

    return pl.pallas_call(
        functools.partial(_attn_b_body, nbl=nbl),
        grid=(nb,),
        in_specs=[_resident((HEADS, NA_TAB, GRID_W, LANES), lambda b: (0, 0, 0, 0)),
                  pl.BlockSpec((HP, QB, LANES), lambda b: (0, b, 0))] + kv_specs(1) + kv_specs(2),
        out_specs=pl.BlockSpec((HP, QB, LANES), lambda b: (0, b, 0)),
        out_shape=jax.ShapeDtypeStruct((HP, n, LANES), BF16),
        compiler_params=_cparams(52),
        name="attn_neighbourhood",
    )(table, qkv3, *([qkv3] * 12))


def _oproj_body(*refs, mode):
    if mode == "attn":
        o_ref, w_ref, h_ref, mod_ref, out_ref = refs
        z = jnp.concatenate([o_ref[c] for c in range(HP)], axis=1)
    else:
        ug_ref, hs_ref, w_ref, h_ref, mod_ref, out_ref = refs
        z = (jax.nn.gelu(ug_ref[...]) * hs_ref[...]).astype(BF16)
    y = jnp.dot(z, w_ref[...], preferred_element_type=F32)
    out_ref[...] = h_ref[...] + mod_ref[0, 2:3, :] * y


def _oproj_call(mix_in, w, h, mod, *, mode, name):
    n = h.shape[0]
    kdim = w.shape[0]
    if mode == "attn":
        in_specs = [pl.BlockSpec((HP, TM, LANES), lambda i: (0, i, 0))]
        args = [mix_in]
    else:
        u, hs = mix_in
        in_specs = [pl.BlockSpec((TM, D_RNN), lambda i: (i, 0)), pl.BlockSpec((TM, D_RNN), lambda i: (i, 0))]
        args = [u, hs]
    in_specs += [_resident((kdim, D), lambda i: (0, 0)), pl.BlockSpec((TM, D), lambda i: (i, 0)), _mod_spec()]
    return pl.pallas_call(
        functools.partial(_oproj_body, mode=mode),
        grid=(n // TM,),
        in_specs=in_specs,
        out_specs=pl.BlockSpec((TM, D), lambda i: (i, 0)),
        out_shape=jax.ShapeDtypeStruct((n, D), F32),
        compiler_params=_cparams(48),
        name=name,
    )(*args, w, h, mod[:, 0:3])


def _shift_rows(x, s, fill, reverse):
    t = x.shape[0]
    row = lax.broadcasted_iota(jnp.int32, x.shape, 0)
    if reverse:
        return jnp.where(row >= t - s, fill, pltpu.roll(x, t - s, 0))
    return jnp.where(row < s, fill, pltpu.roll(x, s, 0))


def _chunk_scan(a, b, reverse):
    s = 1
    while s < a.shape[0]:
        a_s = _shift_rows(a, s, 1.0, reverse)
        b_s = _shift_rows(b, s, 0.0, reverse)
        b = a * b_s + b
        a = a * a_s
        s *= 2
    return a, b


def _lru_body(x_ref, cw_ref, cb_ref, wa_ref, wx_ref, ba_ref, bx_ref, lam_ref, o_ref, *, nchunks):
    half = pl.program_id(0) % 2
    n = nchunks * SCAN_TC
    halo = SUBLANES
    cw = cw_ref[...]
    cb = cb_ref[...]
    sp = jax.nn.softplus(-lam_ref[...])

    def conv_chunk(ci):
        r0 = pl.multiple_of(ci * SCAN_TC, SCAN_TC)
        lat_first = CTX // SCAN_TC
        prev_ok = jnp.logical_and(ci != 0, ci != lat_first).astype(F32)
        next_ok = jnp.logical_and(ci != lat_first - 1, ci != nchunks - 1).astype(F32)
        p0 = pl.multiple_of(jnp.maximum(r0 - halo, 0), halo)
        n0 = pl.multiple_of(jnp.minimum(r0 + SCAN_TC, n - halo), halo)
        xe = jnp.concatenate([x_ref[pl.ds(p0, halo), :] * prev_ok,
                              x_ref[pl.ds(r0, SCAN_TC), :],
                              x_ref[pl.ds(n0, halo), :] * next_ok], axis=0)
        y = cb
        for j in range(CONV_W):
            off = halo + j - CONV_LEFT
            y = y + xe[off:off + SCAN_TC, :] * cw[j:j + 1, :]
        return r0, y

    def direction(d, ci, h_in):
        r0, xc = conv_chunk(ci)
        xb = xc.astype(BF16)
        r = jax.nn.sigmoid(jnp.dot(xb, wa_ref[d, 0], preferred_element_type=F32) + ba_ref[d:d + 1, :])
        ig = jax.nn.sigmoid(jnp.dot(xb, wx_ref[d, 0], preferred_element_type=F32) + bx_ref[d:d + 1, :])
        xs = jnp.where(half == 0, xc[:, :LANES], xc[:, LANES:])
        log_a = -LRU_C * r * sp[d:d + 1, :]
        a = jnp.exp(log_a)
        bb = jnp.sqrt(1.0 - a * a) * (ig * xs)
        a_cum, h = _chunk_scan(a, bb, reverse=(d == 1))
        h = h + a_cum * h_in
        h_out = h[SCAN_TC - 1:SCAN_TC, :] if d == 0 else h[0:1, :]
        return r0, h, h_out

    def fwd(ci, h_in):
        r0, h, h_out = direction(0, ci, h_in)
        o_ref[pl.ds(r0, SCAN_TC), :] = h
        return h_out

    def bwd(ci, h_in):
        r0, h, h_out = direction(1, ci, h_in)
        o_ref[pl.ds(r0, SCAN_TC), :] += h
        return h_out

    h0 = jnp.zeros((1, LANES), F32)
    lax.fori_loop(0, nchunks, fwd, h0)
    nctx = CTX // SCAN_TC
    hc = lax.fori_loop(0, nctx, lambda k, hh: bwd(nctx - 1 - k, hh), h0)
    lax.fori_loop(0, nchunks - nctx, lambda k, hh: bwd(nchunks - 1 - k, hh), hc)


def _lru_call(u, conv_w, conv_b, wa, wx, ba, bx, lam):
    n = u.shape[0]
    nsteps = D_RNN // LANES
    return pl.pallas_call(
        functools.partial(_lru_body, nchunks=n // SCAN_TC),
        grid=(nsteps,),
        in_specs=[_resident((n, RNN_BW), lambda m: (0, RNN_BLOCKS + m // 2)),
                  pl.BlockSpec((CONV_W, RNN_BW), lambda m: (0, m // 2)),
                  pl.BlockSpec((1, RNN_BW), lambda m: (0, m // 2)),
                  pl.BlockSpec((2, 1, RNN_BW, LANES), lambda m: (0, m // 2, 0, m % 2)),
                  pl.BlockSpec((2, 1, RNN_BW, LANES), lambda m: (0, m // 2, 0, m % 2)),
                  pl.BlockSpec((2, LANES), lambda m: (0, m)),
                  pl.BlockSpec((2, LANES), lambda m: (0, m)),
                  pl.BlockSpec((2, LANES), lambda m: (0, m))],
        out_specs=pl.BlockSpec((n, LANES), lambda m: (0, m)),
        out_shape=jax.ShapeDtypeStruct((n, D_RNN), F32),
        compiler_params=_cparams(52),
        name="rglru_scan",
    )(u, conv_w, conv_b.reshape(1, D_RNN), wa, wx, ba, bx, lam)


def _router_body(h_ref, g_ref, mod_ref, wr_ref, br_ref, f_ref, r_ref):
    f = _norm_mod(h_ref[...], g_ref[...], mod_ref[0, 0:1, :], mod_ref[0, 1:2, :])
    f_ref[...] = f
    logits = jnp.dot(f, wr_ref[...], preferred_element_type=F32, precision=lax.Precision.HIGHEST) + br_ref[...]
    lane = lax.broadcasted_iota(jnp.int32, logits.shape, 1)
    ninf = -jnp.inf
    is_g = lane < N_GROUPS
    gl = jnp.where(is_g, logits, ninf)
    gmax = jnp.max(gl, axis=-1, keepdims=True)
    gsel = jnp.min(jnp.where(gl == gmax, lane, LANES), axis=-1, keepdims=True)
    gsum = jnp.sum(jnp.where(is_g, jnp.exp(gl - gmax), 0.0), axis=-1, keepdims=True)
    g_w = 1.0 / gsum
    e_lo = N_GROUPS + EXP_PER_GROUP * gsel
    in_grp = jnp.logical_and(lane >= e_lo, lane < e_lo + EXP_PER_GROUP)
    el = jnp.where(in_grp, logits, ninf)
    m1 = jnp.max(el, axis=-1, keepdims=True)
    i1 = jnp.min(jnp.where(el == m1, lane, LANES), axis=-1, keepdims=True)
    el2 = jnp.where(lane == i1, ninf, el)
    m2 = jnp.max(el2, axis=-1, keepdims=True)
    i2 = jnp.min(jnp.where(el2 == m2, lane, LANES), axis=-1, keepdims=True)
    t = jnp.exp(m2 - m1)
    w1 = g_w / (1.0 + t)
    w2 = g_w * t / (1.0 + t)
    out = jnp.where(lane == 0, (i1 - N_GROUPS).astype(F32),
                    jnp.where(lane == 1, (i2 - N_GROUPS).astype(F32),
                              jnp.where(lane == 2, w1, jnp.where(lane == 3, w2, 0.0))))
    r_ref[...] = out


def _router_call(h, g, mod, wr, br):
    n = h.shape[0]
    return pl.pallas_call(
        _router_body,
        grid=(n // TM,),
        in_specs=[pl.BlockSpec((TM, D), lambda i: (i, 0)),
                  pl.BlockSpec((1, D), lambda i: (0, 0)),
                  _mod_spec(),
                  pl.BlockSpec((D, LANES), lambda i: (0, 0)),
                  pl.BlockSpec((1, LANES), lambda i: (0, 0))],
        out_specs=[pl.BlockSpec((TM, D), lambda i: (i, 0)), pl.BlockSpec((TM, LANES), lambda i: (i, 0))],
        out_shape=[jax.ShapeDtypeStruct((n, D), F32), jax.ShapeDtypeStruct((n, LANES), F32)],
        compiler_params=_cparams(32),
        name="moe_router",
    )(h, g.reshape(1, D), mod[:, 3:6], wr, br)


GATHER_RB = 256


def _gather_body(idx_ref, src_ref, out_ref, sem):
    base = pl.program_id(0) * GATHER_RB

    def issue(j, carry):
        t = idx_ref[base + j]
        pltpu.make_async_copy(src_ref.at[pl.ds(t, 1)], out_ref.at[pl.ds(base + j, 1)], sem).start()
        return carry

    def drain(j, carry):
        pltpu.make_async_copy(src_ref.at[pl.ds(0, 1)], out_ref.at[pl.ds(0, 1)], sem).wait()
        return carry

    lax.fori_loop(0, GATHER_RB, issue, 0)
    lax.fori_loop(0, GATHER_RB, drain, 0)


def _gather_call(idx, src):
    p = idx.shape[0]
    return pl.pallas_call(
        _gather_body,
        grid_spec=pltpu.PrefetchScalarGridSpec(
            num_scalar_prefetch=1,
            grid=(p // GATHER_RB,),
            in_specs=[pl.BlockSpec(memory_space=pl.ANY)],
            out_specs=pl.BlockSpec(memory_space=pl.ANY),
            scratch_shapes=[pltpu.SemaphoreType.DMA(())]),
        out_shape=jax.ShapeDtypeStruct((p, src.shape[1]), src.dtype),
        compiler_params=_cparams(16),
        name="moe_gather",
    )(idx, src)


def _ffn_up_body(be_ref, nu_ref, x_ref, w_ref, a_ref, wb_ref):
    b = pl.program_id(0)
    changed = jnp.logical_or(b == 0, be_ref[b] != be_ref[jnp.maximum(b - 1, 0)])

    @pl.when(changed)
    def _():
        wb_ref[...] = w_ref[0].astype(BF16)

    @pl.when(b < nu_ref[0])
    def _():
        hgu = jnp.dot(x_ref[...].astype(BF16), wb_ref[...], preferred_element_type=F32)
        g = hgu[:, :D_EXPERT]
        u = hgu[:, D_EXPERT:]
        a_ref[...] = (g * jax.nn.sigmoid(g) * u).astype(BF16)

    @pl.when(b >= nu_ref[0])
    def _():
        a_ref[...] = jnp.zeros(a_ref.shape, BF16)


def _ffn_up_call(block_e, n_used, xg, w_gu):
    p = xg.shape[0]
    return pl.pallas_call(
        _ffn_up_body,
        grid_spec=pltpu.PrefetchScalarGridSpec(
            num_scalar_prefetch=2,
            grid=(p // MOE_MB,),
            in_specs=[pl.BlockSpec((MOE_MB, D), lambda b, be, nu: (b, 0)),
                      pl.BlockSpec((1, D, 2 * D_EXPERT), lambda b, be, nu: (be[b], 0, 0))],
            out_specs=pl.BlockSpec((MOE_MB, D_EXPERT), lambda b, be, nu: (b, 0)),
            scratch_shapes=[pltpu.VMEM((D, 2 * D_EXPERT), BF16)]),
        out_shape=jax.ShapeDtypeStruct((p, D_EXPERT), BF16),
        compiler_params=_cparams(50),
        name="moe_ffn_up",
    )(block_e, n_used, xg, w_gu)


def _ffn_down_body(be_ref, nu_ref, dst_ref, a_ref, w_ref, sw_ref, y_ref, wb_ref, yb_ref, sem):
    b = pl.program_id(0)
    changed = jnp.logical_or(b == 0, be_ref[b] != be_ref[jnp.maximum(b - 1, 0)])

    @pl.when(changed)
    def _():
        wb_ref[...] = w_ref[0].astype(BF16)

    @pl.when(b < nu_ref[0])
    def _():
        yb_ref[...] = jnp.dot(a_ref[...], wb_ref[...], preferred_element_type=F32) * sw_ref[...]
        base = b * MOE_MB
        n_rows = y_ref.shape[0]

        def issue(j, carry):
            dst = dst_ref[base + j]

            @pl.when(dst < n_rows)
            def _():
                pltpu.make_async_copy(yb_ref.at[pl.ds(j, 1)], y_ref.at[pl.ds(dst, 1)], sem).start()
            return carry

        def drain(j, carry):
            @pl.when(dst_ref[base + j] < n_rows)
            def _():
                pltpu.make_async_copy(yb_ref.at[pl.ds(0, 1)], y_ref.at[pl.ds(0, 1)], sem).wait()
            return carry

        lax.fori_loop(0, MOE_MB, issue, 0)
        lax.fori_loop(0, MOE_MB, drain, 0)


def _ffn_down_call(block_e, n_used, slot_dst, act, w_down, slot_w, n_out_rows):
    p = act.shape[0]
    return pl.pallas_call(
        _ffn_down_body,
        grid_spec=pltpu.PrefetchScalarGridSpec(
            num_scalar_prefetch=3,
            grid=(p // MOE_MB,),
            in_specs=[pl.BlockSpec((MOE_MB, D_EXPERT), lambda b, be, nu, ds: (b, 0)),
                      pl.BlockSpec((1, D_EXPERT, D), lambda b, be, nu, ds: (be[b], 0, 0)),
                      pl.BlockSpec((MOE_MB, 1), lambda b, be, nu, ds: (b, 0))],
            out_specs=pl.BlockSpec(memory_space=pl.ANY),
            scratch_shapes=[pltpu.VMEM((D_EXPERT, D), BF16), pltpu.VMEM((MOE_MB, D), F32),
                            pltpu.SemaphoreType.DMA(())]),
        out_shape=jax.ShapeDtypeStruct((n_out_rows, D), F32),
        compiler_params=_cparams(40),
        name="moe_ffn_down",
    )(block_e, n_used, slot_dst, act, w_down, slot_w)


def _combine_body(h_ref, y0_ref, y1_ref, mod_ref, o_ref):
    o_ref[...] = h_ref[...] + mod_ref[0, 2:3, :] * (y0_ref[...] + y1_ref[...])


def _combine_call(h, y, mod):
    n = h.shape[0]
    nt = n // TM
    return pl.pallas_call(
        _combine_body,
        grid=(nt,),
        in_specs=[pl.BlockSpec((TM, D), lambda i: (i, 0)),
                  pl.BlockSpec((TM, D), lambda i: (i, 0)),
                  pl.BlockSpec((TM, D), lambda i: (i + nt, 0)),
                  _mod_spec()],
        out_specs=pl.BlockSpec((TM, D), lambda i: (i, 0)),
        out_shape=jax.ShapeDtypeStruct((n, D), F32),
        compiler_params=_cparams(24),
        name="moe_combine",
    )(h, y, y, mod[:, 3:6])


def _dispatch_plan(route, n):
    a = n * TOP_K
    e_idx = route[:, 0:TOP_K].astype(jnp.int32)
    e_w = route[:, TOP_K:2 * TOP_K]
    flat_e = e_idx.reshape(a)
    order = jnp.argsort(flat_e).astype(jnp.int32)
    counts = jnp.sum(flat_e[:, None] == jnp.arange(N_EXPERTS, dtype=jnp.int32)[None, :], axis=0, dtype=jnp.int32)
    padded = (counts + MOE_MB - 1) // MOE_MB * MOE_MB
    start_sorted = jnp.cumsum(counts) - counts
    end_pad = jnp.cumsum(padded)
    start_pad = end_pad - padded
    n_blocks = -(-(a + N_EXPERTS * (MOE_MB - 1)) // MOE_MB)
    p = n_blocks * MOE_MB
    blk_start = jnp.arange(n_blocks, dtype=jnp.int32) * MOE_MB
    block_e = jnp.minimum(jnp.searchsorted(end_pad, blk_start, side="right"), N_EXPERTS - 1).astype(jnp.int32)
    n_used = (end_pad[-1] // MOE_MB).astype(jnp.int32).reshape(1)
    slot = jnp.arange(p, dtype=jnp.int32)
    slot_e = block_e[slot // MOE_MB]
    j = slot - start_pad[slot_e]
    valid = jnp.logical_and(j < counts[slot_e], slot < end_pad[-1])
    src = order[jnp.clip(start_sorted[slot_e] + j, 0, a - 1)]
    slot_tok = jnp.where(valid, src // TOP_K, 0)
    slot_dst = jnp.where(valid, (src % TOP_K) * n + src // TOP_K, TOP_K * n)
    slot_w = jnp.where(valid, e_w.reshape(a)[src], 0.0)
    return block_e, n_used, slot_tok, slot_dst, slot_w.reshape(p, 1)


def _moe_layer(h, g, mod, wr, br, w_gu, w_down):
    n = h.shape[0]
    f, route = _router_call(h, g, mod, wr, br)
    block_e, n_used, slot_tok, slot_dst, slot_w = _dispatch_plan(route, n)
    xg = _gather_call(slot_tok, f)
    act = _ffn_up_call(block_e, n_used, xg, w_gu)
    y = _ffn_down_call(block_e, n_used, slot_dst, act, w_down, slot_w, TOP_K * n)
    return _combine_call(h, y, mod)


def _final_body(h_ref, g_ref, o_ref):
    x = h_ref[...]
    ms = jnp.mean(x * x, axis=-1, keepdims=True)
    o_ref[...] = x * lax.rsqrt(ms + EPS) * g_ref[...]


def _final_call(h, g, l):
    first = CTX // TM
    return pl.pallas_call(
        _final_body,
        grid=(l // TM,),
        in_specs=[pl.BlockSpec((TM, D), lambda i: (i + first, 0)), pl.BlockSpec((1, D), lambda i: (0, 0))],
        out_specs=pl.BlockSpec((TM, D), lambda i: (i, 0)),
        out_shape=jax.ShapeDtypeStruct((l, D), F32),
        compiler_params=_cparams(16),
        name="final_norm",
    )(h, g.reshape(1, D))


def _rope_tables(l):
    quarter = HEAD_DIM // 4
    inv = ROPE_BASE ** (-jnp.arange(quarter, dtype=F32) / quarter)
    pos = jnp.arange(l, dtype=jnp.int32)
    rows = (pos // GRID_W).astype(F32)
    cols = (pos % GRID_W).astype(F32)
    ang_r = rows[:, None] * inv
    ang_c = cols[:, None] * inv
    cos = jnp.concatenate([jnp.cos(ang_r), jnp.cos(ang_r), jnp.cos(ang_c), jnp.cos(ang_c)], axis=1)
    sin = jnp.concatenate([-jnp.sin(ang_r), jnp.sin(ang_r), -jnp.sin(ang_c), jnp.sin(ang_c)], axis=1)
    cos = jnp.concatenate([jnp.ones((CTX, HEAD_DIM), F32), cos], axis=0)
    sin = jnp.concatenate([jnp.zeros((CTX, HEAD_DIM), F32), sin], axis=0)
    return jnp.tile(cos, (1, 2)), jnp.tile(sin, (1, 2))


def _attn_a_weights(w_qkv):
    nq = HEADS * HEAD_DIM
    nkv = A_KV_HEADS * HEAD_DIM
    wq = w_qkv[:, :nq]
    wk = w_qkv[:, nq:nq + nkv].reshape(D, A_KV_HEADS, 1, HEAD_DIM)
    wv = w_qkv[:, nq + nkv:].reshape(D, A_KV_HEADS, 1, HEAD_DIM)
    dup = lambda w: jnp.broadcast_to(w, (D, A_KV_HEADS, 2, HEAD_DIM)).reshape(D, 2 * nkv)
    return jnp.concatenate([wq, dup(wk), dup(wv)], axis=1).astype(BF16)


def kernel(x, c, ctx, c_ctx, ada_w, ada_b, norm_mix_g, norm_ffn_g, router_group_w, router_group_b,
           router_expert_w, router_expert_b, moe_w_gu, moe_w_down, attn_w_qkv, attn_w_o, attn_sink,
           na_w_qkv, na_w_o, na_rpb, rnn_w_in, rnn_conv_w, rnn_conv_b, rnn_wa, rnn_ba, rnn_wx, rnn_bx,
           rnn_lam, rnn_w_out, final_norm_g):
    batch, l, _ = x.shape
    assert batch == 1 and ctx.shape[1] == CTX and l % (QB * 2) == 0 and l // QB >= NA_KBLK
    h = jnp.concatenate([ctx[0], x[0]], axis=0)
    c2 = jnp.stack([c_ctx, c[0]], axis=1)
    mods = _ada_call(c2, ada_w, ada_b).reshape(DEPTH, 2, 6, D)
    rope = _rope_tables(l)
    pad_r = LANES - N_GROUPS - N_EXPERTS
    for i in range(DEPTH):
        kind, j = i % 3, i // 3
        mod = mods[i]
        if kind == 0:
            w = _attn_a_weights(attn_w_qkv[j])
            nq = HEADS * HEAD_DIM
            qkv3 = _proj_call(h, norm_mix_g[i], mod, w, mode="rope", rope=rope,
                              n_rope=nq + 2 * A_KV_HEADS * HEAD_DIM, n_q=nq, name="proj_window")
            o3 = _attn_a_call(qkv3, attn_sink[j])
            h = _oproj_call(o3, attn_w_o[j].astype(BF16), h, mod, mode="attn", name="oproj_window")
        elif kind == 1:
            qkv3 = _proj_call(h, norm_mix_g[i], mod, na_w_qkv[j].astype(BF16), mode="cols",
                              n_q=HEADS * HEAD_DIM, name="proj_neighbourhood")
            o3 = _attn_b_call(qkv3, _na_table_call(na_rpb[j]))
            h = _oproj_call(o3, na_w_o[j].astype(BF16), h, mod, mode="attn", name="oproj_neighbourhood")
        else:
            u = _proj_call(h, norm_mix_g[i], mod, rnn_w_in[j].astype(BF16), mode="plain", name="proj_rglru")
            hs = _lru_call(u, rnn_conv_w[j], rnn_conv_b[j], rnn_wa[j].astype(BF16), rnn_wx[j].astype(BF16),
                           rnn_ba[j], rnn_bx[j], rnn_lam[j])
            h = _oproj_call((u, hs), rnn_w_out[j].astype(BF16), h, mod, mode="rnn", name="oproj_rglru")
        wr = jnp.concatenate([router_group_w[i], router_expert_w[i], jnp.zeros((D, pad_r), F32)], axis=1)
        br = jnp.concatenate([router_group_b[i], router_expert_b[i], jnp.zeros((pad_r,), F32)]).reshape(1, LANES)
        h = _moe_layer(h, norm_ffn_g[i], mod, wr, br, moe_w_gu[i], moe_w_down[i])
    return _final_call(h, final_norm_g, l)[None]
```

```python
import functools

import jax
import jax.numpy as jnp
from jax import lax
from jax.experimental import pallas as pl
from jax.experimental.pallas import tpu as pltpu

F32 = jnp.float32
BF16 = jnp.bfloat16

D = 2048
DEPTH = 4
GRID_W = 64
CTX = 256
HEADS = 32
HEAD_DIM = 64
A_KV_HEADS = 4
A_GROUP = HEADS // A_KV_HEADS
WINDOW = 128
NB_KH = 8
NB_KW = 16
D_RNN = 2560
RNN_BLOCKS = 10
RNN_BW = D_RNN // RNN_BLOCKS
CONV_W = 4
CONV_LEFT = 2
LRU_C = 8.0
N_GROUPS = 4
EXP_PER_GROUP = 8
N_EXPERTS = N_GROUPS * EXP_PER_GROUP
TOP_K = 2
D_EXPERT = 768
ROPE_BASE = 10000.0
EPS = 1e-6
NEG = -1e30

LANES = 128
SUBLANES = 8
MIB = 1024 * 1024

TM = 256
QB = 128
HP = HEADS // 2
MOE_MB = 256
ROW_TILE = D // LANES
SCAN_TC = 256
NA_KBLK = 5
SQRT_SCALE = HEAD_DIM ** -0.5


def _cparams(vmem_mib, sem=("arbitrary",)):
    return pltpu.CompilerParams(dimension_semantics=sem, vmem_limit_bytes=int(vmem_mib * MIB))


def _resident(block_shape, index_map):
    return pl.BlockSpec(block_shape, index_map, pipeline_mode=pl.Buffered(1))


def _mod_spec():
    return pl.BlockSpec((1, 3, D), lambda i: (jnp.minimum(i, 1), 0, 0))


def _norm_mod(x, g, shift, scale):
    ms = jnp.mean(x * x, axis=-1, keepdims=True)
    y = x * lax.rsqrt(ms + EPS) * g
    return y * (1.0 + scale) + shift


ADA_TN = 1024


def _ada_body(c_ref, w_ref, b_ref, o_ref):
    c = c_ref[...]
    cs = c * jax.nn.sigmoid(c)
    for r in range(2):
        cb = jnp.broadcast_to(cs[:, r:r + 1], (D, LANES))
        outs = []
        for j in range(ADA_TN // LANES):
            w = w_ref[0, :, j * LANES:(j + 1) * LANES]
            p = (w * cb).reshape(D // SUBLANES, SUBLANES, LANES).sum(axis=0)
            outs.append(p.sum(axis=0, keepdims=True))
        o_ref[0, r:r + 1, :] = jnp.concatenate(outs, axis=1) + b_ref[0]


def _ada_call(c2, ada_w, ada_b):
    return pl.pallas_call(
        _ada_body,
        grid=(DEPTH, 6 * D // ADA_TN),
        in_specs=[pl.BlockSpec((D, 2), lambda l, j: (0, 0)),
                  pl.BlockSpec((1, D, ADA_TN), lambda l, j: (l, 0, j)),
                  pl.BlockSpec((1, 1, ADA_TN), lambda l, j: (l, 0, j))],
        out_specs=pl.BlockSpec((1, 2, ADA_TN), lambda l, j: (l, 0, j)),
        out_shape=jax.ShapeDtypeStruct((DEPTH, 2, 6 * D), F32),
        compiler_params=_cparams(32, ("arbitrary", "arbitrary")),
        name="ada_mod",
    )(c2, ada_w, ada_b.reshape(DEPTH, 1, 6 * D))


PROJ_CH = 512


def _rope_piece(piece, cos, sin):
    lane = lax.broadcasted_iota(jnp.int32, piece.shape, 1)
    first = (lane & 16) == 0
    partner = jnp.where(first, pltpu.roll(piece, LANES - 16, 1), pltpu.roll(piece, 16, 1))
    return piece * cos + partner * sin


def _proj_body(*refs, mode, n_rope, n_q):
    if mode == "rope":
        x_ref, g_ref, mod_ref, w_ref, cos_ref, sin_ref, o_ref = refs
    else:
        x_ref, g_ref, mod_ref, w_ref, o_ref = refs
    a = _norm_mod(x_ref[...], g_ref[...], mod_ref[0, 0:1, :], mod_ref[0, 1:2, :]).astype(BF16)
    nout = w_ref.shape[1]
    for c in range(nout // PROJ_CH):
        acc = jnp.dot(a, w_ref[:, c * PROJ_CH:(c + 1) * PROJ_CH], preferred_element_type=F32)
        if mode == "plain":
            o_ref[:, c * PROJ_CH:(c + 1) * PROJ_CH] = acc
            continue
        for k in range(PROJ_CH // LANES):
            col0 = c * PROJ_CH + k * LANES
            piece = acc[:, k * LANES:(k + 1) * LANES]
            if col0 < n_rope:
                piece = _rope_piece(piece, cos_ref[...], sin_ref[...])
            if col0 < n_q:
                piece = piece * SQRT_SCALE
            o_ref[col0 // LANES] = piece.astype(BF16)


def _proj_call(h, g, mod, w, *, mode, rope=None, n_rope=0, n_q=0, name):
    n = h.shape[0]
    nout = w.shape[1]
    in_specs = [pl.BlockSpec((TM, D), lambda i: (i, 0)),
                pl.BlockSpec((1, D), lambda i: (0, 0)),
                _mod_spec(),
                _resident((D, nout), lambda i: (0, 0))]
    args = [h, g.reshape(1, D), mod[:, 0:3], w]
    if mode == "rope":
        in_specs += [pl.BlockSpec((TM, LANES), lambda i: (i, 0))] * 2
        args += list(rope)
    if mode == "plain":
        out_spec = pl.BlockSpec((TM, nout), lambda i: (i, 0))
        out_shape = jax.ShapeDtypeStruct((n, nout), F32)
        out_bytes = TM * nout * 4
    else:
        out_spec = pl.BlockSpec((nout // LANES, TM, LANES), lambda i: (0, i, 0))
        out_shape = jax.ShapeDtypeStruct((nout // LANES, n, LANES), BF16)
        out_bytes = TM * nout * 2
    vmem = (D * nout * 2 + 2 * TM * D * 4 + 2 * out_bytes) / MIB + 12
    return pl.pallas_call(
        functools.partial(_proj_body, mode=mode, n_rope=n_rope, n_q=n_q),
        grid=(n // TM,),
        in_specs=in_specs,
        out_specs=out_spec,
        out_shape=out_shape,
        compiler_params=_cparams(vmem),
        name=name,
    )(*args)


def _lane_lo(shape):
    return lax.broadcasted_iota(jnp.int32, shape, len(shape) - 1) < HEAD_DIM


def _attn_a_body(sink_ref, q_ref, kp_ref, kc_ref, kn_ref, kx_ref, vp_ref, vc_ref, vn_ref, vx_ref, o_ref, *, nb):
    b = pl.program_id(0)
    first_lat = CTX // QB
    is_lat = b >= first_lat
    prev_ok = b >= first_lat + 1
    next_ok = jnp.logical_and(is_lat, b <= nb - 2)
    nloc = 3 * QB
    nkeys = nloc + CTX
    qi = lax.broadcasted_iota(jnp.int32, (QB, nkeys), 0)
    kj = lax.broadcasted_iota(jnp.int32, (QB, nkeys), 1)
    rel = qi + WINDOW - kj
    band = jnp.abs(rel) <= WINDOW
    seg_ok = jnp.where(kj < QB, prev_ok.astype(jnp.int32),
                       jnp.where(kj < 2 * QB, is_lat.astype(jnp.int32), next_ok.astype(jnp.int32)))
    ok = jnp.logical_or(kj >= nloc, jnp.logical_and(band, seg_ok > 0))
    mask_add = jnp.where(ok, 0.0, NEG)
    lo = _lane_lo((QB, LANES))
    zero = jnp.zeros((QB, LANES), BF16)
    for kvh in range(A_KV_HEADS):
        keys = jnp.concatenate([kp_ref[kvh], kc_ref[kvh], kn_ref[kvh], kx_ref[kvh]], axis=0)
        vals = jnp.concatenate([vp_ref[kvh], vc_ref[kvh], vn_ref[kvh], vx_ref[kvh]], axis=0)
        parts = []
        sinks = []
        for p in range(A_GROUP // 2):
            q2 = q_ref[kvh * (A_GROUP // 2) + p]
            parts += [jnp.where(lo, q2, zero), jnp.where(lo, zero, q2)]
            for s in range(2):
                sinks.append(jnp.full((1, QB, 1), sink_ref[kvh * A_GROUP + 2 * p + s], F32))
        qs = jnp.concatenate(parts, axis=0)
        sink = jnp.concatenate(sinks, axis=0)
        s3 = lax.dot_general(qs, keys, (((1,), (1,)), ((), ())), preferred_element_type=F32)
        s3 = s3.reshape(A_GROUP, QB, nkeys) + mask_add[None]
        m = jnp.maximum(jnp.max(s3, axis=-1, keepdims=True), sink)
        e = jnp.exp(s3 - m)
        denom = jnp.sum(e, axis=-1, keepdims=True) + jnp.exp(sink - m)
        o = jnp.dot(e.astype(BF16).reshape(A_GROUP * QB, nkeys), vals, preferred_element_type=F32)
        o = o.reshape(A_GROUP, QB, LANES) / denom
        for p in range(A_GROUP // 2):
            o_ref[kvh * (A_GROUP // 2) + p] = jnp.where(lo, o[2 * p], o[2 * p + 1]).astype(BF16)


def _attn_a_call(qkv3, sink):
    n = qkv3.shape[1]
    nb = n // QB
    first_lat = CTX // QB
    kblk, vblk = HP // A_KV_HEADS, HP // A_KV_HEADS + 1

    def kv_specs(blk):
        return [pl.BlockSpec((A_KV_HEADS, QB, LANES), lambda b: (blk, jnp.maximum(b - 1, first_lat), 0)),
                pl.BlockSpec((A_KV_HEADS, QB, LANES), lambda b: (blk, b, 0)),
                pl.BlockSpec((A_KV_HEADS, QB, LANES), lambda b: (blk, jnp.minimum(b + 1, nb - 1), 0)),
                pl.BlockSpec((A_KV_HEADS, CTX, LANES), lambda b: (blk, 0, 0))]

    return pl.pallas_call(
        functools.partial(_attn_a_body, nb=nb),
        grid=(nb,),
        in_specs=[pl.BlockSpec(memory_space=pltpu.SMEM),
                  pl.BlockSpec((HP, QB, LANES), lambda b: (0, b, 0))] + kv_specs(kblk) + kv_specs(vblk),
        out_specs=pl.BlockSpec((HP, QB, LANES), lambda b: (0, b, 0)),
        out_shape=jax.ShapeDtypeStruct((HP, n, LANES), BF16),
        compiler_params=_cparams(40),
        name="attn_window",
    )(sink, qkv3, *([qkv3] * 8))


NA_TAB = 2 * NB_KH
RPB_W = 2 * NB_KW - 1
RPB_H = 2 * NB_KH - 1


def _na_table_body(rpb_ref, o_ref):
    h = pl.program_id(0)
    shape = (GRID_W, LANES)
    c = lax.broadcasted_iota(jnp.int32, shape, 0)
    lane = lax.broadcasted_iota(jnp.int32, shape, 1)
    kc = lane & (GRID_W - 1)
    hi = lane >= GRID_W
    cs = jnp.clip(c - NB_KW // 2, 0, GRID_W - NB_KW)
    colok = jnp.logical_and(kc >= cs, kc < cs + NB_KW)
    diff = kc - c + (NB_KW - 1)
    neg = jnp.full(shape, NEG, F32)
    rows = [neg]
    for d in range(RPB_H):
        acc = neg
        for j in range(RPB_W):
            acc = jnp.where(diff == j, rpb_ref[h * (RPB_H * RPB_W) + d * RPB_W + j], acc)
        rows.append(jnp.where(colok, acc, NEG))
    rows.append(neg)
    for t in range(NA_TAB):
        o_ref[0, t] = jnp.where(hi, rows[t + 1], rows[t])


def _na_table_call(rpb):
    return pl.pallas_call(
        _na_table_body,
        grid=(HEADS,),
        in_specs=[pl.BlockSpec(memory_space=pltpu.SMEM)],
        out_specs=pl.BlockSpec((1, NA_TAB, GRID_W, LANES), lambda h: (h, 0, 0, 0)),
        out_shape=jax.ShapeDtypeStruct((HEADS, NA_TAB, GRID_W, LANES), F32),
        compiler_params=_cparams(16),
        name="na_bias_table",
    )(rpb.reshape(-1))


def _attn_b_body(tab_ref, q_ref, k0, k1, k2, k3, k4, kx_ref, v0, v1, v2, v3, v4, vx_ref, o_ref, *, nbl):
    b = pl.program_id(0)
    first_lat = CTX // QB
    is_lat = b >= first_lat
    bl = jnp.maximum(b - first_lat, 0)
    ws = jnp.clip(bl - 2, 0, nbl - NA_KBLK)
    n_rows = 2 * nbl
    nloc = NA_KBLK * QB
    rs = [jnp.clip(2 * bl + qr - NB_KH // 2, 0, n_rows - NB_KH) for qr in range(2)]
    tidx = [[jnp.clip(2 * (ws + j) - (2 * bl + qr) + (NB_KH - 1), -1, NA_TAB - 2) + 1 for j in range(NA_KBLK)]
            for qr in range(2)]
    qrow = lax.broadcasted_iota(jnp.int32, (QB, nloc), 0)
    kcol = lax.broadcasted_iota(jnp.int32, (QB, nloc), 1)
    krow = 2 * ws + jnp.right_shift(kcol, 6)
    rs_q = jnp.where(qrow < GRID_W, rs[0], rs[1])
    row_ok = jnp.logical_and(jnp.logical_and(krow >= rs_q, krow < rs_q + NB_KH), is_lat)
    row_ok2 = jnp.concatenate([row_ok, row_ok], axis=0)
    lo = _lane_lo((QB, LANES))
    zero = jnp.zeros((QB, LANES), BF16)
    kl = (k0, k1, k2, k3, k4)
    vl = (v0, v1, v2, v3, v4)

    def pair(hp, carry):
        q2 = q_ref[hp]
        qs = jnp.concatenate([jnp.where(lo, q2, zero), jnp.where(lo, zero, q2)], axis=0)
        keys = jnp.concatenate([r[hp] for r in kl] + [kx_ref[hp]], axis=0)
        vals = jnp.concatenate([r[hp] for r in vl] + [vx_ref[hp]], axis=0)
        s = lax.dot_general(qs, keys, (((1,), (1,)), ((), ())), preferred_element_type=F32)
        bias = jnp.concatenate(
            [jnp.concatenate([tab_ref[2 * hp + hh, tidx[qr][j]] for j in range(NA_KBLK)], axis=1)
             for hh in range(2) for qr in range(2)], axis=0)
        s_loc = jnp.where(row_ok2, s[:, :nloc] + bias, NEG)
        s_ctx = s[:, nloc:]
        m = jnp.maximum(jnp.max(s_loc, axis=-1, keepdims=True), jnp.max(s_ctx, axis=-1, keepdims=True))
        e_loc = jnp.exp(s_loc - m)
        e_ctx = jnp.exp(s_ctx - m)
        denom = jnp.sum(e_loc, axis=-1, keepdims=True) + jnp.sum(e_ctx, axis=-1, keepdims=True)
        p = jnp.concatenate([e_loc, e_ctx], axis=1).astype(BF16)
        o = jnp.dot(p, vals, preferred_element_type=F32) / denom
        o_ref[hp] = jnp.where(lo, o[:QB], o[QB:]).astype(BF16)
        return carry

    lax.fori_loop(0, HP, pair, 0)


def _attn_b_call(qkv3, table):
    n = qkv3.shape[1]
    nb = n // QB
    first_lat = CTX // QB
    nbl = nb - first_lat

    def win(blk, i):
        return pl.BlockSpec(
            (HP, QB, LANES),
            lambda b: (blk, first_lat + jnp.clip(jnp.maximum(b - first_lat, 0) - 2, 0, nbl - NA_KBLK) + i, 0))

    def kv_specs(blk):
        return [win(blk, i) for i in range(NA_KBLK)] + [pl.BlockSpec((HP, CTX, LANES), lambda b: (blk, 0, 0))]

    return pl.pallas_call(
        functools.partial(_attn_b_body, nbl=nbl),
        grid=(nb,),
        in_specs=[_resident((HEADS, NA_TAB, GRID_W, LANES), lambda b: (0, 0, 0, 0)),
                  pl.BlockSpec((HP, QB, LANES), lambda b: (0, b, 0))] + kv_specs(1) + kv_specs(2),
        out_specs=pl.BlockSpec((HP, QB, LANES), lambda b: (0, b, 0)),
        out_shape=jax.ShapeDtypeStruct((HP, n, LANES), BF16),
        compiler_params=_cparams(52),
        name="attn_neighbourhood",
    )(table, qkv3, *([qkv3] * 12))


def _oproj_body(*refs, mode):
    if mode == "attn":
        o_ref, w_ref, h_ref, mod_ref, out_ref = refs
        z = jnp.concatenate([o_ref[c] for c in range(HP)], axis=1)
    else:
        ug_ref, hs_ref, w_ref, h_ref, mod_ref, out_ref = refs
        z = (jax.nn.gelu(ug_ref[...]) * hs_ref[...]).astype(BF16)
    y = jnp.dot(z, w_ref[...], preferred_element_type=F32)
    out_ref[...] = h_ref[...] + mod_ref[0, 2:3, :] * y


def _oproj_call(mix_in, w, h, mod, *, mode, name):
    n = h.shape[0]
    kdim = w.shape[0]
    if mode == "attn":
        in_specs = [pl.BlockSpec((HP, TM, LANES), lambda i: (0, i, 0))]
        args = [mix_in]
    else:
        u, hs = mix_in
        in_specs = [pl.BlockSpec((TM, D_RNN), lambda i: (i, 0)), pl.BlockSpec((TM, D_RNN), lambda i: (i, 0))]
        args = [u, hs]
    in_specs += [_resident((kdim, D), lambda i: (0, 0)), pl.BlockSpec((TM, D), lambda i: (i, 0)), _mod_spec()]
    return pl.pallas_call(
        functools.partial(_oproj_body, mode=mode),
        grid=(n // TM,),
        in_specs=in_specs,
        out_specs=pl.BlockSpec((TM, D), lambda i: (i, 0)),
        out_shape=jax.ShapeDtypeStruct((n, D), F32),
        compiler_params=_cparams(48),
        name=name,
    )(*args, w, h, mod[:, 0:3])


def _shift_rows(x, s, fill, reverse):
    t = x.shape[0]
    row = lax.broadcasted_iota(jnp.int32, x.shape, 0)
    if reverse:
        return jnp.where(row >= t - s, fill, pltpu.roll(x, t - s, 0))
    return jnp.where(row < s, fill, pltpu.roll(x, s, 0))


def _chunk_scan(a, b, reverse):
    s = 1
    while s < a.shape[0]:
        a_s = _shift_rows(a, s, 1.0, reverse)
        b_s = _shift_rows(b, s, 0.0, reverse)
        b = a * b_s + b
        a = a * a_s
        s *= 2
    return a, b


def _lru_body(x_ref, cw_ref, cb_ref, wa_ref, wx_ref, ba_ref, bx_ref, lam_ref, o_ref, *, nchunks):
    half = pl.program_id(0) % 2
    n = nchunks * SCAN_TC
    halo = SUBLANES
    cw = cw_ref[...]
    cb = cb_ref[...]
    sp = jax.nn.softplus(-lam_ref[...])

    def conv_chunk(ci):
        r0 = pl.multiple_of(ci * SCAN_TC, SCAN_TC)
        lat_first = CTX // SCAN_TC
        prev_ok = jnp.logical_and(ci != 0, ci != lat_first).astype(F32)
        next_ok = jnp.logical_and(ci != lat_first - 1, ci != nchunks - 1).astype(F32)
        p0 = pl.multiple_of(jnp.maximum(r0 - halo, 0), halo)
        n0 = pl.multiple_of(jnp.minimum(r0 + SCAN_TC, n - halo), halo)
        xe = jnp.concatenate([x_ref[pl.ds(p0, halo), :] * prev_ok,
                              x_ref[pl.ds(r0, SCAN_TC), :],
                              x_ref[pl.ds(n0, halo), :] * next_ok], axis=0)
        y = cb
        for j in range(CONV_W):
            off = halo + j - CONV_LEFT
            y = y + xe[off:off + SCAN_TC, :] * cw[j:j + 1, :]
        return r0, y

    def direction(d, ci, h_in):
        r0, xc = conv_chunk(ci)
        xb = xc.astype(BF16)
        r = jax.nn.sigmoid(jnp.dot(xb, wa_ref[d, 0], preferred_element_type=F32) + ba_ref[d:d + 1, :])
        ig = jax.nn.sigmoid(jnp.dot(xb, wx_ref[d, 0], preferred_element_type=F32) + bx_ref[d:d + 1, :])
        xs = jnp.where(half == 0, xc[:, :LANES], xc[:, LANES:])
        log_a = -LRU_C * r * sp[d:d + 1, :]
        a = jnp.exp(log_a)
        bb = jnp.sqrt(1.0 - a * a) * (ig * xs)
        a_cum, h = _chunk_scan(a, bb, reverse=(d == 1))
        h = h + a_cum * h_in
        h_out = h[SCAN_TC - 1:SCAN_TC, :] if d == 0 else h[0:1, :]
        return r0, h, h_out

    def fwd(ci, h_in):
        r0, h, h_out = direction(0, ci, h_in)
        o_ref[pl.ds(r0, SCAN_TC), :] = h
        return h_out

    def bwd(ci, h_in):
        r0, h, h_out = direction(1, ci, h_in)
        o_ref[pl.ds(r0, SCAN_TC), :] += h
        return h_out

    h0 = jnp.zeros((1, LANES), F32)
    lax.fori_loop(0, nchunks, fwd, h0)
    nctx = CTX // SCAN_TC
    hc = lax.fori_loop(0, nctx, lambda k, hh: bwd(nctx - 1 - k, hh), h0)
    lax.fori_loop(0, nchunks - nctx, lambda k, hh: bwd(nchunks - 1 - k, hh), hc)


def _lru_call(u, conv_w, conv_b, wa, wx, ba, bx, lam):
    n = u.shape[0]
    nsteps = D_RNN // LANES
    return pl.pallas_call(
        functools.partial(_lru_body, nchunks=n // SCAN_TC),
        grid=(nsteps,),
        in_specs=[_resident((n, RNN_BW), lambda m: (0, RNN_BLOCKS + m // 2)),
                  pl.BlockSpec((CONV_W, RNN_BW), lambda m: (0, m // 2)),
                  pl.BlockSpec((1, RNN_BW), lambda m: (0, m // 2)),
                  pl.BlockSpec((2, 1, RNN_BW, LANES), lambda m: (0, m // 2, 0, m % 2)),
                  pl.BlockSpec((2, 1, RNN_BW, LANES), lambda m: (0, m // 2, 0, m % 2)),
                  pl.BlockSpec((2, LANES), lambda m: (0, m)),
                  pl.BlockSpec((2, LANES), lambda m: (0, m)),
                  pl.BlockSpec((2, LANES), lambda m: (0, m))],
        out_specs=pl.BlockSpec((n, LANES), lambda m: (0, m)),
        out_shape=jax.ShapeDtypeStruct((n, D_RNN), F32),
        compiler_params=_cparams(52),
        name="rglru_scan",
    )(u, conv_w, conv_b.reshape(1, D_RNN), wa, wx, ba, bx, lam)


def _rows_to_tiles(ref, x):
    rows = x.shape[0]
    for s in range(ROW_TILE):
        ref[pl.ds(s, rows, stride=ROW_TILE), :] = x[:, s * LANES:(s + 1) * LANES]


def _tiles_to_rows(ref, rows):
    return jnp.concatenate([ref[pl.ds(s, rows, stride=ROW_TILE), :] for s in range(ROW_TILE)], axis=1)


def _router_body(h_ref, g_ref, mod_ref, wr_ref, br_ref, f_ref, r_ref, cnt_ref, base_ref):
    @pl.when(pl.program_id(0) == 0)
    def _():
        base_ref[...] = jnp.zeros(base_ref.shape, F32)

    f = _norm_mod(h_ref[...], g_ref[...], mod_ref[0, 0:1, :], mod_ref[0, 1:2, :])
    _rows_to_tiles(f_ref, f)
    logits = jnp.dot(f, wr_ref[...], preferred_element_type=F32, precision=lax.Precision.HIGHEST) + br_ref[...]
    lane = lax.broadcasted_iota(jnp.int32, logits.shape, 1)
    ninf = -jnp.inf
    is_g = lane < N_GROUPS
    gl = jnp.where(is_g, logits, ninf)
    gmax = jnp.max(gl, axis=-1, keepdims=True)
    gsel = jnp.min(jnp.where(gl == gmax, lane, LANES), axis=-1, keepdims=True)
    gsum = jnp.sum(jnp.where(is_g, jnp.exp(gl - gmax), 0.0), axis=-1, keepdims=True)
    g_w = 1.0 / gsum
    e_lo = N_GROUPS + EXP_PER_GROUP * gsel
    in_grp = jnp.logical_and(lane >= e_lo, lane < e_lo + EXP_PER_GROUP)
    el = jnp.where(in_grp, logits, ninf)
    m1 = jnp.max(el, axis=-1, keepdims=True)
    i1 = jnp.min(jnp.where(el == m1, lane, LANES), axis=-1, keepdims=True)
    el2 = jnp.where(lane == i1, ninf, el)
    m2 = jnp.max(el2, axis=-1, keepdims=True)
    i2 = jnp.min(jnp.where(el2 == m2, lane, LANES), axis=-1, keepdims=True)
    t = jnp.exp(m2 - m1)
    w1 = g_w / (1.0 + t)
    w2 = g_w * t / (1.0 + t)
    oh1 = lane == i1
    oh2 = lane == i2
    rr = lax.broadcasted_iota(jnp.int32, (TM, TM), 0)
    cc = lax.broadcasted_iota(jnp.int32, (TM, TM), 1)
    tri = (cc < rr).astype(BF16)
    pre1 = jnp.dot(tri, oh1.astype(BF16), preferred_element_type=F32)
    pre2 = jnp.dot(tri, oh2.astype(BF16), preferred_element_type=F32)
    base = base_ref[...]
    cnt1 = jnp.sum(oh1.astype(F32), axis=0, keepdims=True)
    cnt2 = jnp.sum(oh2.astype(F32), axis=0, keepdims=True)
    rank1 = jnp.sum(jnp.where(oh1, pre1 + base, 0.0), axis=-1, keepdims=True)
    rank2 = jnp.sum(jnp.where(oh2, pre2 + (base + cnt1), 0.0), axis=-1, keepdims=True)
    total = base + cnt1 + cnt2
    base_ref[...] = total
    cnt_ref[...] = total
    cols = [(i1 - N_GROUPS).astype(F32), (i2 - N_GROUPS).astype(F32), w1, w2, rank1, rank2]
    out = jnp.zeros(logits.shape, F32)
    for k, v in enumerate(cols):
        out = jnp.where(lane == k, v, out)
    r_ref[...] = out


def _router_call(h, g, mod, wr, br):
    n = h.shape[0]
    return pl.pallas_call(
        _router_body,
        grid=(n // TM,),
        in_specs=[pl.BlockSpec((TM, D), lambda i: (i, 0)),
                  pl.BlockSpec((1, D), lambda i: (0, 0)),
                  _mod_spec(),
                  pl.BlockSpec((D, LANES), lambda i: (0, 0)),
                  pl.BlockSpec((1, LANES), lambda i: (0, 0))],
        out_specs=[pl.BlockSpec((TM * ROW_TILE, LANES), lambda i: (i, 0)),
                   pl.BlockSpec((TM, LANES), lambda i: (i, 0)),
                   pl.BlockSpec((1, LANES), lambda i: (0, 0))],
        out_shape=[jax.ShapeDtypeStruct((n * ROW_TILE, LANES), F32),
                   jax.ShapeDtypeStruct((n, LANES), F32),
                   jax.ShapeDtypeStruct((1, LANES), F32)],
        scratch_shapes=[pltpu.VMEM((1, LANES), F32)],
        compiler_params=_cparams(32),
        name="moe_router",
    )(h, g.reshape(1, D), mod[:, 3:6], wr, br)


def _dispatch_body(dest_ref, pad_lo_ref, pad_n_ref, f_ref, xg_ref, z_ref, sem):
    i = pl.program_id(0)
    base = i * TM

    def issue(j, carry):
        for k in range(TOP_K):
            d = dest_ref[(base + j) * TOP_K + k]
            pltpu.make_async_copy(f_ref.at[pl.ds(j * ROW_TILE, ROW_TILE)],
                                  xg_ref.at[pl.ds(d * ROW_TILE, ROW_TILE)], sem).start()
        return carry

    def wait_one(carry):
        pltpu.make_async_copy(z_ref, xg_ref.at[pl.ds(0, ROW_TILE)], sem).wait()
        return carry

    lax.fori_loop(0, TM, issue, 0)

    @pl.when(i == 0)
    def _():
        z_ref[...] = jnp.zeros(z_ref.shape, F32)

        def fill(e, carry):
            lo = pad_lo_ref[e]
            n_fill = pad_n_ref[e]

            def batch(bi, c):
                s0 = bi * MOE_MB
                cnt = jnp.minimum(n_fill - s0, MOE_MB)

                def one(s, c2):
                    pltpu.make_async_copy(z_ref, xg_ref.at[pl.ds((lo + s0 + s) * ROW_TILE, ROW_TILE)], sem).start()
                    return c2

                lax.fori_loop(0, cnt, one, 0)
                lax.fori_loop(0, cnt, lambda s, c2: wait_one(c2), 0)
                return c

            lax.fori_loop(0, (n_fill + (MOE_MB - 1)) // MOE_MB, batch, 0)
            return carry

        lax.fori_loop(0, N_EXPERTS + 1, fill, 0)

    lax.fori_loop(0, TM * TOP_K, lambda j, c: wait_one(c), 0)


def _dispatch_call(dest, pad_lo, pad_n, f2, p):
    n = f2.shape[0] // ROW_TILE
    return pl.pallas_call(
        _dispatch_body,
        grid_spec=pltpu.PrefetchScalarGridSpec(
            num_scalar_prefetch=3,
            grid=(n // TM,),
            in_specs=[pl.BlockSpec((TM * ROW_TILE, LANES), lambda i, d, lo, nn: (i, 0))],
            out_specs=pl.BlockSpec(memory_space=pl.ANY),
            scratch_shapes=[pltpu.VMEM((ROW_TILE, LANES), F32), pltpu.SemaphoreType.DMA(())]),
        out_shape=jax.ShapeDtypeStruct((p * ROW_TILE, LANES), F32),
        compiler_params=_cparams(16),
        name="moe_dispatch",
    )(dest, pad_lo, pad_n, f2)


def _ffn_up_body(be_ref, nu_ref, x_ref, w_ref, a_ref, wb_ref):
    b = pl.program_id(0)
    changed = jnp.logical_or(b == 0, be_ref[b] != be_ref[jnp.maximum(b - 1, 0)])

    @pl.when(changed)
    def _():
        wb_ref[...] = w_ref[0, 0].astype(BF16)

    @pl.when(b < nu_ref[0])
    def _():
        x = _tiles_to_rows(x_ref, MOE_MB).astype(BF16)
        hgu = jnp.dot(x, wb_ref[...], preferred_element_type=F32)
        g = hgu[:, :D_EXPERT]
        u = hgu[:, D_EXPERT:]
        a_ref[...] = (g * jax.nn.sigmoid(g) * u).astype(BF16)

    @pl.when(b >= nu_ref[0])
    def _():
        a_ref[...] = jnp.zeros(a_ref.shape, BF16)


def _ffn_up_call(block_e, n_used, xg, w_gu, layer):
    p = xg.shape[0] // ROW_TILE
    return pl.pallas_call(
        _ffn_up_body,
        grid_spec=pltpu.PrefetchScalarGridSpec(
            num_scalar_prefetch=2,
            grid=(p // MOE_MB,),
            in_specs=[pl.BlockSpec((MOE_MB * ROW_TILE, LANES), lambda b, be, nu: (jnp.minimum(b, nu[0] - 1), 0)),
                      pl.BlockSpec((1, 1, D, 2 * D_EXPERT), lambda b, be, nu: (layer, be[b], 0, 0))],
            out_specs=pl.BlockSpec((MOE_MB, D_EXPERT), lambda b, be, nu: (b, 0)),
            scratch_shapes=[pltpu.VMEM((D, 2 * D_EXPERT), BF16)]),
        out_shape=jax.ShapeDtypeStruct((p, D_EXPERT), BF16),
        compiler_params=_cparams(50),
        name="moe_ffn_up",
    )(block_e, n_used, xg, w_gu)


def _ffn_down_body(be_ref, nu_ref, a_ref, w_ref, y_ref, wb_ref):
    b = pl.program_id(0)
    changed = jnp.logical_or(b == 0, be_ref[b] != be_ref[jnp.maximum(b - 1, 0)])

    @pl.when(changed)
    def _():
        wb_ref[...] = w_ref[0, 0].astype(BF16)

    @pl.when(b < nu_ref[0])
    def _():
        _rows_to_tiles(y_ref, jnp.dot(a_ref[...], wb_ref[...], preferred_element_type=F32))

    @pl.when(b >= nu_ref[0])
    def _():
        y_ref[...] = jnp.zeros(y_ref.shape, F32)


def _ffn_down_call(block_e, n_used, act, w_down, layer):
    p = act.shape[0]
    return pl.pallas_call(
        _ffn_down_body,
        grid_spec=pltpu.PrefetchScalarGridSpec(
            num_scalar_prefetch=2,
            grid=(p // MOE_MB,),
            in_specs=[pl.BlockSpec((MOE_MB, D_EXPERT), lambda b, be, nu: (b, 0)),
                      pl.BlockSpec((1, 1, D_EXPERT, D), lambda b, be, nu: (layer, be[b], 0, 0))],
            out_specs=pl.BlockSpec((MOE_MB * ROW_TILE, LANES), lambda b, be, nu: (b, 0)),
            scratch_shapes=[pltpu.VMEM((D_EXPERT, D), BF16)]),
        out_shape=jax.ShapeDtypeStruct((p * ROW_TILE, LANES), F32),
        compiler_params=_cparams(40),
        name="moe_ffn_down",
    )(block_e, n_used, act, w_down)


def _combine_body(dest_ref, h_ref, r_ref, mod_ref, yb_ref, o_ref, buf_ref, sem):
    base = pl.program_id(0) * TM

    def issue(j, carry):
        for k in range(TOP_K):
            d = dest_ref[(base + j) * TOP_K + k]
            pltpu.make_async_copy(yb_ref.at[pl.ds(d * ROW_TILE, ROW_TILE)],
                                  buf_ref.at[k, pl.ds(j * ROW_TILE, ROW_TILE)], sem).start()
        return carry

    def drain(j, carry):
        pltpu.make_async_copy(yb_ref.at[pl.ds(0, ROW_TILE)], buf_ref.at[0, pl.ds(0, ROW_TILE)], sem).wait()
        return carry

    lax.fori_loop(0, TM, issue, 0)
    lax.fori_loop(0, TM * TOP_K, drain, 0)
    r = r_ref[...]
    y = sum(r[:, TOP_K + k:TOP_K + k + 1] * _tiles_to_rows(buf_ref.at[k], TM) for k in range(TOP_K))
    o_ref[...] = h_ref[...] + mod_ref[0, 2:3, :] * y


def _combine_call(dest, h, route, mod, yb):
    n = h.shape[0]
    return pl.pallas_call(
        _combine_body,
        grid_spec=pltpu.PrefetchScalarGridSpec(
            num_scalar_prefetch=1,
            grid=(n // TM,),
            in_specs=[pl.BlockSpec((TM, D), lambda i, d: (i, 0)),
                      pl.BlockSpec((TM, LANES), lambda i, d: (i, 0)),
                      pl.BlockSpec((1, 3, D), lambda i, d: (jnp.minimum(i, 1), 0, 0)),
                      pl.BlockSpec(memory_space=pl.ANY)],
            out_specs=pl.BlockSpec((TM, D), lambda i, d: (i, 0)),
            scratch_shapes=[pltpu.VMEM((TOP_K, TM * ROW_TILE, LANES), F32), pltpu.SemaphoreType.DMA(())]),
        out_shape=jax.ShapeDtypeStruct((n, D), F32),
        compiler_params=_cparams(32),
        name="moe_combine",
    )(dest, h, route, mod[:, 3:6], yb)


def _dispatch_plan(route, cnt, n):
    experts = jnp.arange(N_EXPERTS, dtype=jnp.int32)
    counts = cnt[0, N_GROUPS:N_GROUPS + N_EXPERTS].astype(jnp.int32)
    padded = (counts + MOE_MB - 1) // MOE_MB * MOE_MB
    end_pad = jnp.cumsum(padded)
    start_pad = end_pad - padded
    n_blocks = -(-(n * TOP_K + N_EXPERTS * (MOE_MB - 1)) // MOE_MB)
    p = n_blocks * MOE_MB
    blk_start = jnp.arange(n_blocks, dtype=jnp.int32) * MOE_MB
    block_e = jnp.minimum(jnp.sum(end_pad[None, :] <= blk_start[:, None], axis=1), N_EXPERTS - 1).astype(jnp.int32)
    n_used = (end_pad[-1:] // MOE_MB).astype(jnp.int32)
    e_idx = route[:, 0:TOP_K].astype(jnp.int32)
    rank = route[:, 2 * TOP_K:3 * TOP_K].astype(jnp.int32)
    start = jnp.sum(jnp.where(e_idx[..., None] == experts, start_pad, 0), axis=-1)
    dest = (start + rank).reshape(n * TOP_K)
    pad_lo = jnp.concatenate([start_pad + counts, end_pad[-1:]]).astype(jnp.int32)
    pad_n = jnp.concatenate([padded - counts, p - end_pad[-1:]]).astype(jnp.int32)
    return p, block_e, n_used, dest, pad_lo, pad_n


def _moe_layer(h, g, mod, wr, br, w_gu, w_down, layer):
    n = h.shape[0]
    f2, route, cnt = _router_call(h, g, mod, wr, br)
    p, block_e, n_used, dest, pad_lo, pad_n = _dispatch_plan(route, cnt, n)
    xg = _dispatch_call(dest, pad_lo, pad_n, f2, p)
    act = _ffn_up_call(block_e, n_used, xg, w_gu, layer)
    yb = _ffn_down_call(block_e, n_used, act, w_down, layer)
    return _combine_call(dest, h, route, mod, yb)


def _final_body(h_ref, g_ref, o_ref):
    x = h_ref[...]
    ms = jnp.mean(x * x, axis=-1, keepdims=True)
    o_ref[...] = x * lax.rsqrt(ms + EPS) * g_ref[...]


def _final_call(h, g, l):
    first = CTX // TM
    return pl.pallas_call(
        _final_body,
        grid=(l // TM,),
        in_specs=[pl.BlockSpec((TM, D), lambda i: (i + first, 0)), pl.BlockSpec((1, D), lambda i: (0, 0))],
        out_specs=pl.BlockSpec((TM, D), lambda i: (i, 0)),
        out_shape=jax.ShapeDtypeStruct((l, D), F32),
        compiler_params=_cparams(16),
        name="final_norm",
    )(h, g.reshape(1, D))


def _rope_tables(l):
    quarter = HEAD_DIM // 4
    inv = ROPE_BASE ** (-jnp.arange(quarter, dtype=F32) / quarter)
    pos = jnp.arange(l, dtype=jnp.int32)
    rows = (pos // GRID_W).astype(F32)
    cols = (pos % GRID_W).astype(F32)
    ang_r = rows[:, None] * inv
    ang_c = cols[:, None] * inv
    cos = jnp.concatenate([jnp.cos(ang_r), jnp.cos(ang_r), jnp.cos(ang_c), jnp.cos(ang_c)], axis=1)
    sin = jnp.concatenate([-jnp.sin(ang_r), jnp.sin(ang_r), -jnp.sin(ang_c), jnp.sin(ang_c)], axis=1)
    cos = jnp.concatenate([jnp.ones((CTX, HEAD_DIM), F32), cos], axis=0)
    sin = jnp.concatenate([jnp.zeros((CTX, HEAD_DIM), F32), sin], axis=0)
    return jnp.tile(cos, (1, 2)), jnp.tile(sin, (1, 2))


def _attn_a_weights(w_qkv):
    nq = HEADS * HEAD_DIM
    nkv = A_KV_HEADS * HEAD_DIM
    wq = w_qkv[:, :nq]
    wk = w_qkv[:, nq:nq + nkv].reshape(D, A_KV_HEADS, 1, HEAD_DIM)
    wv = w_qkv[:, nq + nkv:].reshape(D, A_KV_HEADS, 1, HEAD_DIM)
    dup = lambda w: jnp.broadcast_to(w, (D, A_KV_HEADS, 2, HEAD_DIM)).reshape(D, 2 * nkv)
    return jnp.concatenate([wq, dup(wk), dup(wv)], axis=1).astype(BF16)


def kernel(x, c, ctx, c_ctx, ada_w, ada_b, norm_mix_g, norm_ffn_g, router_group_w, router_group_b,
           router_expert_w, router_expert_b, moe_w_gu, moe_w_down, attn_w_qkv, attn_w_o, attn_sink,
           na_w_qkv, na_w_o, na_rpb, rnn_w_in, rnn_conv_w, rnn_conv_b, rnn_wa, rnn_ba, rnn_wx, rnn_bx,
           rnn_lam, rnn_w_out, final_norm_g):
    batch, l, _ = x.shape
    assert batch == 1 and ctx.shape[1] == CTX and l % (QB * 2) == 0 and l // QB >= NA_KBLK
    h = jnp.concatenate([ctx[0], x[0]], axis=0)
    c2 = jnp.stack([c_ctx, c[0]], axis=1)
    mods = _ada_call(c2, ada_w, ada_b).reshape(DEPTH, 2, 6, D)
    rope = _rope_tables(l)
    pad_r = LANES - N_GROUPS - N_EXPERTS
    for i in range(DEPTH):
        kind, j = i % 3, i // 3
        mod = mods[i]
        if kind == 0:
            w = _attn_a_weights(attn_w_qkv[j])
            nq = HEADS * HEAD_DIM
            qkv3 = _proj_call(h, norm_mix_g[i], mod, w, mode="rope", rope=rope,
                              n_rope=nq + 2 * A_KV_HEADS * HEAD_DIM, n_q=nq, name="proj_window")
            o3 = _attn_a_call(qkv3, attn_sink[j])
            h = _oproj_call(o3, attn_w_o[j].astype(BF16), h, mod, mode="attn", name="oproj_window")
        elif kind == 1:
            qkv3 = _proj_call(h, norm_mix_g[i], mod, na_w_qkv[j].astype(BF16), mode="cols",
                              n_q=HEADS * HEAD_DIM, name="proj_neighbourhood")
            o3 = _attn_b_call(qkv3, _na_table_call(na_rpb[j]))
            h = _oproj_call(o3, na_w_o[j].astype(BF16), h, mod, mode="attn", name="oproj_neighbourhood")
        else:
            u = _proj_call(h, norm_mix_g[i], mod, rnn_w_in[j].astype(BF16), mode="plain", name="proj_rglru")
            hs = _lru_call(u, rnn_conv_w[j], rnn_conv_b[j], rnn_wa[j].astype(BF16), rnn_wx[j].astype(BF16),
                           rnn_ba[j], rnn_bx[j], rnn_lam[j])
            h = _oproj_call((u, hs), rnn_w_out[j].astype(BF16), h, mod, mode="rnn", name="oproj_rglru")
        wr = jnp.concatenate([router_group_w[i], router_expert_w[i], jnp.zeros((D, pad_r), F32)], axis=1)
        br = jnp.concatenate([router_group_b[i], router_expert_b[i], jnp.zeros((pad_r,), F32)]).reshape(1, LANES)
        h = _moe_layer(h, norm_ffn_g[i], mod, wr, br, moe_w_gu, moe_w_down, i)
    return _final_call(h, final_norm_g, l)[None]
```

```python
import functools

import jax
import jax.numpy as jnp
from jax import lax
from jax.experimental import pallas as pl
from jax.experimental.pallas import tpu as pltpu

F32 = jnp.float32
BF16 = jnp.bfloat16

D = 2048
DEPTH = 4
GRID_W = 64
CTX = 256
HEADS = 32
HEAD_DIM = 64
A_KV_HEADS = 4
A_GROUP = HEADS // A_KV_HEADS
WINDOW = 128
NB_KH = 8
NB_KW = 16
D_RNN = 2560
RNN_BLOCKS = 10
RNN_BW = D_RNN // RNN_BLOCKS
CONV_W = 4
CONV_LEFT = 2
LRU_C = 8.0
N_GROUPS = 4
EXP_PER_GROUP = 8
N_EXPERTS = N_GROUPS * EXP_PER_GROUP
TOP_K = 2
D_EXPERT = 768
ROPE_BASE = 10000.0
EPS = 1e-6
NEG = -1e30

LANES = 128
SUBLANES = 8
MIB = 1024 * 1024

TM = 256
QB = 128
HP = HEADS // 2
MOE_MB = 256
ROW_TILE = D // LANES
XROW_TILE = ROW_TILE // 2
DMA_UNROLL = 8
SCAN_TC = 256
NA_KBLK = 5
SQRT_SCALE = HEAD_DIM ** -0.5


def _cparams(vmem_mib, sem=("arbitrary",)):
    return pltpu.CompilerParams(dimension_semantics=sem, vmem_limit_bytes=int(vmem_mib * MIB))


def _resident(block_shape, index_map):
    return pl.BlockSpec(block_shape, index_map, pipeline_mode=pl.Buffered(1))


def _mod_spec():
    return pl.BlockSpec((1, 3, D), lambda i: (jnp.minimum(i, 1), 0, 0))


def _norm_mod(x, g, shift, scale):
    ms = jnp.mean(x * x, axis=-1, keepdims=True)
    y = x * lax.rsqrt(ms + EPS) * g
    return y * (1.0 + scale) + shift


ADA_TN = 1024


def _ada_body(c_ref, w_ref, b_ref, o_ref):
    c = c_ref[...]
    cs = c * jax.nn.sigmoid(c)
    for r in range(2):
        cb = jnp.broadcast_to(cs[:, r:r + 1], (D, LANES))
        outs = []
        for j in range(ADA_TN // LANES):
            w = w_ref[0, :, j * LANES:(j + 1) * LANES]
            p = (w * cb).reshape(D // SUBLANES, SUBLANES, LANES).sum(axis=0)
            outs.append(p.sum(axis=0, keepdims=True))
        o_ref[0, r:r + 1, :] = jnp.concatenate(outs, axis=1) + b_ref[0]


def _ada_call(c2, ada_w, ada_b):
    return pl.pallas_call(
        _ada_body,
        grid=(DEPTH, 6 * D // ADA_TN),
        in_specs=[pl.BlockSpec((D, 2), lambda l, j: (0, 0)),
                  pl.BlockSpec((1, D, ADA_TN), lambda l, j: (l, 0, j)),
                  pl.BlockSpec((1, 1, ADA_TN), lambda l, j: (l, 0, j))],
        out_specs=pl.BlockSpec((1, 2, ADA_TN), lambda l, j: (l, 0, j)),
        out_shape=jax.ShapeDtypeStruct((DEPTH, 2, 6 * D), F32),
        compiler_params=_cparams(32, ("arbitrary", "arbitrary")),
        name="ada_mod",
    )(c2, ada_w, ada_b.reshape(DEPTH, 1, 6 * D))


PROJ_CH = 512


def _rope_piece(piece, cos, sin):
    lane = lax.broadcasted_iota(jnp.int32, piece.shape, 1)
    first = (lane & 16) == 0
    partner = jnp.where(first, pltpu.roll(piece, LANES - 16, 1), pltpu.roll(piece, 16, 1))
    return piece * cos + partner * sin


def _proj_body(*refs, mode, n_rope, n_q):
    if mode == "rope":
        x_ref, g_ref, mod_ref, w_ref, cos_ref, sin_ref, o_ref = refs
    else:
        x_ref, g_ref, mod_ref, w_ref, o_ref = refs
    a = _norm_mod(x_ref[...], g_ref[...], mod_ref[0, 0:1, :], mod_ref[0, 1:2, :]).astype(BF16)
    nout = w_ref.shape[1]
    for c in range(nout // PROJ_CH):
        acc = jnp.dot(a, w_ref[:, c * PROJ_CH:(c + 1) * PROJ_CH], preferred_element_type=F32)
        if mode == "plain":
            o_ref[:, c * PROJ_CH:(c + 1) * PROJ_CH] = acc
            continue
        for k in range(PROJ_CH // LANES):
            col0 = c * PROJ_CH + k * LANES
            piece = acc[:, k * LANES:(k + 1) * LANES]
            if col0 < n_rope:
                piece = _rope_piece(piece, cos_ref[...], sin_ref[...])
            if col0 < n_q:
                piece = piece * SQRT_SCALE
            o_ref[col0 // LANES] = piece.astype(BF16)


def _proj_call(h, g, mod, w, *, mode, rope=None, n_rope=0, n_q=0, name):
    n = h.shape[0]
    nout = w.shape[1]
    in_specs = [pl.BlockSpec((TM, D), lambda i: (i, 0)),
                pl.BlockSpec((1, D), lambda i: (0, 0)),
                _mod_spec(),
                _resident((D, nout), lambda i: (0, 0))]
    args = [h, g.reshape(1, D), mod[:, 0:3], w]
    if mode == "rope":
        in_specs += [pl.BlockSpec((TM, LANES), lambda i: (i, 0))] * 2
        args += list(rope)
    if mode == "plain":
        out_spec = pl.BlockSpec((TM, nout), lambda i: (i, 0))
        out_shape = jax.ShapeDtypeStruct((n, nout), F32)
        out_bytes = TM * nout * 4
    else:
        out_spec = pl.BlockSpec((nout // LANES, TM, LANES), lambda i: (0, i, 0))
        out_shape = jax.ShapeDtypeStruct((nout // LANES, n, LANES), BF16)
        out_bytes = TM * nout * 2
    vmem = (D * nout * 2 + 2 * TM * D * 4 + 2 * out_bytes) / MIB + 12
    return pl.pallas_call(
        functools.partial(_proj_body, mode=mode, n_rope=n_rope, n_q=n_q),
        grid=(n // TM,),
        in_specs=in_specs,
        out_specs=out_spec,
        out_shape=out_shape,
        compiler_params=_cparams(vmem),
        name=name,
    )(*args)


def _lane_lo(shape):
    return lax.broadcasted_iota(jnp.int32, shape, len(shape) - 1) < HEAD_DIM


def _attn_a_body(sink_ref, q_ref, kp_ref, kc_ref, kn_ref, kx_ref, vp_ref, vc_ref, vn_ref, vx_ref, o_ref, *, nb):
    b = pl.program_id(0)
    first_lat = CTX // QB
    is_lat = b >= first_lat
    prev_ok = b >= first_lat + 1
    next_ok = jnp.logical_and(is_lat, b <= nb - 2)
    nloc = 3 * QB
    nkeys = nloc + CTX
    qi = lax.broadcasted_iota(jnp.int32, (QB, nkeys), 0)
    kj = lax.broadcasted_iota(jnp.int32, (QB, nkeys), 1)
    rel = qi + WINDOW - kj
    band = jnp.abs(rel) <= WINDOW
    seg_ok = jnp.where(kj < QB, prev_ok.astype(jnp.int32),
                       jnp.where(kj < 2 * QB, is_lat.astype(jnp.int32), next_ok.astype(jnp.int32)))
    ok = jnp.logical_or(kj >= nloc, jnp.logical_and(band, seg_ok > 0))
    mask_add = jnp.where(ok, 0.0, NEG)
    lo = _lane_lo((QB, LANES))
    zero = jnp.zeros((QB, LANES), BF16)
    for kvh in range(A_KV_HEADS):
        keys = jnp.concatenate([kp_ref[kvh], kc_ref[kvh], kn_ref[kvh], kx_ref[kvh]], axis=0)
        vals = jnp.concatenate([vp_ref[kvh], vc_ref[kvh], vn_ref[kvh], vx_ref[kvh]], axis=0)
        parts = []
        sinks = []
        for p in range(A_GROUP // 2):
            q2 = q_ref[kvh * (A_GROUP // 2) + p]
            parts += [jnp.where(lo, q2, zero), jnp.where(lo, zero, q2)]
            for s in range(2):
                sinks.append(jnp.full((1, QB, 1), sink_ref[kvh * A_GROUP + 2 * p + s], F32))
        qs = jnp.concatenate(parts, axis=0)
        sink = jnp.concatenate(sinks, axis=0)
        s3 = lax.dot_general(qs, keys, (((1,), (1,)), ((), ())), preferred_element_type=F32)
        s3 = s3.reshape(A_GROUP, QB, nkeys) + mask_add[None]
        m = jnp.maximum(jnp.max(s3, axis=-1, keepdims=True), sink)
        e = jnp.exp(s3 - m)
        denom = jnp.sum(e, axis=-1, keepdims=True) + jnp.exp(sink - m)
        o = jnp.dot(e.astype(BF16).reshape(A_GROUP * QB, nkeys), vals, preferred_element_type=F32)
        o = o.reshape(A_GROUP, QB, LANES) / denom
        for p in range(A_GROUP // 2):
            o_ref[kvh * (A_GROUP // 2) + p] = jnp.where(lo, o[2 * p], o[2 * p + 1]).astype(BF16)


def _attn_a_call(qkv3, sink):
    n = qkv3.shape[1]
    nb = n // QB
    first_lat = CTX // QB
    kblk, vblk = HP // A_KV_HEADS, HP // A_KV_HEADS + 1

    def kv_specs(blk):
        return [pl.BlockSpec((A_KV_HEADS, QB, LANES), lambda b: (blk, jnp.maximum(b - 1, first_lat), 0)),
                pl.BlockSpec((A_KV_HEADS, QB, LANES), lambda b: (blk, b, 0)),
                pl.BlockSpec((A_KV_HEADS, QB, LANES), lambda b: (blk, jnp.minimum(b + 1, nb - 1), 0)),
                pl.BlockSpec((A_KV_HEADS, CTX, LANES), lambda b: (blk, 0, 0))]

    return pl.pallas_call(
        functools.partial(_attn_a_body, nb=nb),
        grid=(nb,),
        in_specs=[pl.BlockSpec(memory_space=pltpu.SMEM),
                  pl.BlockSpec((HP, QB, LANES), lambda b: (0, b, 0))] + kv_specs(kblk) + kv_specs(vblk),
        out_specs=pl.BlockSpec((HP, QB, LANES), lambda b: (0, b, 0)),
        out_shape=jax.ShapeDtypeStruct((HP, n, LANES), BF16),
        compiler_params=_cparams(40),
        name="attn_window",
    )(sink, qkv3, *([qkv3] * 8))


NA_TAB = 2 * NB_KH
RPB_W = 2 * NB_KW - 1
RPB_H = 2 * NB_KH - 1


def _na_table_body(rpb_ref, o_ref):
    h = pl.program_id(0)
    shape = (GRID_W, LANES)
    c = lax.broadcasted_iota(jnp.int32, shape, 0)
    lane = lax.broadcasted_iota(jnp.int32, shape, 1)
    kc = lane & (GRID_W - 1)
    hi = lane >= GRID_W
    cs = jnp.clip(c - NB_KW // 2, 0, GRID_W - NB_KW)
    colok = jnp.logical_and(kc >= cs, kc < cs + NB_KW)
    diff = kc - c + (NB_KW - 1)
    neg = jnp.full(shape, NEG, F32)
    rows = [neg]
    for d in range(RPB_H):
        acc = neg
        for j in range(RPB_W):
            acc = jnp.where(diff == j, rpb_ref[h * (RPB_H * RPB_W) + d * RPB_W + j], acc)
        rows.append(jnp.where(colok, acc, NEG))
    rows.append(neg)
    for t in range(NA_TAB):
        o_ref[0, t] = jnp.where(hi, rows[t + 1], rows[t])


def _na_table_call(rpb):
    return pl.pallas_call(
        _na_table_body,
        grid=(HEADS,),
        in_specs=[pl.BlockSpec(memory_space=pltpu.SMEM)],
        out_specs=pl.BlockSpec((1, NA_TAB, GRID_W, LANES), lambda h: (h, 0, 0, 0)),
        out_shape=jax.ShapeDtypeStruct((HEADS, NA_TAB, GRID_W, LANES), F32),
        compiler_params=_cparams(16),
        name="na_bias_table",
    )(rpb.reshape(-1))


def _attn_b_body(tab_ref, q_ref, k0, k1, k2, k3, k4, kx_ref, v0, v1, v2, v3, v4, vx_ref, o_ref, *, nbl):
    b = pl.program_id(0)
    first_lat = CTX // QB
    is_lat = b >= first_lat
    bl = jnp.maximum(b - first_lat, 0)
    ws = jnp.clip(bl - 2, 0, nbl - NA_KBLK)
    n_rows = 2 * nbl
    nloc = NA_KBLK * QB
    rs = [jnp.clip(2 * bl + qr - NB_KH // 2, 0, n_rows - NB_KH) for qr in range(2)]
    tidx = [[jnp.clip(2 * (ws + j) - (2 * bl + qr) + (NB_KH - 1), -1, NA_TAB - 2) + 1 for j in range(NA_KBLK)]
            for qr in range(2)]
    qrow = lax.broadcasted_iota(jnp.int32, (QB, nloc), 0)
    kcol = lax.broadcasted_iota(jnp.int32, (QB, nloc), 1)
    krow = 2 * ws + jnp.right_shift(kcol, 6)
    rs_q = jnp.where(qrow < GRID_W, rs[0], rs[1])
    row_ok = jnp.logical_and(jnp.logical_and(krow >= rs_q, krow < rs_q + NB_KH), is_lat)
    row_ok2 = jnp.concatenate([row_ok, row_ok], axis=0)
    lo = _lane_lo((QB, LANES))
    zero = jnp.zeros((QB, LANES), BF16)
    kl = (k0, k1, k2, k3, k4)
    vl = (v0, v1, v2, v3, v4)

    def pair(hp, carry):
        q2 = q_ref[hp]
        qs = jnp.concatenate([jnp.where(lo, q2, zero), jnp.where(lo, zero, q2)], axis=0)
        keys = jnp.concatenate([r[hp] for r in kl] + [kx_ref[hp]], axis=0)
        vals = jnp.concatenate([r[hp] for r in vl] + [vx_ref[hp]], axis=0)
        s = lax.dot_general(qs, keys, (((1,), (1,)), ((), ())), preferred_element_type=F32)
        bias = jnp.concatenate(
            [jnp.concatenate([tab_ref[2 * hp + hh, tidx[qr][j]] for j in range(NA_KBLK)], axis=1)
             for hh in range(2) for qr in range(2)], axis=0)
        s_loc = jnp.where(row_ok2, s[:, :nloc] + bias, NEG)
        s_ctx = s[:, nloc:]
        m = jnp.maximum(jnp.max(s_loc, axis=-1, keepdims=True), jnp.max(s_ctx, axis=-1, keepdims=True))
        e_loc = jnp.exp(s_loc - m)
        e_ctx = jnp.exp(s_ctx - m)
        denom = jnp.sum(e_loc, axis=-1, keepdims=True) + jnp.sum(e_ctx, axis=-1, keepdims=True)
        p = jnp.concatenate([e_loc, e_ctx], axis=1).astype(BF16)
        o = jnp.dot(p, vals, preferred_element_type=F32) / denom
        o_ref[hp] = jnp.where(lo, o[:QB], o[QB:]).astype(BF16)
        return carry

    lax.fori_loop(0, HP, pair, 0)


def _attn_b_call(qkv3, table):
    n = qkv3.shape[1]
    nb = n // QB
    first_lat = CTX // QB
    nbl = nb - first_lat

    def win(blk, i):
        return pl.BlockSpec(
            (HP, QB, LANES),
            lambda b: (blk, first_lat + jnp.clip(jnp.maximum(b - first_lat, 0) - 2, 0, nbl - NA_KBLK) + i, 0))

    def kv_specs(blk):
        return [win(blk, i) for i in range(NA_KBLK)] + [pl.BlockSpec((HP, CTX, LANES), lambda b: (blk, 0, 0))]

    return pl.pallas_call(
        functools.partial(_attn_b_body, nbl=nbl),
        grid=(nb,),
        in_specs=[_resident((HEADS, NA_TAB, GRID_W, LANES), lambda b: (0, 0, 0, 0)),
                  pl.BlockSpec((HP, QB, LANES), lambda b: (0, b, 0))] + kv_specs(1) + kv_specs(2),
        out_specs=pl.BlockSpec((HP, QB, LANES), lambda b: (0, b, 0)),
        out_shape=jax.ShapeDtypeStruct((HP, n, LANES), BF16),
        compiler_params=_cparams(52),
        name="attn_neighbourhood",
    )(table, qkv3, *([qkv3] * 12))


def _oproj_body(*refs, mode):
    if mode == "attn":
        o_ref, w_ref, h_ref, mod_ref, out_ref = refs
        z = jnp.concatenate([o_ref[c] for c in range(HP)], axis=1)
    else:
        ug_ref, hs_ref, w_ref, h_ref, mod_ref, out_ref = refs
        z = (jax.nn.gelu(ug_ref[...]) * hs_ref[...]).astype(BF16)
    y = jnp.dot(z, w_ref[...], preferred_element_type=F32)
    out_ref[...] = h_ref[...] + mod_ref[0, 2:3, :] * y


def _oproj_call(mix_in, w, h, mod, *, mode, name):
    n = h.shape[0]
    kdim = w.shape[0]
    if mode == "attn":
        in_specs = [pl.BlockSpec((HP, TM, LANES), lambda i: (0, i, 0))]
        args = [mix_in]
    else:
        u, hs = mix_in
        in_specs = [pl.BlockSpec((TM, D_RNN), lambda i: (i, 0)), pl.BlockSpec((TM, D_RNN), lambda i: (i, 0))]
        args = [u, hs]
    in_specs += [_resident((kdim, D), lambda i: (0, 0)), pl.BlockSpec((TM, D), lambda i: (i, 0)), _mod_spec()]
    return pl.pallas_call(
        functools.partial(_oproj_body, mode=mode),
        grid=(n // TM,),
        in_specs=in_specs,
        out_specs=pl.BlockSpec((TM, D), lambda i: (i, 0)),
        out_shape=jax.ShapeDtypeStruct((n, D), F32),
        compiler_params=_cparams(48),
        name=name,
    )(*args, w, h, mod[:, 0:3])


def _shift_rows(x, s, fill, reverse):
    t = x.shape[0]
    if s % SUBLANES == 0:
        pad = jnp.full((s, x.shape[1]), fill, x.dtype)
        return jnp.concatenate([x[s:], pad] if reverse else [pad, x[:t - s]], axis=0)
    row = lax.broadcasted_iota(jnp.int32, x.shape, 0)
    if reverse:
        return jnp.where(row >= t - s, fill, pltpu.roll(x, t - s, 0))
    return jnp.where(row < s, fill, pltpu.roll(x, s, 0))


def _chunk_scan(a, b, reverse):
    s = 1
    while s < a.shape[0]:
        a_s = _shift_rows(a, s, 1.0, reverse)
        b_s = _shift_rows(b, s, 0.0, reverse)
        b = a * b_s + b
        a = a * a_s
        s *= 2
    return a, b


def _lru_body(x_ref, cw_ref, cb_ref, wa_ref, wx_ref, ba_ref, bx_ref, lam_ref, o_ref, *, nchunks):
    n = nchunks * SCAN_TC
    halo = SUBLANES
    cw = cw_ref[...]
    cb = cb_ref[...]
    sp = jax.nn.softplus(-lam_ref[...])

    def conv_chunk(ci):
        r0 = pl.multiple_of(ci * SCAN_TC, SCAN_TC)
        lat_first = CTX // SCAN_TC
        prev_ok = jnp.logical_and(ci != 0, ci != lat_first).astype(F32)
        next_ok = jnp.logical_and(ci != lat_first - 1, ci != nchunks - 1).astype(F32)
        p0 = pl.multiple_of(jnp.maximum(r0 - halo, 0), halo)
        n0 = pl.multiple_of(jnp.minimum(r0 + SCAN_TC, n - halo), halo)
        xe = jnp.concatenate([x_ref[pl.ds(p0, halo), :] * prev_ok,
                              x_ref[pl.ds(r0, SCAN_TC), :],
                              x_ref[pl.ds(n0, halo), :] * next_ok], axis=0)
        y = cb
        for j in range(CONV_W):
            off = halo + j - CONV_LEFT
            y = y + xe[off:off + SCAN_TC, :] * cw[j:j + 1, :]
        return r0, y

    def direction(d, ci, h_in):
        r0, xc = conv_chunk(ci)
        xb = xc.astype(BF16)
        r = jax.nn.sigmoid(jnp.dot(xb, wa_ref[d, 0], preferred_element_type=F32) + ba_ref[d:d + 1, :])
        ig = jax.nn.sigmoid(jnp.dot(xb, wx_ref[d, 0], preferred_element_type=F32) + bx_ref[d:d + 1, :])
        log_a = -LRU_C * r * sp[d:d + 1, :]
        a = jnp.exp(log_a)
        bb = jnp.sqrt(1.0 - a * a) * (ig * xc)
        a_cum, h = _chunk_scan(a, bb, reverse=(d == 1))
        h = h + a_cum * h_in
        h_out = h[SCAN_TC - 1:SCAN_TC, :] if d == 0 else h[0:1, :]
        return r0, h, h_out

    def fwd(ci, h_in):
        r0, h, h_out = direction(0, ci, h_in)
        o_ref[pl.ds(r0, SCAN_TC), :] = h
        return h_out

    def bwd(ci, h_in):
        r0, h, h_out = direction(1, ci, h_in)
        o_ref[pl.ds(r0, SCAN_TC), :] += h
        return h_out

    h0 = jnp.zeros((1, RNN_BW), F32)
    lax.fori_loop(0, nchunks, fwd, h0)
    nctx = CTX // SCAN_TC
    hc = lax.fori_loop(0, nctx, lambda k, hh: bwd(nctx - 1 - k, hh), h0)
    lax.fori_loop(0, nchunks - nctx, lambda k, hh: bwd(nchunks - 1 - k, hh), hc)


def _lru_call(u, conv_w, conv_b, wa, wx, ba, bx, lam):
    n = u.shape[0]
    return pl.pallas_call(
        functools.partial(_lru_body, nchunks=n // SCAN_TC),
        grid=(RNN_BLOCKS,),
        in_specs=[_resident((n, RNN_BW), lambda m: (0, RNN_BLOCKS + m)),
                  pl.BlockSpec((CONV_W, RNN_BW), lambda m: (0, m)),
                  pl.BlockSpec((1, RNN_BW), lambda m: (0, m)),
                  pl.BlockSpec((2, 1, RNN_BW, RNN_BW), lambda m: (0, m, 0, 0)),
                  pl.BlockSpec((2, 1, RNN_BW, RNN_BW), lambda m: (0, m, 0, 0)),
                  pl.BlockSpec((2, RNN_BW), lambda m: (0, m)),
                  pl.BlockSpec((2, RNN_BW), lambda m: (0, m)),
                  pl.BlockSpec((2, RNN_BW), lambda m: (0, m))],
        out_specs=_resident((n, RNN_BW), lambda m: (0, m)),
        out_shape=jax.ShapeDtypeStruct((n, D_RNN), F32),
        compiler_params=_cparams(52),
        name="rglru_scan",
    )(u, conv_w, conv_b.reshape(1, D_RNN), wa, wx, ba, bx, lam)


def _rows_to_tiles(ref, x):
    rows = x.shape[0]
    for s in range(ROW_TILE):
        ref[pl.ds(s, rows, stride=ROW_TILE), :] = x[:, s * LANES:(s + 1) * LANES]


def _tiles_to_rows(ref, rows):
    return jnp.concatenate([ref[pl.ds(s, rows, stride=ROW_TILE), :] for s in range(ROW_TILE)], axis=1)


def _pack_rows(ref, x):
    rows = x.shape[0]
    lo = lax.bitcast_convert_type(x[:, :D // 2].astype(BF16).astype(F32), jnp.uint32)
    hi = lax.bitcast_convert_type(x[:, D // 2:].astype(BF16).astype(F32), jnp.uint32)
    w = hi | (lo >> 16)
    for s in range(XROW_TILE):
        ref[pl.ds(s, rows, stride=XROW_TILE), :] = w[:, s * LANES:(s + 1) * LANES]


def _unpack_rows(ref, rows):
    w = jnp.concatenate([ref[pl.ds(s, rows, stride=XROW_TILE), :] for s in range(XROW_TILE)], axis=1)
    lo = lax.bitcast_convert_type(w << 16, F32)
    hi = lax.bitcast_convert_type(w & jnp.uint32(0xFFFF0000), F32)
    return jnp.concatenate([lo, hi], axis=1).astype(BF16)


def _router_body(h_ref, g_ref, mod_ref, wr_ref, br_ref, f_ref, r_ref, cnt_ref, base_ref):
    @pl.when(pl.program_id(0) == 0)
    def _():
        base_ref[...] = jnp.zeros(base_ref.shape, F32)

    f = _norm_mod(h_ref[...], g_ref[...], mod_ref[0, 0:1, :], mod_ref[0, 1:2, :])
    _pack_rows(f_ref, f)
    logits = jnp.dot(f, wr_ref[...], preferred_element_type=F32, precision=lax.Precision.HIGHEST) + br_ref[...]
    lane = lax.broadcasted_iota(jnp.int32, logits.shape, 1)
    ninf = -jnp.inf
    is_g = lane < N_GROUPS
    gl = jnp.where(is_g, logits, ninf)
    gmax = jnp.max(gl, axis=-1, keepdims=True)
    gsel = jnp.min(jnp.where(gl == gmax, lane, LANES), axis=-1, keepdims=True)
    gsum = jnp.sum(jnp.where(is_g, jnp.exp(gl - gmax), 0.0), axis=-1, keepdims=True)
    g_w = 1.0 / gsum
    e_lo = N_GROUPS + EXP_PER_GROUP * gsel
    in_grp = jnp.logical_and(lane >= e_lo, lane < e_lo + EXP_PER_GROUP)
    el = jnp.where(in_grp, logits, ninf)
    m1 = jnp.max(el, axis=-1, keepdims=True)
    i1 = jnp.min(jnp.where(el == m1, lane, LANES), axis=-1, keepdims=True)
    el2 = jnp.where(lane == i1, ninf, el)
    m2 = jnp.max(el2, axis=-1, keepdims=True)
    i2 = jnp.min(jnp.where(el2 == m2, lane, LANES), axis=-1, keepdims=True)
    t = jnp.exp(m2 - m1)
    w1 = g_w / (1.0 + t)
    w2 = g_w * t / (1.0 + t)
    oh1 = lane == i1
    oh2 = lane == i2
    rr = lax.broadcasted_iota(jnp.int32, (TM, TM), 0)
    cc = lax.broadcasted_iota(jnp.int32, (TM, TM), 1)
    tri = (cc < rr).astype(BF16)
    pre1 = jnp.dot(tri, oh1.astype(BF16), preferred_element_type=F32)
    pre2 = jnp.dot(tri, oh2.astype(BF16), preferred_element_type=F32)
    base = base_ref[...]
    cnt1 = jnp.sum(oh1.astype(F32), axis=0, keepdims=True)
    cnt2 = jnp.sum(oh2.astype(F32), axis=0, keepdims=True)
    rank1 = jnp.sum(jnp.where(oh1, pre1 + base, 0.0), axis=-1, keepdims=True)
    rank2 = jnp.sum(jnp.where(oh2, pre2 + (base + cnt1), 0.0), axis=-1, keepdims=True)
    total = base + cnt1 + cnt2
    base_ref[...] = total
    cnt_ref[...] = total
    cols = [(i1 - N_GROUPS).astype(F32), (i2 - N_GROUPS).astype(F32), w1, w2, rank1, rank2]
    out = jnp.zeros(logits.shape, F32)
    for k, v in enumerate(cols):
        out = jnp.where(lane == k, v, out)
    r_ref[...] = out


def _router_call(h, g, mod, wr, br):
    n = h.shape[0]
    return pl.pallas_call(
        _router_body,
        grid=(n // TM,),
        in_specs=[pl.BlockSpec((TM, D), lambda i: (i, 0)),
                  pl.BlockSpec((1, D), lambda i: (0, 0)),
                  _mod_spec(),
                  pl.BlockSpec((D, LANES), lambda i: (0, 0)),
                  pl.BlockSpec((1, LANES), lambda i: (0, 0))],
        out_specs=[pl.BlockSpec((TM * XROW_TILE, LANES), lambda i: (i, 0)),
                   pl.BlockSpec((TM, LANES), lambda i: (i, 0)),
                   pl.BlockSpec((1, LANES), lambda i: (0, 0))],
        out_shape=[jax.ShapeDtypeStruct((n * XROW_TILE, LANES), jnp.uint32),
                   jax.ShapeDtypeStruct((n, LANES), F32),
                   jax.ShapeDtypeStruct((1, LANES), F32)],
        scratch_shapes=[pltpu.VMEM((1, LANES), F32)],
        compiler_params=_cparams(32),
        name="moe_router",
    )(h, g.reshape(1, D), mod[:, 3:6], wr, br)


def _dispatch_body(dest_ref, pad_lo_ref, pad_n_ref, f_ref, xg_ref, z_ref, sem):
    i = pl.program_id(0)
    base = i * TM

    def issue(j, carry):
        for k in range(TOP_K):
            d = dest_ref[(base + j) * TOP_K + k]
            pltpu.make_async_copy(f_ref.at[pl.ds(j * XROW_TILE, XROW_TILE)],
                                  xg_ref.at[pl.ds(d * XROW_TILE, XROW_TILE)], sem).start(priority=k)
        return carry

    def wait_one(carry):
        pltpu.make_async_copy(z_ref, xg_ref.at[pl.ds(0, XROW_TILE)], sem).wait()
        return carry

    lax.fori_loop(0, TM, issue, 0, unroll=DMA_UNROLL)

    @pl.when(i == 0)
    def _():
        z_ref[...] = jnp.zeros(z_ref.shape, jnp.uint32)

        def fill(e, carry):
            lo = pad_lo_ref[e]
            n_fill = pad_n_ref[e]

            def batch(bi, c):
                s0 = bi * MOE_MB
                cnt = jnp.minimum(n_fill - s0, MOE_MB)

                def one(s, c2):
                    pltpu.make_async_copy(z_ref, xg_ref.at[pl.ds((lo + s0 + s) * XROW_TILE, XROW_TILE)], sem).start()
                    return c2

                lax.fori_loop(0, cnt, one, 0)
                lax.fori_loop(0, cnt, lambda s, c2: wait_one(c2), 0)
                return c

            lax.fori_loop(0, (n_fill + (MOE_MB - 1)) // MOE_MB, batch, 0)
            return carry

        lax.fori_loop(0, N_EXPERTS + 1, fill, 0)

    lax.fori_loop(0, TM * TOP_K, lambda j, c: wait_one(c), 0, unroll=DMA_UNROLL)


def _dispatch_call(dest, pad_lo, pad_n, f2, p):
    n = f2.shape[0] // XROW_TILE
    return pl.pallas_call(
        _dispatch_body,
        grid_spec=pltpu.PrefetchScalarGridSpec(
            num_scalar_prefetch=3,
            grid=(n // TM,),
            in_specs=[pl.BlockSpec((TM * XROW_TILE, LANES), lambda i, d, lo, nn: (i, 0))],
            out_specs=pl.BlockSpec(memory_space=pl.ANY),
            scratch_shapes=[pltpu.VMEM((XROW_TILE, LANES), jnp.uint32), pltpu.SemaphoreType.DMA(())]),
        out_shape=jax.ShapeDtypeStruct((p * XROW_TILE, LANES), jnp.uint32),
        compiler_params=_cparams(16),
        name="moe_dispatch",
    )(dest, pad_lo, pad_n, f2)


def _ffn_up_body(be_ref, nu_ref, x_ref, w_ref, a_ref, wb_ref):
    b = pl.program_id(0)
    changed = jnp.logical_or(b == 0, be_ref[b] != be_ref[jnp.maximum(b - 1, 0)])

    @pl.when(changed)
    def _():
        wb_ref[...] = w_ref[0, 0].astype(BF16)

    @pl.when(b < nu_ref[0])
    def _():
        hgu = jnp.dot(_unpack_rows(x_ref, MOE_MB), wb_ref[...], preferred_element_type=F32)
        g = hgu[:, :D_EXPERT]
        u = hgu[:, D_EXPERT:]
        a_ref[...] = (g * jax.nn.sigmoid(g) * u).astype(BF16)

    @pl.when(b >= nu_ref[0])
    def _():
        a_ref[...] = jnp.zeros(a_ref.shape, BF16)


def _ffn_up_call(block_e, n_used, xg, w_gu, layer):
    p = xg.shape[0] // XROW_TILE
    return pl.pallas_call(
        _ffn_up_body,
        grid_spec=pltpu.PrefetchScalarGridSpec(
            num_scalar_prefetch=2,
            grid=(p // MOE_MB,),
            in_specs=[pl.BlockSpec((MOE_MB * XROW_TILE, LANES), lambda b, be, nu: (jnp.minimum(b, nu[0] - 1), 0)),
                      pl.BlockSpec((1, 1, D, 2 * D_EXPERT), lambda b, be, nu: (layer, be[b], 0, 0))],
            out_specs=pl.BlockSpec((MOE_MB, D_EXPERT), lambda b, be, nu: (b, 0)),
            scratch_shapes=[pltpu.VMEM((D, 2 * D_EXPERT), BF16)]),
        out_shape=jax.ShapeDtypeStruct((p, D_EXPERT), BF16),
        compiler_params=_cparams(50),
        name="moe_ffn_up",
    )(block_e, n_used, xg, w_gu)


def _ffn_down_body(be_ref, nu_ref, a_ref, w_ref, y_ref, wb_ref):
    b = pl.program_id(0)
    changed = jnp.logical_or(b == 0, be_ref[b] != be_ref[jnp.maximum(b - 1, 0)])

    @pl.when(changed)
    def _():
        wb_ref[...] = w_ref[0, 0].astype(BF16)

    @pl.when(b < nu_ref[0])
    def _():
        _rows_to_tiles(y_ref, jnp.dot(a_ref[...], wb_ref[...], preferred_element_type=F32))

    @pl.when(b >= nu_ref[0])
    def _():
        y_ref[...] = jnp.zeros(y_ref.shape, F32)


def _ffn_down_call(block_e, n_used, act, w_down, layer):
    p = act.shape[0]
    return pl.pallas_call(
        _ffn_down_body,
        grid_spec=pltpu.PrefetchScalarGridSpec(
            num_scalar_prefetch=2,
            grid=(p // MOE_MB,),
            in_specs=[pl.BlockSpec((MOE_MB, D_EXPERT), lambda b, be, nu: (b, 0)),
                      pl.BlockSpec((1, 1, D_EXPERT, D), lambda b, be, nu: (layer, be[b], 0, 0))],
            out_specs=pl.BlockSpec((MOE_MB * ROW_TILE, LANES), lambda b, be, nu: (b, 0)),
            scratch_shapes=[pltpu.VMEM((D_EXPERT, D), BF16)]),
        out_shape=jax.ShapeDtypeStruct((p * ROW_TILE, LANES), F32),
        compiler_params=_cparams(40),
        name="moe_ffn_down",
    )(block_e, n_used, act, w_down)


def _combine_body(dest_ref, h_ref, r_ref, mod_ref, yb_ref, o_ref, buf_ref, sem):
    base = pl.program_id(0) * TM

    def issue(j, carry):
        for k in range(TOP_K):
            d = dest_ref[(base + j) * TOP_K + k]
            pltpu.make_async_copy(yb_ref.at[pl.ds(d * ROW_TILE, ROW_TILE)],
                                  buf_ref.at[k, pl.ds(j * ROW_TILE, ROW_TILE)], sem).start(priority=k)
        return carry

    def drain(j, carry):
        pltpu.make_async_copy(yb_ref.at[pl.ds(0, ROW_TILE)], buf_ref.at[0, pl.ds(0, ROW_TILE)], sem).wait()
        return carry

    lax.fori_loop(0, TM, issue, 0, unroll=DMA_UNROLL)
    lax.fori_loop(0, TM * TOP_K, drain, 0, unroll=DMA_UNROLL)
    r = r_ref[...]
    y = sum(r[:, TOP_K + k:TOP_K + k + 1] * _tiles_to_rows(buf_ref.at[k], TM) for k in range(TOP_K))
    o_ref[...] = h_ref[...] + mod_ref[0, 2:3, :] * y


def _combine_call(dest, h, route, mod, yb):
    n = h.shape[0]
    return pl.pallas_call(
        _combine_body,
        grid_spec=pltpu.PrefetchScalarGridSpec(
            num_scalar_prefetch=1,
            grid=(n // TM,),
            in_specs=[pl.BlockSpec((TM, D), lambda i, d: (i, 0)),
                      pl.BlockSpec((TM, LANES), lambda i, d: (i, 0)),
                      pl.BlockSpec((1, 3, D), lambda i, d: (jnp.minimum(i, 1), 0, 0)),
                      pl.BlockSpec(memory_space=pl.ANY)],
            out_specs=pl.BlockSpec((TM, D), lambda i, d: (i, 0)),
            scratch_shapes=[pltpu.VMEM((TOP_K, TM * ROW_TILE, LANES), F32), pltpu.SemaphoreType.DMA(())]),
        out_shape=jax.ShapeDtypeStruct((n, D), F32),
        compiler_params=_cparams(32),
        name="moe_combine",
    )(dest, h, route, mod[:, 3:6], yb)


def _dispatch_plan(route, cnt, n):
    experts = jnp.arange(N_EXPERTS, dtype=jnp.int32)
    counts = cnt[0, N_GROUPS:N_GROUPS + N_EXPERTS].astype(jnp.int32)
    padded = (counts + MOE_MB - 1) // MOE_MB * MOE_MB
    end_pad = jnp.cumsum(padded)
    start_pad = end_pad - padded
    n_blocks = -(-(n * TOP_K + N_EXPERTS * (MOE_MB - 1)) // MOE_MB)
    p = n_blocks * MOE_MB
    blk_start = jnp.arange(n_blocks, dtype=jnp.int32) * MOE_MB
    block_e = jnp.minimum(jnp.sum(end_pad[None, :] <= blk_start[:, None], axis=1), N_EXPERTS - 1).astype(jnp.int32)
    n_used = (end_pad[-1:] // MOE_MB).astype(jnp.int32)
    e_idx = route[:, 0:TOP_K].astype(jnp.int32)
    rank = route[:, 2 * TOP_K:3 * TOP_K].astype(jnp.int32)
    start = jnp.sum(jnp.where(e_idx[..., None] == experts, start_pad, 0), axis=-1)
    dest = (start + rank).reshape(n * TOP_K)
    pad_lo = jnp.concatenate([start_pad + counts, end_pad[-1:]]).astype(jnp.int32)
    pad_n = jnp.concatenate([padded - counts, p - end_pad[-1:]]).astype(jnp.int32)
    return p, block_e, n_used, dest, pad_lo, pad_n


def _moe_layer(h, g, mod, wr, br, w_gu, w_down, layer):
    n = h.shape[0]
    f2, route, cnt = _router_call(h, g, mod, wr, br)
    p, block_e, n_used, dest, pad_lo, pad_n = _dispatch_plan(route, cnt, n)
    xg = _dispatch_call(dest, pad_lo, pad_n, f2, p)
    act = _ffn_up_call(block_e, n_used, xg, w_gu, layer)
    yb = _ffn_down_call(block_e, n_used, act, w_down, layer)
    return _combine_call(dest, h, route, mod, yb)


def _final_body(h_ref, g_ref, o_ref):
    x = h_ref[...]
    ms = jnp.mean(x * x, axis=-1, keepdims=True)
    o_ref[...] = x * lax.rsqrt(ms + EPS) * g_ref[...]


def _final_call(h, g, l):
    first = CTX // TM
    return pl.pallas_call(
        _final_body,
        grid=(l // TM,),
        in_specs=[pl.BlockSpec((TM, D), lambda i: (i + first, 0)), pl.BlockSpec((1, D), lambda i: (0, 0))],
        out_specs=pl.BlockSpec((TM, D), lambda i: (i, 0)),
        out_shape=jax.ShapeDtypeStruct((l, D), F32),
        compiler_params=_cparams(16),
        name="final_norm",
    )(h, g.reshape(1, D))


def _rope_tables(l):
    quarter = HEAD_DIM // 4
    inv = ROPE_BASE ** (-jnp.arange(quarter, dtype=F32) / quarter)
    pos = jnp.arange(l, dtype=jnp.int32)
    rows = (pos // GRID_W).astype(F32)
    cols = (pos % GRID_W).astype(F32)
    ang_r = rows[:, None] * inv
    ang_c = cols[:, None] * inv
    cos = jnp.concatenate([jnp.cos(ang_r), jnp.cos(ang_r), jnp.cos(ang_c), jnp.cos(ang_c)], axis=1)
    sin = jnp.concatenate([-jnp.sin(ang_r), jnp.sin(ang_r), -jnp.sin(ang_c), jnp.sin(ang_c)], axis=1)
    cos = jnp.concatenate([jnp.ones((CTX, HEAD_DIM), F32), cos], axis=0)
    sin = jnp.concatenate([jnp.zeros((CTX, HEAD_DIM), F32), sin], axis=0)
    return jnp.tile(cos, (1, 2)), jnp.tile(sin, (1, 2))


def _attn_a_weights(w_qkv):
    nq = HEADS * HEAD_DIM
    nkv = A_KV_HEADS * HEAD_DIM
    wq = w_qkv[:, :nq]
    wk = w_qkv[:, nq:nq + nkv].reshape(D, A_KV_HEADS, 1, HEAD_DIM)
    wv = w_qkv[:, nq + nkv:].reshape(D, A_KV_HEADS, 1, HEAD_DIM)
    dup = lambda w: jnp.broadcast_to(w, (D, A_KV_HEADS, 2, HEAD_DIM)).reshape(D, 2 * nkv)
    return jnp.concatenate([wq, dup(wk), dup(wv)], axis=1).astype(BF16)


def kernel(x, c, ctx, c_ctx, ada_w, ada_b, norm_mix_g, norm_ffn_g, router_group_w, router_group_b,
           router_expert_w, router_expert_b, moe_w_gu, moe_w_down, attn_w_qkv, attn_w_o, attn_sink,
           na_w_qkv, na_w_o, na_rpb, rnn_w_in, rnn_conv_w, rnn_conv_b, rnn_wa, rnn_ba, rnn_wx, rnn_bx,
           rnn_lam, rnn_w_out, final_norm_g):
    batch, l, _ = x.shape
    assert batch == 1 and ctx.shape[1] == CTX and l % (QB * 2) == 0 and l // QB >= NA_KBLK
    h = jnp.concatenate([ctx[0], x[0]], axis=0)
    c2 = jnp.stack([c_ctx, c[0]], axis=1)
    mods = _ada_call(c2, ada_w, ada_b).reshape(DEPTH, 2, 6, D)
    rope = _rope_tables(l)
    pad_r = LANES - N_GROUPS - N_EXPERTS
    for i in range(DEPTH):
        kind, j = i % 3, i // 3
        mod = mods[i]
        if kind == 0:
            w = _attn_a_weights(attn_w_qkv[j])
            nq = HEADS * HEAD_DIM
            qkv3 = _proj_call(h, norm_mix_g[i], mod, w, mode="rope", rope=rope,
                              n_rope=nq + 2 * A_KV_HEADS * HEAD_DIM, n_q=nq, name="proj_window")
            o3 = _attn_a_call(qkv3, attn_sink[j])
            h = _oproj_call(o3, attn_w_o[j].astype(BF16), h, mod, mode="attn", name="oproj_window")
        elif kind == 1:
            qkv3 = _proj_call(h, norm_mix_g[i], mod, na_w_qkv[j].astype(BF16), mode="cols",
                              n_q=HEADS * HEAD_DIM, name="proj_neighbourhood")
            o3 = _attn_b_call(qkv3, _na_table_call(na_rpb[j]))
            h = _oproj_call(o3, na_w_o[j].astype(BF16), h, mod, mode="attn", name="oproj_neighbourhood")
        else:
            u = _proj_call(h, norm_mix_g[i], mod, rnn_w_in[j].astype(BF16), mode="plain", name="proj_rglru")
            hs = _lru_call(u, rnn_conv_w[j], rnn_conv_b[j], rnn_wa[j].astype(BF16), rnn_wx[j].astype(BF16),
                           rnn_ba[j], rnn_bx[j], rnn_lam[j])
            h = _oproj_call((u, hs), rnn_w_out[j].astype(BF16), h, mod, mode="rnn", name="oproj_rglru")
        wr = jnp.concatenate([router_group_w[i], router_expert_w[i], jnp.zeros((D, pad_r), F32)], axis=1)
        br = jnp.concatenate([router_group_b[i], router_expert_b[i], jnp.zeros((pad_r,), F32)]).reshape(1, LANES)
        h = _moe_layer(h, norm_ffn_g[i], mod, wr, br, moe_w_gu, moe_w_down, i)
    return _final_call(h, final_norm_g, l)[None]
```

```python
import functools

import jax
import jax.numpy as jnp
from jax import lax
from jax.experimental import pallas as pl
from jax.experimental.pallas import tpu as pltpu

F32 = jnp.float32
BF16 = jnp.bfloat16

D = 2048
DEPTH = 4
GRID_W = 64
CTX = 256
HEADS = 32
HEAD_DIM = 64
A_KV_HEADS = 4
A_GROUP = HEADS // A_KV_HEADS
WINDOW = 128
NB_KH = 8
NB_KW = 16
D_RNN = 2560
RNN_BLOCKS = 10
RNN_BW = D_RNN // RNN_BLOCKS
CONV_W = 4
CONV_LEFT = 2
LRU_C = 8.0
N_GROUPS = 4
EXP_PER_GROUP = 8
N_EXPERTS = N_GROUPS * EXP_PER_GROUP
TOP_K = 2
D_EXPERT = 768
ROPE_BASE = 10000.0
EPS = 1e-6
NEG = -1e30

LANES = 128
SUBLANES = 8
MIB = 1024 * 1024

TM = 256
QB = 128
HP = HEADS // 2
MOE_MB = 256
ROW_TILE = D // LANES
XROW_TILE = ROW_TILE // 2
DMA_UNROLL = 8
SCAN_TC = 256
NA_KBLK = 5
SQRT_SCALE = HEAD_DIM ** -0.5


def _cparams(vmem_mib, sem=("arbitrary",)):
    return pltpu.CompilerParams(dimension_semantics=sem, vmem_limit_bytes=int(vmem_mib * MIB))


def _resident(block_shape, index_map):
    return pl.BlockSpec(block_shape, index_map, pipeline_mode=pl.Buffered(1))


def _mod_spec():
    return pl.BlockSpec((1, 3, D), lambda i: (jnp.minimum(i, 1), 0, 0))


def _norm_mod(x, g, shift, scale):
    ms = jnp.mean(x * x, axis=-1, keepdims=True)
    y = x * lax.rsqrt(ms + EPS) * g
    return y * (1.0 + scale) + shift


ADA_TN = 1024


def _ada_body(c_ref, w_ref, b_ref, o_ref):
    c = c_ref[...]
    cs = c * jax.nn.sigmoid(c)
    for r in range(2):
        cb = jnp.broadcast_to(cs[:, r:r + 1], (D, LANES))
        outs = []
        for j in range(ADA_TN // LANES):
            w = w_ref[0, :, j * LANES:(j + 1) * LANES]
            p = (w * cb).reshape(D // SUBLANES, SUBLANES, LANES).sum(axis=0)
            outs.append(p.sum(axis=0, keepdims=True))
        o_ref[0, r:r + 1, :] = jnp.concatenate(outs, axis=1) + b_ref[0]


def _ada_call(c2, ada_w, ada_b):
    return pl.pallas_call(
        _ada_body,
        grid=(DEPTH, 6 * D // ADA_TN),
        in_specs=[pl.BlockSpec((D, 2), lambda l, j: (0, 0)),
                  pl.BlockSpec((1, D, ADA_TN), lambda l, j: (l, 0, j)),
                  pl.BlockSpec((1, 1, ADA_TN), lambda l, j: (l, 0, j))],
        out_specs=pl.BlockSpec((1, 2, ADA_TN), lambda l, j: (l, 0, j)),
        out_shape=jax.ShapeDtypeStruct((DEPTH, 2, 6 * D), F32),
        compiler_params=_cparams(32, ("arbitrary", "arbitrary")),
        name="ada_mod",
    )(c2, ada_w, ada_b.reshape(DEPTH, 1, 6 * D))


PROJ_CH = 512


def _rope_piece(piece, cos, sin):
    lane = lax.broadcasted_iota(jnp.int32, piece.shape, 1)
    first = (lane & 16) == 0
    partner = jnp.where(first, pltpu.roll(piece, LANES - 16, 1), pltpu.roll(piece, 16, 1))
    return piece * cos + partner * sin


def _proj_body(*refs, mode, n_rope, n_q):
    if mode == "rope":
        x_ref, g_ref, mod_ref, w_ref, cos_ref, sin_ref, o_ref = refs
    else:
        x_ref, g_ref, mod_ref, w_ref, o_ref = refs
    a = _norm_mod(x_ref[...], g_ref[...], mod_ref[0, 0:1, :], mod_ref[0, 1:2, :]).astype(BF16)
    nout = w_ref.shape[1]
    for c in range(nout // PROJ_CH):
        acc = jnp.dot(a, w_ref[:, c * PROJ_CH:(c + 1) * PROJ_CH], preferred_element_type=F32)
        if mode == "plain":
            o_ref[:, c * PROJ_CH:(c + 1) * PROJ_CH] = acc
            continue
        for k in range(PROJ_CH // LANES):
            col0 = c * PROJ_CH + k * LANES
            piece = acc[:, k * LANES:(k + 1) * LANES]
            if col0 < n_rope:
                piece = _rope_piece(piece, cos_ref[...], sin_ref[...])
            if col0 < n_q:
                piece = piece * SQRT_SCALE
            o_ref[col0 // LANES] = piece.astype(BF16)


def _proj_call(h, g, mod, w, *, mode, rope=None, n_rope=0, n_q=0, name):
    n = h.shape[0]
    nout = w.shape[1]
    in_specs = [pl.BlockSpec((TM, D), lambda i: (i, 0)),
                pl.BlockSpec((1, D), lambda i: (0, 0)),
                _mod_spec(),
                _resident((D, nout), lambda i: (0, 0))]
    args = [h, g.reshape(1, D), mod[:, 0:3], w]
    if mode == "rope":
        in_specs += [pl.BlockSpec((TM, LANES), lambda i: (i, 0))] * 2
        args += list(rope)
    if mode == "plain":
        out_spec = pl.BlockSpec((TM, nout), lambda i: (i, 0))
        out_shape = jax.ShapeDtypeStruct((n, nout), F32)
        out_bytes = TM * nout * 4
    else:
        out_spec = pl.BlockSpec((nout // LANES, TM, LANES), lambda i: (0, i, 0))
        out_shape = jax.ShapeDtypeStruct((nout // LANES, n, LANES), BF16)
        out_bytes = TM * nout * 2
    vmem = (D * nout * 2 + 2 * TM * D * 4 + 2 * out_bytes) / MIB + 12
    return pl.pallas_call(
        functools.partial(_proj_body, mode=mode, n_rope=n_rope, n_q=n_q),
        grid=(n // TM,),
        in_specs=in_specs,
        out_specs=out_spec,
        out_shape=out_shape,
        compiler_params=_cparams(vmem),
        name=name,
    )(*args)


def _lane_lo(shape):
    return lax.broadcasted_iota(jnp.int32, shape, len(shape) - 1) < HEAD_DIM


def _pipeline_pairs(scores, probs, output):
    s_next = scores(0)
    pending = None
    for hp in range(HP):
        s_cur = s_next
        if hp + 1 < HP:
            s_next = scores(hp + 1)
        cur = probs(hp, s_cur)
        if pending is not None:
            output(hp - 1, *pending)
        pending = cur
    output(HP - 1, *pending)


def _attn_a_body(sink_ref, q_ref, kp_ref, kc_ref, kn_ref, kx_ref, vp_ref, vc_ref, vn_ref, vx_ref, o_ref, *, nb):
    b = pl.program_id(0)
    first_lat = CTX // QB
    is_lat = b >= first_lat
    prev_ok = b >= first_lat + 1
    next_ok = jnp.logical_and(is_lat, b <= nb - 2)
    nloc = 3 * QB
    nkeys = nloc + CTX
    qi = lax.broadcasted_iota(jnp.int32, (QB, nkeys), 0)
    kj = lax.broadcasted_iota(jnp.int32, (QB, nkeys), 1)
    rel = qi + WINDOW - kj
    band = jnp.abs(rel) <= WINDOW
    seg_ok = jnp.where(kj < QB, prev_ok.astype(jnp.int32),
                       jnp.where(kj < 2 * QB, is_lat.astype(jnp.int32), next_ok.astype(jnp.int32)))
    ok = jnp.logical_or(kj >= nloc, jnp.logical_and(band, seg_ok > 0))
    mask_add = jnp.where(ok, 0.0, NEG)
    lo = _lane_lo((QB, LANES))
    zero = jnp.zeros((QB, LANES), BF16)
    def scores(hp):
        kvh = hp // (A_GROUP // 2)
        keys = jnp.concatenate([kp_ref[kvh], kc_ref[kvh], kn_ref[kvh], kx_ref[kvh]], axis=0)
        q2 = q_ref[hp]
        qs = jnp.concatenate([jnp.where(lo, q2, zero), jnp.where(lo, zero, q2)], axis=0)
        return lax.dot_general(qs, keys, (((1,), (1,)), ((), ())), preferred_element_type=F32)

    def probs(hp, s):
        sink = jnp.concatenate([jnp.full((1, QB, 1), sink_ref[2 * hp + hh], F32) for hh in range(2)], axis=0)
        s3 = s.reshape(2, QB, nkeys) + mask_add[None]
        m = jnp.maximum(jnp.max(s3, axis=-1, keepdims=True), sink)
        e = jnp.exp(s3 - m)
        denom = jnp.sum(e, axis=-1, keepdims=True) + jnp.exp(sink - m)
        return e.astype(BF16).reshape(2 * QB, nkeys), denom

    def output(hp, e, denom):
        kvh = hp // (A_GROUP // 2)
        vals = jnp.concatenate([vp_ref[kvh], vc_ref[kvh], vn_ref[kvh], vx_ref[kvh]], axis=0)
        o = jnp.dot(e, vals, preferred_element_type=F32).reshape(2, QB, LANES) / denom
        o_ref[hp] = jnp.where(lo, o[0], o[1]).astype(BF16)

    _pipeline_pairs(scores, probs, output)


def _attn_a_call(qkv3, sink):
    n = qkv3.shape[1]
    nb = n // QB
    first_lat = CTX // QB
    kblk, vblk = HP // A_KV_HEADS, HP // A_KV_HEADS + 1

    def kv_specs(blk):
        return [pl.BlockSpec((A_KV_HEADS, QB, LANES), lambda b: (blk, jnp.maximum(b - 1, first_lat), 0)),
                pl.BlockSpec((A_KV_HEADS, QB, LANES), lambda b: (blk, b, 0)),
                pl.BlockSpec((A_KV_HEADS, QB, LANES), lambda b: (blk, jnp.minimum(b + 1, nb - 1), 0)),
                pl.BlockSpec((A_KV_HEADS, CTX, LANES), lambda b: (blk, 0, 0))]

    return pl.pallas_call(
        functools.partial(_attn_a_body, nb=nb),
        grid=(nb,),
        in_specs=[pl.BlockSpec(memory_space=pltpu.SMEM),
                  pl.BlockSpec((HP, QB, LANES), lambda b: (0, b, 0))] + kv_specs(kblk) + kv_specs(vblk),
        out_specs=pl.BlockSpec((HP, QB, LANES), lambda b: (0, b, 0)),
        out_shape=jax.ShapeDtypeStruct((HP, n, LANES), BF16),
        compiler_params=_cparams(40),
        name="attn_window",
    )(sink, qkv3, *([qkv3] * 8))


NA_TAB = 2 * NB_KH
RPB_W = 2 * NB_KW - 1
RPB_H = 2 * NB_KH - 1


def _na_table_body(rpb_ref, o_ref):
    h = pl.program_id(0)
    shape = (GRID_W, LANES)
    c = lax.broadcasted_iota(jnp.int32, shape, 0)
    lane = lax.broadcasted_iota(jnp.int32, shape, 1)
    kc = lane & (GRID_W - 1)
    hi = lane >= GRID_W
    cs = jnp.clip(c - NB_KW // 2, 0, GRID_W - NB_KW)
    colok = jnp.logical_and(kc >= cs, kc < cs + NB_KW)
    diff = kc - c + (NB_KW - 1)
    neg = jnp.full(shape, NEG, F32)
    rows = [neg]
    for d in range(RPB_H):
        acc = neg
        for j in range(RPB_W):
            acc = jnp.where(diff == j, rpb_ref[h * (RPB_H * RPB_W) + d * RPB_W + j], acc)
        rows.append(jnp.where(colok, acc, NEG))
    rows.append(neg)
    for t in range(NA_TAB):
        o_ref[0, t] = jnp.where(hi, rows[t + 1], rows[t])


def _na_table_call(rpb):
    return pl.pallas_call(
        _na_table_body,
        grid=(HEADS,),
        in_specs=[pl.BlockSpec(memory_space=pltpu.SMEM)],
        out_specs=pl.BlockSpec((1, NA_TAB, GRID_W, LANES), lambda h: (h, 0, 0, 0)),
        out_shape=jax.ShapeDtypeStruct((HEADS, NA_TAB, GRID_W, LANES), F32),
        compiler_params=_cparams(16),
        name="na_bias_table",
    )(rpb.reshape(-1))


def _attn_b_body(tab_ref, q_ref, k0, k1, k2, k3, k4, kx_ref, v0, v1, v2, v3, v4, vx_ref, o_ref, *, nbl):
    b = pl.program_id(0)
    first_lat = CTX // QB
    is_lat = b >= first_lat
    bl = jnp.maximum(b - first_lat, 0)
    ws = jnp.clip(bl - 2, 0, nbl - NA_KBLK)
    n_rows = 2 * nbl
    nloc = NA_KBLK * QB
    rs = [jnp.clip(2 * bl + qr - NB_KH // 2, 0, n_rows - NB_KH) for qr in range(2)]
    tidx = [[jnp.clip(2 * (ws + j) - (2 * bl + qr) + (NB_KH - 1), -1, NA_TAB - 2) + 1 for j in range(NA_KBLK)]
            for qr in range(2)]
    qrow = lax.broadcasted_iota(jnp.int32, (QB, nloc), 0)
    kcol = lax.broadcasted_iota(jnp.int32, (QB, nloc), 1)
    krow = 2 * ws + jnp.right_shift(kcol, 6)
    rs_q = jnp.where(qrow < GRID_W, rs[0], rs[1])
    row_ok = jnp.logical_and(jnp.logical_and(krow >= rs_q, krow < rs_q + NB_KH), is_lat)
    row_ok2 = jnp.concatenate([row_ok, row_ok], axis=0)
    lo = _lane_lo((QB, LANES))
    zero = jnp.zeros((QB, LANES), BF16)
    kl = (k0, k1, k2, k3, k4)
    vl = (v0, v1, v2, v3, v4)

    def scores(hp):
        q2 = q_ref[hp]
        qs = jnp.concatenate([jnp.where(lo, q2, zero), jnp.where(lo, zero, q2)], axis=0)
        keys = jnp.concatenate([r[hp] for r in kl] + [kx_ref[hp]], axis=0)
        return lax.dot_general(qs, keys, (((1,), (1,)), ((), ())), preferred_element_type=F32)

    def probs(hp, s):
        bias = jnp.concatenate(
            [jnp.concatenate([tab_ref[2 * hp + hh, tidx[qr][j]] for j in range(NA_KBLK)], axis=1)
             for hh in range(2) for qr in range(2)], axis=0)
        s_loc = jnp.where(row_ok2, s[:, :nloc] + bias, NEG)
        s_ctx = s[:, nloc:]
        m = jnp.maximum(jnp.max(s_loc, axis=-1, keepdims=True), jnp.max(s_ctx, axis=-1, keepdims=True))
        e_loc = jnp.exp(s_loc - m)
        e_ctx = jnp.exp(s_ctx - m)
        denom = jnp.sum(e_loc, axis=-1, keepdims=True) + jnp.sum(e_ctx, axis=-1, keepdims=True)
        return jnp.concatenate([e_loc, e_ctx], axis=1).astype(BF16), denom

    def output(hp, p, denom):
        vals = jnp.concatenate([r[hp] for r in vl] + [vx_ref[hp]], axis=0)
        o = jnp.dot(p, vals, preferred_element_type=F32) / denom
        o_ref[hp] = jnp.where(lo, o[:QB], o[QB:]).astype(BF16)

    _pipeline_pairs(scores, probs, output)


def _attn_b_call(qkv3, table):
    n = qkv3.shape[1]
    nb = n // QB
    first_lat = CTX // QB
    nbl = nb - first_lat

    def win(blk, i):
        return pl.BlockSpec(
            (HP, QB, LANES),
            lambda b: (blk, first_lat + jnp.clip(jnp.maximum(b - first_lat, 0) - 2, 0, nbl - NA_KBLK) + i, 0))

    def kv_specs(blk):
        return [win(blk, i) for i in range(NA_KBLK)] + [pl.BlockSpec((HP, CTX, LANES), lambda b: (blk, 0, 0))]

    return pl.pallas_call(
        functools.partial(_attn_b_body, nbl=nbl),
        grid=(nb,),
        in_specs=[_resident((HEADS, NA_TAB, GRID_W, LANES), lambda b: (0, 0, 0, 0)),
                  pl.BlockSpec((HP, QB, LANES), lambda b: (0, b, 0))] + kv_specs(1) + kv_specs(2),
        out_specs=pl.BlockSpec((HP, QB, LANES), lambda b: (0, b, 0)),
        out_shape=jax.ShapeDtypeStruct((HP, n, LANES), BF16),
        compiler_params=_cparams(52),
        name="attn_neighbourhood",
    )(table, qkv3, *([qkv3] * 12))


def _oproj_body(*refs, mode):
    if mode == "attn":
        o_ref, w_ref, h_ref, mod_ref, out_ref = refs
        z = jnp.concatenate([o_ref[c] for c in range(HP)], axis=1)
    else:
        ug_ref, hs_ref, w_ref, h_ref, mod_ref, out_ref = refs
        z = (jax.nn.gelu(ug_ref[...]) * hs_ref[...]).astype(BF16)
    y = jnp.dot(z, w_ref[...], preferred_element_type=F32)
    out_ref[...] = h_ref[...] + mod_ref[0, 2:3, :] * y


def _oproj_call(mix_in, w, h, mod, *, mode, name):
    n = h.shape[0]
    kdim = w.shape[0]
    if mode == "attn":
        in_specs = [pl.BlockSpec((HP, TM, LANES), lambda i: (0, i, 0))]
        args = [mix_in]
    else:
        u, hs = mix_in
        in_specs = [pl.BlockSpec((TM, D_RNN), lambda i: (i, 0)), pl.BlockSpec((TM, D_RNN), lambda i: (i, 0))]
        args = [u, hs]
    in_specs += [_resident((kdim, D), lambda i: (0, 0)), pl.BlockSpec((TM, D), lambda i: (i, 0)), _mod_spec()]
    return pl.pallas_call(
        functools.partial(_oproj_body, mode=mode),
        grid=(n // TM,),
        in_specs=in_specs,
        out_specs=pl.BlockSpec((TM, D), lambda i: (i, 0)),
        out_shape=jax.ShapeDtypeStruct((n, D), F32),
        compiler_params=_cparams(48),
        name=name,
    )(*args, w, h, mod[:, 0:3])


def _shift_rows(x, s, fill, reverse):
    t = x.shape[0]
    if s % SUBLANES == 0:
        pad = jnp.full((s, x.shape[1]), fill, x.dtype)
        return jnp.concatenate([x[s:], pad] if reverse else [pad, x[:t - s]], axis=0)
    row = lax.broadcasted_iota(jnp.int32, x.shape, 0)
    if reverse:
        return jnp.where(row >= t - s, fill, pltpu.roll(x, t - s, 0))
    return jnp.where(row < s, fill, pltpu.roll(x, s, 0))


def _chunk_scan(a, b, reverse):
    s = 1
    while s < a.shape[0]:
        a_s = _shift_rows(a, s, 1.0, reverse)
        b_s = _shift_rows(b, s, 0.0, reverse)
        b = a * b_s + b
        a = a * a_s
        s *= 2
    return a, b


def _lru_body(x_ref, cw_ref, cb_ref, wa_ref, wx_ref, ba_ref, bx_ref, lam_ref, o_ref, *, nchunks):
    n = nchunks * SCAN_TC
    halo = SUBLANES
    cw = cw_ref[...]
    cb = cb_ref[...]
    sp = jax.nn.softplus(-lam_ref[...])

    def conv_chunk(ci):
        r0 = pl.multiple_of(ci * SCAN_TC, SCAN_TC)
        lat_first = CTX // SCAN_TC
        prev_ok = jnp.logical_and(ci != 0, ci != lat_first).astype(F32)
        next_ok = jnp.logical_and(ci != lat_first - 1, ci != nchunks - 1).astype(F32)
        p0 = pl.multiple_of(jnp.maximum(r0 - halo, 0), halo)
        n0 = pl.multiple_of(jnp.minimum(r0 + SCAN_TC, n - halo), halo)
        xe = jnp.concatenate([x_ref[pl.ds(p0, halo), :] * prev_ok,
                              x_ref[pl.ds(r0, SCAN_TC), :],
                              x_ref[pl.ds(n0, halo), :] * next_ok], axis=0)
        y = cb
        for j in range(CONV_W):
            off = halo + j - CONV_LEFT
            y = y + xe[off:off + SCAN_TC, :] * cw[j:j + 1, :]
        return r0, y

    def direction(d, ci, h_in):
        r0, xc = conv_chunk(ci)
        xb = xc.astype(BF16)
        r = jax.nn.sigmoid(jnp.dot(xb, wa_ref[d, 0], preferred_element_type=F32) + ba_ref[d:d + 1, :])
        ig = jax.nn.sigmoid(jnp.dot(xb, wx_ref[d, 0], preferred_element_type=F32) + bx_ref[d:d + 1, :])
        log_a = -LRU_C * r * sp[d:d + 1, :]
        a = jnp.exp(log_a)
        bb = jnp.sqrt(1.0 - a * a) * (ig * xc)
        a_cum, h = _chunk_scan(a, bb, reverse=(d == 1))
        h = h + a_cum * h_in
        h_out = h[SCAN_TC - 1:SCAN_TC, :] if d == 0 else h[0:1, :]
        return r0, h, h_out

    def fwd(ci, h_in):
        r0, h, h_out = direction(0, ci, h_in)
        o_ref[pl.ds(r0, SCAN_TC), :] = h
        return h_out

    def bwd(ci, h_in):
        r0, h, h_out = direction(1, ci, h_in)
        o_ref[pl.ds(r0, SCAN_TC), :] += h
        return h_out

    h0 = jnp.zeros((1, RNN_BW), F32)
    lax.fori_loop(0, nchunks, fwd, h0)
    nctx = CTX // SCAN_TC
    hc = lax.fori_loop(0, nctx, lambda k, hh: bwd(nctx - 1 - k, hh), h0)
    lax.fori_loop(0, nchunks - nctx, lambda k, hh: bwd(nchunks - 1 - k, hh), hc)


def _lru_call(u, conv_w, conv_b, wa, wx, ba, bx, lam):
    n = u.shape[0]
    return pl.pallas_call(
        functools.partial(_lru_body, nchunks=n // SCAN_TC),
        grid=(RNN_BLOCKS,),
        in_specs=[_resident((n, RNN_BW), lambda m: (0, RNN_BLOCKS + m)),
                  pl.BlockSpec((CONV_W, RNN_BW), lambda m: (0, m)),
                  pl.BlockSpec((1, RNN_BW), lambda m: (0, m)),
                  pl.BlockSpec((2, 1, RNN_BW, RNN_BW), lambda m: (0, m, 0, 0)),
                  pl.BlockSpec((2, 1, RNN_BW, RNN_BW), lambda m: (0, m, 0, 0)),
                  pl.BlockSpec((2, RNN_BW), lambda m: (0, m)),
                  pl.BlockSpec((2, RNN_BW), lambda m: (0, m)),
                  pl.BlockSpec((2, RNN_BW), lambda m: (0, m))],
        out_specs=_resident((n, RNN_BW), lambda m: (0, m)),
        out_shape=jax.ShapeDtypeStruct((n, D_RNN), F32),
        compiler_params=_cparams(52),
        name="rglru_scan",
    )(u, conv_w, conv_b.reshape(1, D_RNN), wa, wx, ba, bx, lam)


def _pack_rows(ref, x):
    rows = x.shape[0]
    lo = lax.bitcast_convert_type(x[:, :D // 2].astype(BF16).astype(F32), jnp.uint32)
    hi = lax.bitcast_convert_type(x[:, D // 2:].astype(BF16).astype(F32), jnp.uint32)
    w = hi | (lo >> 16)
    for s in range(XROW_TILE):
        ref[pl.ds(s, rows, stride=XROW_TILE), :] = w[:, s * LANES:(s + 1) * LANES]


def _unpack_rows(ref, rows, dtype):
    w = jnp.concatenate([ref[pl.ds(s, rows, stride=XROW_TILE), :] for s in range(XROW_TILE)], axis=1)
    lo = lax.bitcast_convert_type(w << 16, F32)
    hi = lax.bitcast_convert_type(w & jnp.uint32(0xFFFF0000), F32)
    return jnp.concatenate([lo, hi], axis=1).astype(dtype)


def _router_body(h_ref, g_ref, mod_ref, wrh_ref, wrl_ref, br_ref, f_ref, r_ref, cnt_ref, base_ref):
    @pl.when(pl.program_id(0) == 0)
    def _():
        base_ref[...] = jnp.zeros(base_ref.shape, F32)

    f = _norm_mod(h_ref[...], g_ref[...], mod_ref[0, 0:1, :], mod_ref[0, 1:2, :])
    _pack_rows(f_ref, f)
    f_hi = f.astype(BF16)
    f_lo = (f - f_hi.astype(F32)).astype(BF16)
    logits = (jnp.dot(jnp.concatenate([f_hi, f_lo], axis=0), wrh_ref[...], preferred_element_type=F32).reshape(2, TM, LANES).sum(axis=0)
              + jnp.dot(f_hi, wrl_ref[...], preferred_element_type=F32) + br_ref[...])
    lane = lax.broadcasted_iota(jnp.int32, logits.shape, 1)
    ninf = -jnp.inf
    is_g = lane < N_GROUPS
    gl = jnp.where(is_g, logits, ninf)
    gmax = jnp.max(gl, axis=-1, keepdims=True)
    gsel = jnp.min(jnp.where(gl == gmax, lane, LANES), axis=-1, keepdims=True)
    gsum = jnp.sum(jnp.where(is_g, jnp.exp(gl - gmax), 0.0), axis=-1, keepdims=True)
    g_w = 1.0 / gsum
    e_lo = N_GROUPS + EXP_PER_GROUP * gsel
    in_grp = jnp.logical_and(lane >= e_lo, lane < e_lo + EXP_PER_GROUP)
    el = jnp.where(in_grp, logits, ninf)
    m1 = jnp.max(el, axis=-1, keepdims=True)
    i1 = jnp.min(jnp.where(el == m1, lane, LANES), axis=-1, keepdims=True)
    el2 = jnp.where(lane == i1, ninf, el)
    m2 = jnp.max(el2, axis=-1, keepdims=True)
    i2 = jnp.min(jnp.where(el2 == m2, lane, LANES), axis=-1, keepdims=True)
    t = jnp.exp(m2 - m1)
    w1 = g_w / (1.0 + t)
    w2 = g_w * t / (1.0 + t)
    oh1 = lane == i1
    oh2 = lane == i2
    rr = lax.broadcasted_iota(jnp.int32, (TM, TM), 0)
    cc = lax.broadcasted_iota(jnp.int32, (TM, TM), 1)
    tri = (cc < rr).astype(BF16)
    pre1 = jnp.dot(tri, oh1.astype(BF16), preferred_element_type=F32)
    pre2 = jnp.dot(tri, oh2.astype(BF16), preferred_element_type=F32)
    base = base_ref[...]
    cnt1 = jnp.sum(oh1.astype(F32), axis=0, keepdims=True)
    cnt2 = jnp.sum(oh2.astype(F32), axis=0, keepdims=True)
    rank1 = jnp.sum(jnp.where(oh1, pre1 + base, 0.0), axis=-1, keepdims=True)
    rank2 = jnp.sum(jnp.where(oh2, pre2 + (base + cnt1), 0.0), axis=-1, keepdims=True)
    total = base + cnt1 + cnt2
    base_ref[...] = total
    cnt_ref[...] = total
    cols = [(i1 - N_GROUPS).astype(F32), (i2 - N_GROUPS).astype(F32), w1, w2, rank1, rank2]
    out = jnp.zeros(logits.shape, F32)
    for k, v in enumerate(cols):
        out = jnp.where(lane == k, v, out)
    r_ref[...] = out


def _router_call(h, g, mod, wr, br):
    n = h.shape[0]
    wr_hi = wr.astype(BF16)
    wr_lo = (wr - wr_hi.astype(F32)).astype(BF16)
    return pl.pallas_call(
        _router_body,
        grid=(n // TM,),
        in_specs=[pl.BlockSpec((TM, D), lambda i: (i, 0)),
                  pl.BlockSpec((1, D), lambda i: (0, 0)),
                  _mod_spec(),
                  pl.BlockSpec((D, LANES), lambda i: (0, 0)),
                  pl.BlockSpec((D, LANES), lambda i: (0, 0)),
                  pl.BlockSpec((1, LANES), lambda i: (0, 0))],
        out_specs=[pl.BlockSpec((TM * XROW_TILE, LANES), lambda i: (i, 0)),
                   pl.BlockSpec((TM, LANES), lambda i: (i, 0)),
                   pl.BlockSpec((1, LANES), lambda i: (0, 0))],
        out_shape=[jax.ShapeDtypeStruct((n * XROW_TILE, LANES), jnp.uint32),
                   jax.ShapeDtypeStruct((n, LANES), F32),
                   jax.ShapeDtypeStruct((1, LANES), F32)],
        scratch_shapes=[pltpu.VMEM((1, LANES), F32)],
        compiler_params=_cparams(32),
        name="moe_router",
    )(h, g.reshape(1, D), mod[:, 3:6], wr_hi, wr_lo, br)


def _dispatch_body(dest_ref, pad_lo_ref, pad_n_ref, f_ref, xg_ref, z_ref, sem):
    i = pl.program_id(0)
    base = i * TM

    def issue(j, carry):
        for k in range(TOP_K):
            d = dest_ref[(base + j) * TOP_K + k]
            pltpu.make_async_copy(f_ref.at[pl.ds(j * XROW_TILE, XROW_TILE)],
                                  xg_ref.at[pl.ds(d * XROW_TILE, XROW_TILE)], sem).start(priority=k)
        return carry

    def wait_one(carry):
        pltpu.make_async_copy(z_ref, xg_ref.at[pl.ds(0, XROW_TILE)], sem).wait()
        return carry

    lax.fori_loop(0, TM, issue, 0, unroll=DMA_UNROLL)

    @pl.when(i == 0)
    def _():
        z_ref[...] = jnp.zeros(z_ref.shape, jnp.uint32)

        def fill(e, carry):
            lo = pad_lo_ref[e]
            n_fill = pad_n_ref[e]

            def batch(bi, c):
                s0 = bi * MOE_MB
                cnt = jnp.minimum(n_fill - s0, MOE_MB)

                def one(s, c2):
                    pltpu.make_async_copy(z_ref, xg_ref.at[pl.ds((lo + s0 + s) * XROW_TILE, XROW_TILE)], sem).start()
                    return c2

                lax.fori_loop(0, cnt, one, 0)
                lax.fori_loop(0, cnt, lambda s, c2: wait_one(c2), 0)
                return c

            lax.fori_loop(0, (n_fill + (MOE_MB - 1)) // MOE_MB, batch, 0)
            return carry

        lax.fori_loop(0, N_EXPERTS + 1, fill, 0)

    lax.fori_loop(0, TM * TOP_K, lambda j, c: wait_one(c), 0, unroll=DMA_UNROLL)


def _dispatch_call(dest, pad_lo, pad_n, f2, p):
    n = f2.shape[0] // XROW_TILE
    return pl.pallas_call(
        _dispatch_body,
        grid_spec=pltpu.PrefetchScalarGridSpec(
            num_scalar_prefetch=3,
            grid=(n // TM,),
            in_specs=[pl.BlockSpec((TM * XROW_TILE, LANES), lambda i, d, lo, nn: (i, 0))],
            out_specs=pl.BlockSpec(memory_space=pl.ANY),
            scratch_shapes=[pltpu.VMEM((XROW_TILE, LANES), jnp.uint32), pltpu.SemaphoreType.DMA(())]),
        out_shape=jax.ShapeDtypeStruct((p * XROW_TILE, LANES), jnp.uint32),
        compiler_params=_cparams(16),
        name="moe_dispatch",
    )(dest, pad_lo, pad_n, f2)


def _ffn_up_body(be_ref, nu_ref, x_ref, w_ref, a_ref, wb_ref):
    b = pl.program_id(0)
    changed = jnp.logical_or(b == 0, be_ref[b] != be_ref[jnp.maximum(b - 1, 0)])

    @pl.when(changed)
    def _():
        wb_ref[...] = w_ref[0, 0].astype(BF16)

    @pl.when(b < nu_ref[0])
    def _():
        hgu = jnp.dot(_unpack_rows(x_ref, MOE_MB, BF16), wb_ref[...], preferred_element_type=F32)
        g = hgu[:, :D_EXPERT]
        u = hgu[:, D_EXPERT:]
        a_ref[...] = (g * jax.nn.sigmoid(g) * u).astype(BF16)

    @pl.when(b >= nu_ref[0])
    def _():
        a_ref[...] = jnp.zeros(a_ref.shape, BF16)


def _ffn_up_call(block_e, n_used, xg, w_gu, layer):
    p = xg.shape[0] // XROW_TILE
    return pl.pallas_call(
        _ffn_up_body,
        grid_spec=pltpu.PrefetchScalarGridSpec(
            num_scalar_prefetch=2,
            grid=(p // MOE_MB,),
            in_specs=[pl.BlockSpec((MOE_MB * XROW_TILE, LANES), lambda b, be, nu: (jnp.minimum(b, nu[0] - 1), 0)),
                      pl.BlockSpec((1, 1, D, 2 * D_EXPERT), lambda b, be, nu: (layer, be[b], 0, 0))],
            out_specs=pl.BlockSpec((MOE_MB, D_EXPERT), lambda b, be, nu: (b, 0)),
            scratch_shapes=[pltpu.VMEM((D, 2 * D_EXPERT), BF16)]),
        out_shape=jax.ShapeDtypeStruct((p, D_EXPERT), BF16),
        compiler_params=_cparams(50),
        name="moe_ffn_up",
    )(block_e, n_used, xg, w_gu)


def _ffn_down_body(be_ref, nu_ref, a_ref, w_ref, y_ref, wb_ref):
    b = pl.program_id(0)
    changed = jnp.logical_or(b == 0, be_ref[b] != be_ref[jnp.maximum(b - 1, 0)])

    @pl.when(changed)
    def _():
        wb_ref[...] = w_ref[0, 0].astype(BF16)

    @pl.when(b < nu_ref[0])
    def _():
        _pack_rows(y_ref, jnp.dot(a_ref[...], wb_ref[...], preferred_element_type=F32))

    @pl.when(b >= nu_ref[0])
    def _():
        y_ref[...] = jnp.zeros(y_ref.shape, jnp.uint32)


def _ffn_down_call(block_e, n_used, act, w_down, layer):
    p = act.shape[0]
    return pl.pallas_call(
        _ffn_down_body,
        grid_spec=pltpu.PrefetchScalarGridSpec(
            num_scalar_prefetch=2,
            grid=(p // MOE_MB,),
            in_specs=[pl.BlockSpec((MOE_MB, D_EXPERT), lambda b, be, nu: (b, 0)),
                      pl.BlockSpec((1, 1, D_EXPERT, D), lambda b, be, nu: (layer, be[b], 0, 0))],
            out_specs=pl.BlockSpec((MOE_MB * XROW_TILE, LANES), lambda b, be, nu: (b, 0)),
            scratch_shapes=[pltpu.VMEM((D_EXPERT, D), BF16)]),
        out_shape=jax.ShapeDtypeStruct((p * XROW_TILE, LANES), jnp.uint32),
        compiler_params=_cparams(40),
        name="moe_ffn_down",
    )(block_e, n_used, act, w_down)


def _combine_body(dest_ref, h_ref, r_ref, mod_ref, g_ref, yb_ref, o_ref, buf_ref, sem, *, final):
    i = pl.program_id(0)
    slot = i % 2

    def gather(tile, dst_slot):
        base = tile * TM

        def issue(j, carry):
            for k in range(TOP_K):
                d = dest_ref[(base + j) * TOP_K + k]
                pltpu.make_async_copy(yb_ref.at[pl.ds(d * XROW_TILE, XROW_TILE)],
                                      buf_ref.at[dst_slot, k, pl.ds(j * XROW_TILE, XROW_TILE)],
                                      sem.at[dst_slot]).start(priority=k)
            return carry

        lax.fori_loop(0, TM, issue, 0, unroll=DMA_UNROLL)

    @pl.when(i == 0)
    def _():
        gather(0, 0)

    @pl.when(i + 1 < pl.num_programs(0))
    def _():
        gather(i + 1, 1 - slot)

    def drain(j, carry):
        pltpu.make_async_copy(yb_ref.at[pl.ds(0, XROW_TILE)], buf_ref.at[slot, 0, pl.ds(0, XROW_TILE)],
                              sem.at[slot]).wait()
        return carry

    lax.fori_loop(0, TM * TOP_K, drain, 0, unroll=DMA_UNROLL)
    r = r_ref[...]
    y = sum(r[:, TOP_K + k:TOP_K + k + 1] * _unpack_rows(buf_ref.at[slot, k], TM, F32) for k in range(TOP_K))
    h_new = h_ref[...] + mod_ref[0, 2:3, :] * y
    if final:
        ms = jnp.mean(h_new * h_new, axis=-1, keepdims=True)
        h_new = h_new * lax.rsqrt(ms + EPS) * g_ref[...]
    o_ref[...] = h_new


def _combine_call(dest, h, route, mod, yb, final_g=None):
    n = h.shape[0]
    final = final_g is not None
    first = CTX // TM
    out_rows = n - CTX if final else n
    out_map = (lambda i, d: (jnp.maximum(i - first, 0), 0)) if final else (lambda i, d: (i, 0))
    g = final_g if final else jnp.ones((D,), F32)
    return pl.pallas_call(
        functools.partial(_combine_body, final=final),
        grid_spec=pltpu.PrefetchScalarGridSpec(
            num_scalar_prefetch=1,
            grid=(n // TM,),
            in_specs=[pl.BlockSpec((TM, D), lambda i, d: (i, 0)),
                      pl.BlockSpec((TM, LANES), lambda i, d: (i, 0)),
                      pl.BlockSpec((1, 3, D), lambda i, d: (jnp.minimum(i, 1), 0, 0)),
                      pl.BlockSpec((1, D), lambda i, d: (0, 0)),
                      pl.BlockSpec(memory_space=pl.ANY)],
            out_specs=pl.BlockSpec((TM, D), out_map),
            scratch_shapes=[pltpu.VMEM((2, TOP_K, TM * XROW_TILE, LANES), jnp.uint32),
                            pltpu.SemaphoreType.DMA((2,))]),
        out_shape=jax.ShapeDtypeStruct((out_rows, D), F32),
        compiler_params=_cparams(32),
        name="moe_combine",
    )(dest, h, route, mod[:, 3:6], g.reshape(1, D), yb)


def _dispatch_plan(route, cnt, n):
    experts = jnp.arange(N_EXPERTS, dtype=jnp.int32)
    counts = cnt[0, N_GROUPS:N_GROUPS + N_EXPERTS].astype(jnp.int32)
    padded = (counts + MOE_MB - 1) // MOE_MB * MOE_MB
    end_pad = jnp.cumsum(padded)
    start_pad = end_pad - padded
    n_blocks = -(-(n * TOP_K + N_EXPERTS * (MOE_MB - 1)) // MOE_MB)
    p = n_blocks * MOE_MB
    blk_start = jnp.arange(n_blocks, dtype=jnp.int32) * MOE_MB
    block_e = jnp.minimum(jnp.sum(end_pad[None, :] <= blk_start[:, None], axis=1), N_EXPERTS - 1).astype(jnp.int32)
    n_used = (end_pad[-1:] // MOE_MB).astype(jnp.int32)
    e_idx = route[:, 0:TOP_K].astype(jnp.int32)
    rank = route[:, 2 * TOP_K:3 * TOP_K].astype(jnp.int32)
    start = jnp.sum(jnp.where(e_idx[..., None] == experts, start_pad, 0), axis=-1)
    dest = (start + rank).reshape(n * TOP_K)
    pad_lo = jnp.concatenate([start_pad + counts, end_pad[-1:]]).astype(jnp.int32)
    pad_n = jnp.concatenate([padded - counts, p - end_pad[-1:]]).astype(jnp.int32)
    return p, block_e, n_used, dest, pad_lo, pad_n


def _moe_layer(h, g, mod, wr, br, w_gu, w_down, layer, final_g=None):
    n = h.shape[0]
    f2, route, cnt = _router_call(h, g, mod, wr, br)
    p, block_e, n_used, dest, pad_lo, pad_n = _dispatch_plan(route, cnt, n)
    xg = _dispatch_call(dest, pad_lo, pad_n, f2, p)
    act = _ffn_up_call(block_e, n_used, xg, w_gu, layer)
    yb = _ffn_down_call(block_e, n_used, act, w_down, layer)
    return _combine_call(dest, h, route, mod, yb, final_g)


def _rope_tables(l):
    quarter = HEAD_DIM // 4
    inv = ROPE_BASE ** (-jnp.arange(quarter, dtype=F32) / quarter)
    n_rows = l // GRID_W
    ang_r = jnp.arange(n_rows, dtype=F32)[:, None] * inv
    ang_c = jnp.arange(GRID_W, dtype=F32)[:, None] * inv
    by_row = lambda t: jnp.broadcast_to(t[:, None, :], (n_rows, GRID_W, quarter)).reshape(l, quarter)
    by_col = lambda t: jnp.broadcast_to(t[None, :, :], (n_rows, GRID_W, quarter)).reshape(l, quarter)
    cr, sr, cc, sc = by_row(jnp.cos(ang_r)), by_row(jnp.sin(ang_r)), by_col(jnp.cos(ang_c)), by_col(jnp.sin(ang_c))
    cos = jnp.concatenate([cr, cr, cc, cc], axis=1)
    sin = jnp.concatenate([-sr, sr, -sc, sc], axis=1)
    cos = jnp.concatenate([jnp.ones((CTX, HEAD_DIM), F32), cos], axis=0)
    sin = jnp.concatenate([jnp.zeros((CTX, HEAD_DIM), F32), sin], axis=0)
    return jnp.tile(cos, (1, 2)), jnp.tile(sin, (1, 2))


def _attn_a_weights(w_qkv):
    nq = HEADS * HEAD_DIM
    nkv = A_KV_HEADS * HEAD_DIM
    wq = w_qkv[:, :nq]
    wk = w_qkv[:, nq:nq + nkv].reshape(D, A_KV_HEADS, 1, HEAD_DIM)
    wv = w_qkv[:, nq + nkv:].reshape(D, A_KV_HEADS, 1, HEAD_DIM)
    dup = lambda w: jnp.broadcast_to(w, (D, A_KV_HEADS, 2, HEAD_DIM)).reshape(D, 2 * nkv)
    return jnp.concatenate([wq, dup(wk), dup(wv)], axis=1).astype(BF16)


def kernel(x, c, ctx, c_ctx, ada_w, ada_b, norm_mix_g, norm_ffn_g, router_group_w, router_group_b,
           router_expert_w, router_expert_b, moe_w_gu, moe_w_down, attn_w_qkv, attn_w_o, attn_sink,
           na_w_qkv, na_w_o, na_rpb, rnn_w_in, rnn_conv_w, rnn_conv_b, rnn_wa, rnn_ba, rnn_wx, rnn_bx,
           rnn_lam, rnn_w_out, final_norm_g):
    batch, l, _ = x.shape
    assert batch == 1 and ctx.shape[1] == CTX and l % (QB * 2) == 0 and l // QB >= NA_KBLK
    h = jnp.concatenate([ctx[0], x[0]], axis=0)
    c2 = jnp.stack([c_ctx, c[0]], axis=1)
    mods = _ada_call(c2, ada_w, ada_b).reshape(DEPTH, 2, 6, D)
    rope = _rope_tables(l)
    pad_r = LANES - N_GROUPS - N_EXPERTS
    for i in range(DEPTH):
        kind, j = i % 3, i // 3
        mod = mods[i]
        if kind == 0:
            w = _attn_a_weights(attn_w_qkv[j])
            nq = HEADS * HEAD_DIM
            qkv3 = _proj_call(h, norm_mix_g[i], mod, w, mode="rope", rope=rope,
                              n_rope=nq + 2 * A_KV_HEADS * HEAD_DIM, n_q=nq, name="proj_window")
            o3 = _attn_a_call(qkv3, attn_sink[j])
            h = _oproj_call(o3, attn_w_o[j].astype(BF16), h, mod, mode="attn", name="oproj_window")
        elif kind == 1:
            qkv3 = _proj_call(h, norm_mix_g[i], mod, na_w_qkv[j].astype(BF16), mode="cols",
                              n_q=HEADS * HEAD_DIM, name="proj_neighbourhood")
            o3 = _attn_b_call(qkv3, _na_table_call(na_rpb[j]))
            h = _oproj_call(o3, na_w_o[j].astype(BF16), h, mod, mode="attn", name="oproj_neighbourhood")
        else:
            u = _proj_call(h, norm_mix_g[i], mod, rnn_w_in[j].astype(BF16), mode="plain", name="proj_rglru")
            hs = _lru_call(u, rnn_conv_w[j], rnn_conv_b[j], rnn_wa[j].astype(BF16), rnn_wx[j].astype(BF16),
                           rnn_ba[j], rnn_bx[j], rnn_lam[j])
            h = _oproj_call((u, hs), rnn_w_out[j].astype(BF16), h, mod, mode="rnn", name="oproj_rglru")
        wr = jnp.concatenate([router_group_w[i], router_expert_w[i], jnp.zeros((D, pad_r), F32)], axis=1)
        br = jnp.concatenate([router_group_b[i], router_expert_b[i], jnp.zeros((pad_r,), F32)]).reshape(1, LANES)
        h = _moe_layer(h, norm_ffn_g[i], mod, wr, br, moe_w_gu, moe_w_down, i,
                       final_norm_g if i == DEPTH - 1 else None)
    return h[None]
```

```python
import functools

import jax
import jax.numpy as jnp
from jax import lax
from jax.experimental import pallas as pl
from jax.experimental.pallas import tpu as pltpu

F32 = jnp.float32
BF16 = jnp.bfloat16

D = 2048
DEPTH = 4
GRID_W = 64
CTX = 256
HEADS = 32
HEAD_DIM = 64
A_KV_HEADS = 4
A_GROUP = HEADS // A_KV_HEADS
WINDOW = 128
NB_KH = 8
NB_KW = 16
D_RNN = 2560
RNN_BLOCKS = 10
RNN_BW = D_RNN // RNN_BLOCKS
CONV_W = 4
CONV_LEFT = 2
LRU_C = 8.0
N_GROUPS = 4
EXP_PER_GROUP = 8
N_EXPERTS = N_GROUPS * EXP_PER_GROUP
TOP_K = 2
D_EXPERT = 768
ROPE_BASE = 10000.0
EPS = 1e-6
NEG = -1e30

LANES = 128
SUBLANES = 8
MIB = 1024 * 1024

TM = 256
QB = 128
HP = HEADS // 2
MOE_MB = 256
ROW_TILE = D // LANES
XROW_TILE = ROW_TILE // 2
DMA_UNROLL = 8
SCAN_TC = 256
NA_KBLK = 5
SQRT_SCALE = HEAD_DIM ** -0.5


def _cparams(vmem_mib, sem=("arbitrary",)):
    return pltpu.CompilerParams(dimension_semantics=sem, vmem_limit_bytes=int(vmem_mib * MIB))


def _resident(block_shape, index_map):
    return pl.BlockSpec(block_shape, index_map, pipeline_mode=pl.Buffered(1))


def _mod_spec():
    return pl.BlockSpec((1, 3, D), lambda i: (jnp.minimum(i, 1), 0, 0))


def _norm_mod(x, g, shift, scale):
    ms = jnp.mean(x * x, axis=-1, keepdims=True)
    y = x * lax.rsqrt(ms + EPS) * g
    return y * (1.0 + scale) + shift


ADA_TN = 1024


def _ada_body(c_ref, w_ref, b_ref, o_ref):
    c = c_ref[...]
    cs = c * jax.nn.sigmoid(c)
    for r in range(2):
        cb = jnp.broadcast_to(cs[:, r:r + 1], (D, LANES))
        outs = []
        for j in range(ADA_TN // LANES):
            w = w_ref[0, :, j * LANES:(j + 1) * LANES]
            p = (w * cb).reshape(D // SUBLANES, SUBLANES, LANES).sum(axis=0)
            outs.append(p.sum(axis=0, keepdims=True))
        o_ref[0, r:r + 1, :] = jnp.concatenate(outs, axis=1) + b_ref[0]


def _ada_call(c2, ada_w, ada_b):
    return pl.pallas_call(
        _ada_body,
        grid=(DEPTH, 6 * D // ADA_TN),
        in_specs=[pl.BlockSpec((D, 2), lambda l, j: (0, 0)),
                  pl.BlockSpec((1, D, ADA_TN), lambda l, j: (l, 0, j)),
                  pl.BlockSpec((1, 1, ADA_TN), lambda l, j: (l, 0, j))],
        out_specs=pl.BlockSpec((1, 2, ADA_TN), lambda l, j: (l, 0, j)),
        out_shape=jax.ShapeDtypeStruct((DEPTH, 2, 6 * D), F32),
        compiler_params=_cparams(32, ("arbitrary", "arbitrary")),
        name="ada_mod",
    )(c2, ada_w, ada_b.reshape(DEPTH, 1, 6 * D))


PROJ_CH = 512


def _rope_piece(piece, cos, sin):
    lane = lax.broadcasted_iota(jnp.int32, piece.shape, 1)
    first = (lane & 16) == 0
    partner = jnp.where(first, pltpu.roll(piece, LANES - 16, 1), pltpu.roll(piece, 16, 1))
    return piece * cos + partner * sin


def _proj_body(*refs, mode, n_rope, n_q):
    if mode == "rope":
        x_ref, g_ref, mod_ref, w_ref, cos_ref, sin_ref, o_ref = refs
    else:
        x_ref, g_ref, mod_ref, w_ref, o_ref = refs
    a = _norm_mod(x_ref[...], g_ref[...], mod_ref[0, 0:1, :], mod_ref[0, 1:2, :]).astype(BF16)
    nout = w_ref.shape[1]
    for c in range(nout // PROJ_CH):
        acc = jnp.dot(a, w_ref[:, c * PROJ_CH:(c + 1) * PROJ_CH], preferred_element_type=F32)
        if mode == "plain":
            o_ref[:, c * PROJ_CH:(c + 1) * PROJ_CH] = acc
            continue
        for k in range(PROJ_CH // LANES):
            col0 = c * PROJ_CH + k * LANES
            piece = acc[:, k * LANES:(k + 1) * LANES]
            if col0 < n_rope:
                piece = _rope_piece(piece, cos_ref[...], sin_ref[...])
            if col0 < n_q:
                piece = piece * SQRT_SCALE
            o_ref[col0 // LANES] = piece.astype(BF16)


def _proj_call(h, g, mod, w, *, mode, rope=None, n_rope=0, n_q=0, name):
    n = h.shape[0]
    nout = w.shape[1]
    in_specs = [pl.BlockSpec((TM, D), lambda i: (i, 0)),
                pl.BlockSpec((1, D), lambda i: (0, 0)),
                _mod_spec(),
                _resident((D, nout), lambda i: (0, 0))]
    args = [h, g.reshape(1, D), mod[:, 0:3], w]
    if mode == "rope":
        in_specs += [pl.BlockSpec((TM, LANES), lambda i: (i, 0))] * 2
        args += list(rope)
    if mode == "plain":
        out_spec = pl.BlockSpec((TM, nout), lambda i: (i, 0))
        out_shape = jax.ShapeDtypeStruct((n, nout), F32)
        out_bytes = TM * nout * 4
    else:
        out_spec = pl.BlockSpec((nout // LANES, TM, LANES), lambda i: (0, i, 0))
        out_shape = jax.ShapeDtypeStruct((nout // LANES, n, LANES), BF16)
        out_bytes = TM * nout * 2
    vmem = (D * nout * 2 + 2 * TM * D * 4 + 2 * out_bytes) / MIB + 12
    return pl.pallas_call(
        functools.partial(_proj_body, mode=mode, n_rope=n_rope, n_q=n_q),
        grid=(n // TM,),
        in_specs=in_specs,
        out_specs=out_spec,
        out_shape=out_shape,
        compiler_params=_cparams(vmem),
        name=name,
    )(*args)


def _lane_lo(shape):
    return lax.broadcasted_iota(jnp.int32, shape, len(shape) - 1) < HEAD_DIM


def _pipeline_pairs(scores, probs, output):
    s_next = scores(0)
    pending = None
    for hp in range(HP):
        s_cur = s_next
        if hp + 1 < HP:
            s_next = scores(hp + 1)
        cur = probs(hp, s_cur)
        if pending is not None:
            output(hp - 1, *pending)
        pending = cur
    output(HP - 1, *pending)


def _attn_a_body(sink_ref, q_ref, kp_ref, kc_ref, kn_ref, kx_ref, vp_ref, vc_ref, vn_ref, vx_ref, o_ref, *, nb):
    b = pl.program_id(0)
    first_lat = CTX // QB
    is_lat = b >= first_lat
    prev_ok = b >= first_lat + 1
    next_ok = jnp.logical_and(is_lat, b <= nb - 2)
    nloc = 3 * QB
    nkeys = nloc + CTX
    qi = lax.broadcasted_iota(jnp.int32, (QB, nkeys), 0)
    kj = lax.broadcasted_iota(jnp.int32, (QB, nkeys), 1)
    rel = qi + WINDOW - kj
    band = jnp.abs(rel) <= WINDOW
    seg_ok = jnp.where(kj < QB, prev_ok.astype(jnp.int32),
                       jnp.where(kj < 2 * QB, is_lat.astype(jnp.int32), next_ok.astype(jnp.int32)))
    ok = jnp.logical_or(kj >= nloc, jnp.logical_and(band, seg_ok > 0))
    mask_add = jnp.where(ok, 0.0, NEG)
    lo = _lane_lo((QB, LANES))
    zero = jnp.zeros((QB, LANES), BF16)
    def scores(hp):
        kvh = hp // (A_GROUP // 2)
        keys = jnp.concatenate([kp_ref[kvh], kc_ref[kvh], kn_ref[kvh], kx_ref[kvh]], axis=0)
        q2 = q_ref[hp]
        qs = jnp.concatenate([jnp.where(lo, q2, zero), jnp.where(lo, zero, q2)], axis=0)
        return lax.dot_general(qs, keys, (((1,), (1,)), ((), ())), preferred_element_type=F32)

    def probs(hp, s):
        sink = jnp.concatenate([jnp.full((1, QB, 1), sink_ref[2 * hp + hh], F32) for hh in range(2)], axis=0)
        s3 = s.reshape(2, QB, nkeys) + mask_add[None]
        m = jnp.maximum(jnp.max(s3, axis=-1, keepdims=True), sink)
        e = jnp.exp(s3 - m)
        denom = jnp.sum(e, axis=-1, keepdims=True) + jnp.exp(sink - m)
        return e.astype(BF16).reshape(2 * QB, nkeys), denom

    def output(hp, e, denom):
        kvh = hp // (A_GROUP // 2)
        vals = jnp.concatenate([vp_ref[kvh], vc_ref[kvh], vn_ref[kvh], vx_ref[kvh]], axis=0)
        o = jnp.dot(e, vals, preferred_element_type=F32).reshape(2, QB, LANES) / denom
        o_ref[hp] = jnp.where(lo, o[0], o[1]).astype(BF16)

    _pipeline_pairs(scores, probs, output)


def _attn_a_call(qkv3, sink):
    n = qkv3.shape[1]
    nb = n // QB
    first_lat = CTX // QB
    kblk, vblk = HP // A_KV_HEADS, HP // A_KV_HEADS + 1

    def kv_specs(blk):
        return [pl.BlockSpec((A_KV_HEADS, QB, LANES), lambda b: (blk, jnp.maximum(b - 1, first_lat), 0)),
                pl.BlockSpec((A_KV_HEADS, QB, LANES), lambda b: (blk, b, 0)),
                pl.BlockSpec((A_KV_HEADS, QB, LANES), lambda b: (blk, jnp.minimum(b + 1, nb - 1), 0)),
                pl.BlockSpec((A_KV_HEADS, CTX, LANES), lambda b: (blk, 0, 0))]

    return pl.pallas_call(
        functools.partial(_attn_a_body, nb=nb),
        grid=(nb,),
        in_specs=[pl.BlockSpec(memory_space=pltpu.SMEM),
                  pl.BlockSpec((HP, QB, LANES), lambda b: (0, b, 0))] + kv_specs(kblk) + kv_specs(vblk),
        out_specs=pl.BlockSpec((HP, QB, LANES), lambda b: (0, b, 0)),
        out_shape=jax.ShapeDtypeStruct((HP, n, LANES), BF16),
        compiler_params=_cparams(40),
        name="attn_window",
    )(sink, qkv3, *([qkv3] * 8))


NA_TAB = 2 * NB_KH
RPB_W = 2 * NB_KW - 1
RPB_H = 2 * NB_KH - 1


def _na_table_body(rpb_ref, o_ref):
    h = pl.program_id(0)
    shape = (GRID_W, LANES)
    c = lax.broadcasted_iota(jnp.int32, shape, 0)
    lane = lax.broadcasted_iota(jnp.int32, shape, 1)
    kc = lane & (GRID_W - 1)
    hi = lane >= GRID_W
    cs = jnp.clip(c - NB_KW // 2, 0, GRID_W - NB_KW)
    colok = jnp.logical_and(kc >= cs, kc < cs + NB_KW)
    diff = kc - c + (NB_KW - 1)
    neg = jnp.full(shape, NEG, F32)
    rows = [neg]
    for d in range(RPB_H):
        acc = neg
        for j in range(RPB_W):
            acc = jnp.where(diff == j, rpb_ref[h * (RPB_H * RPB_W) + d * RPB_W + j], acc)
        rows.append(jnp.where(colok, acc, NEG))
    rows.append(neg)
    for t in range(NA_TAB):
        o_ref[0, t] = jnp.where(hi, rows[t + 1], rows[t])


def _na_table_call(rpb):
    return pl.pallas_call(
        _na_table_body,
        grid=(HEADS,),
        in_specs=[pl.BlockSpec(memory_space=pltpu.SMEM)],
        out_specs=pl.BlockSpec((1, NA_TAB, GRID_W, LANES), lambda h: (h, 0, 0, 0)),
        out_shape=jax.ShapeDtypeStruct((HEADS, NA_TAB, GRID_W, LANES), F32),
        compiler_params=_cparams(16),
        name="na_bias_table",
    )(rpb.reshape(-1))


def _attn_b_body(tab_ref, q_ref, k0, k1, k2, k3, k4, kx_ref, v0, v1, v2, v3, v4, vx_ref, o_ref, *, nbl):
    b = pl.program_id(0)
    first_lat = CTX // QB
    is_lat = b >= first_lat
    bl = jnp.maximum(b - first_lat, 0)
    ws = jnp.clip(bl - 2, 0, nbl - NA_KBLK)
    n_rows = 2 * nbl
    nloc = NA_KBLK * QB
    rs = [jnp.clip(2 * bl + qr - NB_KH // 2, 0, n_rows - NB_KH) for qr in range(2)]
    tidx = [[jnp.clip(2 * (ws + j) - (2 * bl + qr) + (NB_KH - 1), -1, NA_TAB - 2) + 1 for j in range(NA_KBLK)]
            for qr in range(2)]
    qrow = lax.broadcasted_iota(jnp.int32, (QB, nloc), 0)
    kcol = lax.broadcasted_iota(jnp.int32, (QB, nloc), 1)
    krow = 2 * ws + jnp.right_shift(kcol, 6)
    rs_q = jnp.where(qrow < GRID_W, rs[0], rs[1])
    row_ok = jnp.logical_and(jnp.logical_and(krow >= rs_q, krow < rs_q + NB_KH), is_lat)
    row_ok2 = jnp.concatenate([row_ok, row_ok], axis=0)
    lo = _lane_lo((QB, LANES))
    zero = jnp.zeros((QB, LANES), BF16)
    kl = (k0, k1, k2, k3, k4)
    vl = (v0, v1, v2, v3, v4)

    def scores(hp):
        q2 = q_ref[hp]
        qs = jnp.concatenate([jnp.where(lo, q2, zero), jnp.where(lo, zero, q2)], axis=0)
        keys = jnp.concatenate([r[hp] for r in kl] + [kx_ref[hp]], axis=0)
        return lax.dot_general(qs, keys, (((1,), (1,)), ((), ())), preferred_element_type=F32)

    def probs(hp, s):
        bias = jnp.concatenate(
            [jnp.concatenate([tab_ref[2 * hp + hh, tidx[qr][j]] for j in range(NA_KBLK)], axis=1)
             for hh in range(2) for qr in range(2)], axis=0)
        s_loc = jnp.where(row_ok2, s[:, :nloc] + bias, NEG)
        s_ctx = s[:, nloc:]
        m = jnp.maximum(jnp.max(s_loc, axis=-1, keepdims=True), jnp.max(s_ctx, axis=-1, keepdims=True))
        e_loc = jnp.exp(s_loc - m)
        e_ctx = jnp.exp(s_ctx - m)
        denom = jnp.sum(e_loc, axis=-1, keepdims=True) + jnp.sum(e_ctx, axis=-1, keepdims=True)
        return jnp.concatenate([e_loc, e_ctx], axis=1).astype(BF16), denom

    def output(hp, p, denom):
        vals = jnp.concatenate([r[hp] for r in vl] + [vx_ref[hp]], axis=0)
        o = jnp.dot(p, vals, preferred_element_type=F32) / denom
        o_ref[hp] = jnp.where(lo, o[:QB], o[QB:]).astype(BF16)

    _pipeline_pairs(scores, probs, output)


def _attn_b_call(qkv3, table):
    n = qkv3.shape[1]
    nb = n // QB
    first_lat = CTX // QB
    nbl = nb - first_lat

    def win(blk, i):
        return pl.BlockSpec(
            (HP, QB, LANES),
            lambda b: (blk, first_lat + jnp.clip(jnp.maximum(b - first_lat, 0) - 2, 0, nbl - NA_KBLK) + i, 0))

    def kv_specs(blk):
        return [win(blk, i) for i in range(NA_KBLK)] + [pl.BlockSpec((HP, CTX, LANES), lambda b: (blk, 0, 0))]

    return pl.pallas_call(
        functools.partial(_attn_b_body, nbl=nbl),
        grid=(nb,),
        in_specs=[_resident((HEADS, NA_TAB, GRID_W, LANES), lambda b: (0, 0, 0, 0)),
                  pl.BlockSpec((HP, QB, LANES), lambda b: (0, b, 0))] + kv_specs(1) + kv_specs(2),
        out_specs=pl.BlockSpec((HP, QB, LANES), lambda b: (0, b, 0)),
        out_shape=jax.ShapeDtypeStruct((HP, n, LANES), BF16),
        compiler_params=_cparams(52),
        name="attn_neighbourhood",
    )(table, qkv3, *([qkv3] * 12))


def _oproj_body(*refs, mode):
    if mode == "attn":
        o_ref, w_ref, h_ref, mod_ref, out_ref = refs
        z = jnp.concatenate([o_ref[c] for c in range(HP)], axis=1)
    else:
        ug_ref, hs_ref, w_ref, h_ref, mod_ref, out_ref = refs
        z = (jax.nn.gelu(ug_ref[...]) * hs_ref[...]).astype(BF16)
    y = jnp.dot(z, w_ref[...], preferred_element_type=F32)
    out_ref[...] = h_ref[...] + mod_ref[0, 2:3, :] * y


def _oproj_call(mix_in, w, h, mod, *, mode, name):
    n = h.shape[0]
    kdim = w.shape[0]
    if mode == "attn":
        in_specs = [pl.BlockSpec((HP, TM, LANES), lambda i: (0, i, 0))]
        args = [mix_in]
    else:
        u, hs = mix_in
        in_specs = [pl.BlockSpec((TM, D_RNN), lambda i: (i, 0)), pl.BlockSpec((TM, D_RNN), lambda i: (i, 0))]
        args = [u, hs]
    in_specs += [_resident((kdim, D), lambda i: (0, 0)), pl.BlockSpec((TM, D), lambda i: (i, 0)), _mod_spec()]
    return pl.pallas_call(
        functools.partial(_oproj_body, mode=mode),
        grid=(n // TM,),
        in_specs=in_specs,
        out_specs=pl.BlockSpec((TM, D), lambda i: (i, 0)),
        out_shape=jax.ShapeDtypeStruct((n, D), F32),
        compiler_params=_cparams(48),
        name=name,
    )(*args, w, h, mod[:, 0:3])


def _shift_rows(x, s, fill, reverse):
    t = x.shape[0]
    if s % SUBLANES == 0:
        pad = jnp.full((s, x.shape[1]), fill, x.dtype)
        return jnp.concatenate([x[s:], pad] if reverse else [pad, x[:t - s]], axis=0)
    row = lax.broadcasted_iota(jnp.int32, x.shape, 0)
    if reverse:
        return jnp.where(row >= t - s, fill, pltpu.roll(x, t - s, 0))
    return jnp.where(row < s, fill, pltpu.roll(x, s, 0))


def _chunk_scan(a, b, reverse):
    s = 1
    while s < a.shape[0]:
        a_s = _shift_rows(a, s, 1.0, reverse)
        b_s = _shift_rows(b, s, 0.0, reverse)
        b = a * b_s + b
        a = a * a_s
        s *= 2
    return a, b


def _lru_body(x_ref, cw_ref, cb_ref, wa_ref, wx_ref, ba_ref, bx_ref, lam_ref, o_ref, *, nchunks):
    n = nchunks * SCAN_TC
    halo = SUBLANES
    cw = cw_ref[...]
    cb = cb_ref[...]
    sp = jax.nn.softplus(-lam_ref[...])

    def conv_chunk(ci):
        r0 = pl.multiple_of(ci * SCAN_TC, SCAN_TC)
        lat_first = CTX // SCAN_TC
        prev_ok = jnp.logical_and(ci != 0, ci != lat_first).astype(F32)
        next_ok = jnp.logical_and(ci != lat_first - 1, ci != nchunks - 1).astype(F32)
        p0 = pl.multiple_of(jnp.maximum(r0 - halo, 0), halo)
        n0 = pl.multiple_of(jnp.minimum(r0 + SCAN_TC, n - halo), halo)
        xe = jnp.concatenate([x_ref[pl.ds(p0, halo), :] * prev_ok,
                              x_ref[pl.ds(r0, SCAN_TC), :],
                              x_ref[pl.ds(n0, halo), :] * next_ok], axis=0)
        y = cb
        for j in range(CONV_W):
            off = halo + j - CONV_LEFT
            y = y + xe[off:off + SCAN_TC, :] * cw[j:j + 1, :]
        return r0, y

    def direction(d, ci, h_in):
        r0, xc = conv_chunk(ci)
        xb = xc.astype(BF16)
        r = jax.nn.sigmoid(jnp.dot(xb, wa_ref[d, 0], preferred_element_type=F32) + ba_ref[d:d + 1, :])
        ig = jax.nn.sigmoid(jnp.dot(xb, wx_ref[d, 0], preferred_element_type=F32) + bx_ref[d:d + 1, :])
        log_a = -LRU_C * r * sp[d:d + 1, :]
        a = jnp.exp(log_a)
        bb = jnp.sqrt(1.0 - a * a) * (ig * xc)
        a_cum, h = _chunk_scan(a, bb, reverse=(d == 1))
        h = h + a_cum * h_in
        h_out = h[SCAN_TC - 1:SCAN_TC, :] if d == 0 else h[0:1, :]
        return r0, h, h_out

    def fwd(ci, h_in):
        r0, h, h_out = direction(0, ci, h_in)
        o_ref[pl.ds(r0, SCAN_TC), :] = h
        return h_out

    def bwd(ci, h_in):
        r0, h, h_out = direction(1, ci, h_in)
        o_ref[pl.ds(r0, SCAN_TC), :] += h
        return h_out

    h0 = jnp.zeros((1, RNN_BW), F32)
    lax.fori_loop(0, nchunks, fwd, h0)
    nctx = CTX // SCAN_TC
    hc = lax.fori_loop(0, nctx, lambda k, hh: bwd(nctx - 1 - k, hh), h0)
    lax.fori_loop(0, nchunks - nctx, lambda k, hh: bwd(nchunks - 1 - k, hh), hc)


def _lru_call(u, conv_w, conv_b, wa, wx, ba, bx, lam):
    n = u.shape[0]
    return pl.pallas_call(
        functools.partial(_lru_body, nchunks=n // SCAN_TC),
        grid=(RNN_BLOCKS,),
        in_specs=[_resident((n, RNN_BW), lambda m: (0, RNN_BLOCKS + m)),
                  pl.BlockSpec((CONV_W, RNN_BW), lambda m: (0, m)),
                  pl.BlockSpec((1, RNN_BW), lambda m: (0, m)),
                  pl.BlockSpec((2, 1, RNN_BW, RNN_BW), lambda m: (0, m, 0, 0)),
                  pl.BlockSpec((2, 1, RNN_BW, RNN_BW), lambda m: (0, m, 0, 0)),
                  pl.BlockSpec((2, RNN_BW), lambda m: (0, m)),
                  pl.BlockSpec((2, RNN_BW), lambda m: (0, m)),
                  pl.BlockSpec((2, RNN_BW), lambda m: (0, m))],
        out_specs=_resident((n, RNN_BW), lambda m: (0, m)),
        out_shape=jax.ShapeDtypeStruct((n, D_RNN), F32),
        compiler_params=_cparams(52),
        name="rglru_scan",
    )(u, conv_w, conv_b.reshape(1, D_RNN), wa, wx, ba, bx, lam)


def _pack_rows(ref, x):
    rows = x.shape[0]
    lo = lax.bitcast_convert_type(x[:, :D // 2].astype(BF16).astype(F32), jnp.uint32)
    hi = lax.bitcast_convert_type(x[:, D // 2:].astype(BF16).astype(F32), jnp.uint32)
    w = hi | (lo >> 16)
    for s in range(XROW_TILE):
        ref[pl.ds(s, rows, stride=XROW_TILE), :] = w[:, s * LANES:(s + 1) * LANES]


def _unpack_rows(ref, rows, dtype):
    w = jnp.concatenate([ref[pl.ds(s, rows, stride=XROW_TILE), :] for s in range(XROW_TILE)], axis=1)
    lo = lax.bitcast_convert_type(w << 16, F32)
    hi = lax.bitcast_convert_type(w & jnp.uint32(0xFFFF0000), F32)
    return jnp.concatenate([lo, hi], axis=1).astype(dtype)


def _router_body(h_ref, g_ref, mod_ref, wrh_ref, wrl_ref, br_ref, f_ref, r_ref, cnt_ref, base_ref):
    @pl.when(pl.program_id(0) == 0)
    def _():
        base_ref[...] = jnp.zeros(base_ref.shape, F32)

    f = _norm_mod(h_ref[...], g_ref[...], mod_ref[0, 0:1, :], mod_ref[0, 1:2, :])
    _pack_rows(f_ref, f)
    f_hi = f.astype(BF16)
    f_lo = (f - f_hi.astype(F32)).astype(BF16)
    logits = (jnp.dot(jnp.concatenate([f_hi, f_lo], axis=0), wrh_ref[...], preferred_element_type=F32).reshape(2, TM, LANES).sum(axis=0)
              + jnp.dot(f_hi, wrl_ref[...], preferred_element_type=F32) + br_ref[...])
    lane = lax.broadcasted_iota(jnp.int32, logits.shape, 1)
    ninf = -jnp.inf
    is_g = lane < N_GROUPS
    gl = jnp.where(is_g, logits, ninf)
    gmax = jnp.max(gl, axis=-1, keepdims=True)
    gsel = jnp.min(jnp.where(gl == gmax, lane, LANES), axis=-1, keepdims=True)
    gsum = jnp.sum(jnp.where(is_g, jnp.exp(gl - gmax), 0.0), axis=-1, keepdims=True)
    g_w = 1.0 / gsum
    e_lo = N_GROUPS + EXP_PER_GROUP * gsel
    in_grp = jnp.logical_and(lane >= e_lo, lane < e_lo + EXP_PER_GROUP)
    el = jnp.where(in_grp, logits, ninf)
    m1 = jnp.max(el, axis=-1, keepdims=True)
    i1 = jnp.min(jnp.where(el == m1, lane, LANES), axis=-1, keepdims=True)
    el2 = jnp.where(lane == i1, ninf, el)
    m2 = jnp.max(el2, axis=-1, keepdims=True)
    i2 = jnp.min(jnp.where(el2 == m2, lane, LANES), axis=-1, keepdims=True)
    t = jnp.exp(m2 - m1)
    w1 = g_w / (1.0 + t)
    w2 = g_w * t / (1.0 + t)
    oh1 = lane == i1
    oh2 = lane == i2
    rr = lax.broadcasted_iota(jnp.int32, (TM, TM), 0)
    cc = lax.broadcasted_iota(jnp.int32, (TM, TM), 1)
    tri = (cc < rr).astype(BF16)
    pre1 = jnp.dot(tri, oh1.astype(BF16), preferred_element_type=F32)
    pre2 = jnp.dot(tri, oh2.astype(BF16), preferred_element_type=F32)
    base = base_ref[...]
    cnt1 = jnp.sum(oh1.astype(F32), axis=0, keepdims=True)
    cnt2 = jnp.sum(oh2.astype(F32), axis=0, keepdims=True)
    rank1 = jnp.sum(jnp.where(oh1, pre1 + base, 0.0), axis=-1, keepdims=True)
    rank2 = jnp.sum(jnp.where(oh2, pre2 + (base + cnt1), 0.0), axis=-1, keepdims=True)
    total = base + cnt1 + cnt2
    base_ref[...] = total
    cnt_ref[...] = total
    cols = [(i1 - N_GROUPS).astype(F32), (i2 - N_GROUPS).astype(F32), w1, w2, rank1, rank2]
    out = jnp.zeros(logits.shape, F32)
    for k, v in enumerate(cols):
        out = jnp.where(lane == k, v, out)
    r_ref[...] = out


def _router_call(h, g, mod, wr, br):
    n = h.shape[0]
    wr_hi = wr.astype(BF16)
    wr_lo = (wr - wr_hi.astype(F32)).astype(BF16)
    return pl.pallas_call(
        _router_body,
        grid=(n // TM,),
        in_specs=[pl.BlockSpec((TM, D), lambda i: (i, 0)),
                  pl.BlockSpec((1, D), lambda i: (0, 0)),
                  _mod_spec(),
                  pl.BlockSpec((D, LANES), lambda i: (0, 0)),
                  pl.BlockSpec((D, LANES), lambda i: (0, 0)),
                  pl.BlockSpec((1, LANES), lambda i: (0, 0))],
        out_specs=[pl.BlockSpec((TM * XROW_TILE, LANES), lambda i: (i, 0)),
                   pl.BlockSpec((TM, LANES), lambda i: (i, 0)),
                   pl.BlockSpec((1, LANES), lambda i: (0, 0))],
        out_shape=[jax.ShapeDtypeStruct((n * XROW_TILE, LANES), jnp.uint32),
                   jax.ShapeDtypeStruct((n, LANES), F32),
                   jax.ShapeDtypeStruct((1, LANES), F32)],
        scratch_shapes=[pltpu.VMEM((1, LANES), F32)],
        compiler_params=_cparams(32),
        name="moe_router",
    )(h, g.reshape(1, D), mod[:, 3:6], wr_hi, wr_lo, br)


def _dispatch_body(dest_ref, pad_lo_ref, pad_n_ref, f_ref, xg_ref, z_ref, sem):
    i = pl.program_id(0)
    base = i * TM

    def issue(j, carry):
        for k in range(TOP_K):
            d = dest_ref[(base + j) * TOP_K + k]
            pltpu.make_async_copy(f_ref.at[pl.ds(j * XROW_TILE, XROW_TILE)],
                                  xg_ref.at[pl.ds(d * XROW_TILE, XROW_TILE)], sem).start(priority=k)
        return carry

    def wait_one(carry):
        pltpu.make_async_copy(z_ref, xg_ref.at[pl.ds(0, XROW_TILE)], sem).wait()
        return carry

    lax.fori_loop(0, TM, issue, 0, unroll=DMA_UNROLL)

    @pl.when(i == 0)
    def _():
        z_ref[...] = jnp.zeros(z_ref.shape, jnp.uint32)

        def fill(e, carry):
            lo = pad_lo_ref[e]
            n_fill = pad_n_ref[e]

            def batch(bi, c):
                s0 = bi * MOE_MB
                cnt = jnp.minimum(n_fill - s0, MOE_MB)

                def one(s, c2):
                    pltpu.make_async_copy(z_ref, xg_ref.at[pl.ds((lo + s0 + s) * XROW_TILE, XROW_TILE)], sem).start()
                    return c2

                lax.fori_loop(0, cnt, one, 0)
                lax.fori_loop(0, cnt, lambda s, c2: wait_one(c2), 0)
                return c

            lax.fori_loop(0, (n_fill + (MOE_MB - 1)) // MOE_MB, batch, 0)
            return carry

        lax.fori_loop(0, N_EXPERTS + 1, fill, 0)

    lax.fori_loop(0, TM * TOP_K, lambda j, c: wait_one(c), 0, unroll=DMA_UNROLL)


def _dispatch_call(dest, pad_lo, pad_n, f2, p):
    n = f2.shape[0] // XROW_TILE
    return pl.pallas_call(
        _dispatch_body,
        grid_spec=pltpu.PrefetchScalarGridSpec(
            num_scalar_prefetch=3,
            grid=(n // TM,),
            in_specs=[pl.BlockSpec((TM * XROW_TILE, LANES), lambda i, d, lo, nn: (i, 0))],
            out_specs=pl.BlockSpec(memory_space=pl.ANY),
            scratch_shapes=[pltpu.VMEM((XROW_TILE, LANES), jnp.uint32), pltpu.SemaphoreType.DMA(())]),
        out_shape=jax.ShapeDtypeStruct((p * XROW_TILE, LANES), jnp.uint32),
        compiler_params=_cparams(16),
        name="moe_dispatch",
    )(dest, pad_lo, pad_n, f2)


def _expert_weights(b, be_ref, ne_ref, nu_ref, w_hbm, wf_ref, wb_ref, sem, slot_ref, layer):
    e = be_ref[b]
    first = jnp.logical_or(b == 0, e != be_ref[jnp.maximum(b - 1, 0)])

    def fetch(expert, slot):
        return pltpu.make_async_copy(w_hbm.at[layer, expert], wf_ref.at[slot], sem.at[slot])

    @pl.when(b == 0)
    def _():
        slot_ref[0] = 0
        fetch(e, 0).start()

    @pl.when(jnp.logical_and(first, b < nu_ref[0]))
    def _():
        s = slot_ref[0]
        fetch(e, s).wait()
        wb_ref[...] = wf_ref[s].astype(BF16)
        ne = ne_ref[b]

        @pl.when(ne != e)
        def _():
            fetch(ne, 1 - s).start()

        slot_ref[0] = 1 - s


def _ffn_up_body(be_ref, ne_ref, nu_ref, x_ref, w_hbm, a_ref, wf_ref, wb_ref, sem, slot_ref, *, layer):
    b = pl.program_id(0)
    _expert_weights(b, be_ref, ne_ref, nu_ref, w_hbm, wf_ref, wb_ref, sem, slot_ref, layer)

    @pl.when(b < nu_ref[0])
    def _():
        hgu = jnp.dot(_unpack_rows(x_ref, MOE_MB, BF16), wb_ref[...], preferred_element_type=F32)
        g = hgu[:, :D_EXPERT]
        u = hgu[:, D_EXPERT:]
        a_ref[...] = (g * jax.nn.sigmoid(g) * u).astype(BF16)

    @pl.when(b >= nu_ref[0])
    def _():
        a_ref[...] = jnp.zeros(a_ref.shape, BF16)


def _expert_scratch(rows, cols):
    return [pltpu.VMEM((2, rows, cols), F32), pltpu.VMEM((rows, cols), BF16),
            pltpu.SemaphoreType.DMA((2,)), pltpu.SMEM((1,), jnp.int32)]


def _ffn_up_call(block_e, next_e, n_used, xg, w_gu, layer):
    p = xg.shape[0] // XROW_TILE
    return pl.pallas_call(
        functools.partial(_ffn_up_body, layer=layer),
        grid_spec=pltpu.PrefetchScalarGridSpec(
            num_scalar_prefetch=3,
            grid=(p // MOE_MB,),
            in_specs=[pl.BlockSpec((MOE_MB * XROW_TILE, LANES),
                                   lambda b, be, ne, nu: (jnp.minimum(b, nu[0] - 1), 0)),
                      pl.BlockSpec(memory_space=pl.ANY)],
            out_specs=pl.BlockSpec((MOE_MB, D_EXPERT), lambda b, be, ne, nu: (b, 0)),
            scratch_shapes=_expert_scratch(D, 2 * D_EXPERT)),
        out_shape=jax.ShapeDtypeStruct((p, D_EXPERT), BF16),
        compiler_params=_cparams(48),
        name="moe_ffn_up",
    )(block_e, next_e, n_used, xg, w_gu)


def _ffn_down_body(be_ref, ne_ref, nu_ref, a_ref, w_hbm, y_ref, wf_ref, wb_ref, sem, slot_ref, *, layer):
    b = pl.program_id(0)
    _expert_weights(b, be_ref, ne_ref, nu_ref, w_hbm, wf_ref, wb_ref, sem, slot_ref, layer)

    @pl.when(b < nu_ref[0])
    def _():
        _pack_rows(y_ref, jnp.dot(a_ref[...], wb_ref[...], preferred_element_type=F32))

    @pl.when(b >= nu_ref[0])
    def _():
        y_ref[...] = jnp.zeros(y_ref.shape, jnp.uint32)


def _ffn_down_call(block_e, next_e, n_used, act, w_down, layer):
    p = act.shape[0]
    return pl.pallas_call(
        functools.partial(_ffn_down_body, layer=layer),
        grid_spec=pltpu.PrefetchScalarGridSpec(
            num_scalar_prefetch=3,
            grid=(p // MOE_MB,),
            in_specs=[pl.BlockSpec((MOE_MB, D_EXPERT), lambda b, be, ne, nu: (b, 0)),
                      pl.BlockSpec(memory_space=pl.ANY)],
            out_specs=pl.BlockSpec((MOE_MB * XROW_TILE, LANES), lambda b, be, ne, nu: (b, 0)),
            scratch_shapes=_expert_scratch(D_EXPERT, D)),
        out_shape=jax.ShapeDtypeStruct((p * XROW_TILE, LANES), jnp.uint32),
        compiler_params=_cparams(32),
        name="moe_ffn_down",
    )(block_e, next_e, n_used, act, w_down)


def _combine_body(dest_ref, h_ref, r_ref, mod_ref, g_ref, yb_ref, o_ref, buf_ref, sem, *, final):
    i = pl.program_id(0)
    slot = i % 2

    def gather(tile, dst_slot):
        base = tile * TM

        def issue(j, carry):
            for k in range(TOP_K):
                d = dest_ref[(base + j) * TOP_K + k]
                pltpu.make_async_copy(yb_ref.at[pl.ds(d * XROW_TILE, XROW_TILE)],
                                      buf_ref.at[dst_slot, k, pl.ds(j * XROW_TILE, XROW_TILE)],
                                      sem.at[dst_slot]).start(priority=k)
            return carry

        lax.fori_loop(0, TM, issue, 0, unroll=DMA_UNROLL)

    @pl.when(i == 0)
    def _():
        gather(0, 0)

    @pl.when(i + 1 < pl.num_programs(0))
    def _():
        gather(i + 1, 1 - slot)

    def drain(j, carry):
        pltpu.make_async_copy(yb_ref.at[pl.ds(0, XROW_TILE)], buf_ref.at[slot, 0, pl.ds(0, XROW_TILE)],
                              sem.at[slot]).wait()
        return carry

    lax.fori_loop(0, TM * TOP_K, drain, 0, unroll=DMA_UNROLL)
    r = r_ref[...]
    y = sum(r[:, TOP_K + k:TOP_K + k + 1] * _unpack_rows(buf_ref.at[slot, k], TM, F32) for k in range(TOP_K))
    h_new = h_ref[...] + mod_ref[0, 2:3, :] * y
    if final:
        ms = jnp.mean(h_new * h_new, axis=-1, keepdims=True)
        h_new = h_new * lax.rsqrt(ms + EPS) * g_ref[...]
    o_ref[...] = h_new


def _combine_call(dest, h, route, mod, yb, final_g=None):
    n = h.shape[0]
    final = final_g is not None
    first = CTX // TM
    out_rows = n - CTX if final else n
    out_map = (lambda i, d: (jnp.maximum(i - first, 0), 0)) if final else (lambda i, d: (i, 0))
    g = final_g if final else jnp.ones((D,), F32)
    return pl.pallas_call(
        functools.partial(_combine_body, final=final),
        grid_spec=pltpu.PrefetchScalarGridSpec(
            num_scalar_prefetch=1,
            grid=(n // TM,),
            in_specs=[pl.BlockSpec((TM, D), lambda i, d: (i, 0)),
                      pl.BlockSpec((TM, LANES), lambda i, d: (i, 0)),
                      pl.BlockSpec((1, 3, D), lambda i, d: (jnp.minimum(i, 1), 0, 0)),
                      pl.BlockSpec((1, D), lambda i, d: (0, 0)),
                      pl.BlockSpec(memory_space=pl.ANY)],
            out_specs=pl.BlockSpec((TM, D), out_map),
            scratch_shapes=[pltpu.VMEM((2, TOP_K, TM * XROW_TILE, LANES), jnp.uint32),
                            pltpu.SemaphoreType.DMA((2,))]),
        out_shape=jax.ShapeDtypeStruct((out_rows, D), F32),
        compiler_params=_cparams(32),
        name="moe_combine",
    )(dest, h, route, mod[:, 3:6], g.reshape(1, D), yb)


def _dispatch_plan(route, cnt, n):
    experts = jnp.arange(N_EXPERTS, dtype=jnp.int32)
    counts = cnt[0, N_GROUPS:N_GROUPS + N_EXPERTS].astype(jnp.int32)
    padded = (counts + MOE_MB - 1) // MOE_MB * MOE_MB
    end_pad = jnp.cumsum(padded)
    start_pad = end_pad - padded
    n_blocks = -(-(n * TOP_K + N_EXPERTS * (MOE_MB - 1)) // MOE_MB)
    p = n_blocks * MOE_MB
    blk_start = jnp.arange(n_blocks, dtype=jnp.int32) * MOE_MB
    block_e = jnp.minimum(jnp.sum(end_pad[None, :] <= blk_start[:, None], axis=1), N_EXPERTS - 1).astype(jnp.int32)
    n_used = (end_pad[-1:] // MOE_MB).astype(jnp.int32)
    group_end = end_pad[block_e] // MOE_MB
    next_e = jnp.where(group_end < n_used[0], block_e[jnp.minimum(group_end, n_blocks - 1)], block_e)
    e_idx = route[:, 0:TOP_K].astype(jnp.int32)
    rank = route[:, 2 * TOP_K:3 * TOP_K].astype(jnp.int32)
    start = jnp.sum(jnp.where(e_idx[..., None] == experts, start_pad, 0), axis=-1)
    dest = (start + rank).reshape(n * TOP_K)
    pad_lo = jnp.concatenate([start_pad + counts, end_pad[-1:]]).astype(jnp.int32)
    pad_n = jnp.concatenate([padded - counts, p - end_pad[-1:]]).astype(jnp.int32)
    return p, block_e, next_e, n_used, dest, pad_lo, pad_n


def _moe_layer(h, g, mod, wr, br, w_gu, w_down, layer, final_g=None):
    n = h.shape[0]
    f2, route, cnt = _router_call(h, g, mod, wr, br)
    p, block_e, next_e, n_used, dest, pad_lo, pad_n = _dispatch_plan(route, cnt, n)
    xg = _dispatch_call(dest, pad_lo, pad_n, f2, p)
    act = _ffn_up_call(block_e, next_e, n_used, xg, w_gu, layer)
    yb = _ffn_down_call(block_e, next_e, n_used, act, w_down, layer)
    return _combine_call(dest, h, route, mod, yb, final_g)


def _rope_tables(l):
    quarter = HEAD_DIM // 4
    inv = ROPE_BASE ** (-jnp.arange(quarter, dtype=F32) / quarter)
    n_rows = l // GRID_W
    ang_r = jnp.arange(n_rows, dtype=F32)[:, None] * inv
    ang_c = jnp.arange(GRID_W, dtype=F32)[:, None] * inv
    by_row = lambda t: jnp.broadcast_to(t[:, None, :], (n_rows, GRID_W, quarter)).reshape(l, quarter)
    by_col = lambda t: jnp.broadcast_to(t[None, :, :], (n_rows, GRID_W, quarter)).reshape(l, quarter)
    cr, sr, cc, sc = by_row(jnp.cos(ang_r)), by_row(jnp.sin(ang_r)), by_col(jnp.cos(ang_c)), by_col(jnp.sin(ang_c))
    cos = jnp.concatenate([cr, cr, cc, cc], axis=1)
    sin = jnp.concatenate([-sr, sr, -sc, sc], axis=1)
    cos = jnp.concatenate([jnp.ones((CTX, HEAD_DIM), F32), cos], axis=0)
    sin = jnp.concatenate([jnp.zeros((CTX, HEAD_DIM), F32), sin], axis=0)
    return jnp.tile(cos, (1, 2)), jnp.tile(sin, (1, 2))


def _attn_a_weights(w_qkv):
    nq = HEADS * HEAD_DIM
    nkv = A_KV_HEADS * HEAD_DIM
    wq = w_qkv[:, :nq]
    wk = w_qkv[:, nq:nq + nkv].reshape(D, A_KV_HEADS, 1, HEAD_DIM)
    wv = w_qkv[:, nq + nkv:].reshape(D, A_KV_HEADS, 1, HEAD_DIM)
    dup = lambda w: jnp.broadcast_to(w, (D, A_KV_HEADS, 2, HEAD_DIM)).reshape(D, 2 * nkv)
    return jnp.concatenate([wq, dup(wk), dup(wv)], axis=1).astype(BF16)


def kernel(x, c, ctx, c_ctx, ada_w, ada_b, norm_mix_g, norm_ffn_g, router_group_w, router_group_b,
           router_expert_w, router_expert_b, moe_w_gu, moe_w_down, attn_w_qkv, attn_w_o, attn_sink,
           na_w_qkv, na_w_o, na_rpb, rnn_w_in, rnn_conv_w, rnn_conv_b, rnn_wa, rnn_ba, rnn_wx, rnn_bx,
           rnn_lam, rnn_w_out, final_norm_g):
    batch, l, _ = x.shape
    assert batch == 1 and ctx.shape[1] == CTX and l % (QB * 2) == 0 and l // QB >= NA_KBLK
    h = jnp.concatenate([ctx[0], x[0]], axis=0)
    c2 = jnp.stack([c_ctx, c[0]], axis=1)
    mods = _ada_call(c2, ada_w, ada_b).reshape(DEPTH, 2, 6, D)
    rope = _rope_tables(l)
    pad_r = LANES - N_GROUPS - N_EXPERTS
    for i in range(DEPTH):
        kind, j = i % 3, i // 3
        mod = mods[i]
        if kind == 0:
            w = _attn_a_weights(attn_w_qkv[j])
            nq = HEADS * HEAD_DIM
            qkv3 = _proj_call(h, norm_mix_g[i], mod, w, mode="rope", rope=rope,
                              n_rope=nq + 2 * A_KV_HEADS * HEAD_DIM, n_q=nq, name="proj_window")
            o3 = _attn_a_call(qkv3, attn_sink[j])
            h = _oproj_call(o3, attn_w_o[j].astype(BF16), h, mod, mode="attn", name="oproj_window")
        elif kind == 1:
            qkv3 = _proj_call(h, norm_mix_g[i], mod, na_w_qkv[j].astype(BF16), mode="cols",
                              n_q=HEADS * HEAD_DIM, name="proj_neighbourhood")
            o3 = _attn_b_call(qkv3, _na_table_call(na_rpb[j]))
            h = _oproj_call(o3, na_w_o[j].astype(BF16), h, mod, mode="attn", name="oproj_neighbourhood")
        else:
            u = _proj_call(h, norm_mix_g[i], mod, rnn_w_in[j].astype(BF16), mode="plain", name="proj_rglru")
            hs = _lru_call(u, rnn_conv_w[j], rnn_conv_b[j], rnn_wa[j].astype(BF16), rnn_wx[j].astype(BF16),
                           rnn_ba[j], rnn_bx[j], rnn_lam[j])
            h = _oproj_call((u, hs), rnn_w_out[j].astype(BF16), h, mod, mode="rnn", name="oproj_rglru")
        wr = jnp.concatenate([router_group_w[i], router_expert_w[i], jnp.zeros((D, pad_r), F32)], axis=1)
        br = jnp.concatenate([router_group_b[i], router_expert_b[i], jnp.zeros((pad_r,), F32)]).reshape(1, LANES)
        h = _moe_layer(h, norm_ffn_g[i], mod, wr, br, moe_w_gu, moe_w_down, i,
                       final_norm_g if i == DEPTH - 1 else None)
    return h[None]
```

```python
import functools

import jax
import jax.numpy as jnp
from jax import lax
from jax.experimental import pallas as pl
from jax.experimental.pallas import tpu as pltpu

F32 = jnp.float32
BF16 = jnp.bfloat16

D = 2048
DEPTH = 4
GRID_W = 64
CTX = 256
HEADS = 32
HEAD_DIM = 64
A_KV_HEADS = 4
A_GROUP = HEADS // A_KV_HEADS
WINDOW = 128
NB_KH = 8
NB_KW = 16
D_RNN = 2560
RNN_BLOCKS = 10
RNN_BW = D_RNN // RNN_BLOCKS
CONV_W = 4
CONV_LEFT = 2
LRU_C = 8.0
N_GROUPS = 4
EXP_PER_GROUP = 8
N_EXPERTS = N_GROUPS * EXP_PER_GROUP
TOP_K = 2
D_EXPERT = 768
ROPE_BASE = 10000.0
EPS = 1e-6
NEG = -1e30

LANES = 128
SUBLANES = 8
MIB = 1024 * 1024

TM = 256
QB = 128
HP = HEADS // 2
MOE_MB = 256
ROW_TILE = D // LANES
XROW_TILE = ROW_TILE // 2
DMA_UNROLL = 8
SCAN_TC = 256
NA_KBLK = 5
SQRT_SCALE = HEAD_DIM ** -0.5
LOG2E = 1.4426950408889634


def _cparams(vmem_mib, sem=("arbitrary",)):
    return pltpu.CompilerParams(dimension_semantics=sem, vmem_limit_bytes=int(vmem_mib * MIB))


def _resident(block_shape, index_map):
    return pl.BlockSpec(block_shape, index_map, pipeline_mode=pl.Buffered(1))


def _mod_spec():
    return pl.BlockSpec((1, 3, D), lambda i: (jnp.minimum(i, 1), 0, 0))


def _norm_mod(x, g, shift, scale):
    ms = jnp.mean(x * x, axis=-1, keepdims=True)
    y = x * lax.rsqrt(ms + EPS) * g
    return y * (1.0 + scale) + shift


ADA_TN = 1024


def _ada_body(c_ref, w_ref, b_ref, o_ref):
    c = c_ref[...]
    cs = c * jax.nn.sigmoid(c)
    for r in range(2):
        cb = jnp.broadcast_to(cs[:, r:r + 1], (D, LANES))
        outs = []
        for j in range(ADA_TN // LANES):
            w = w_ref[0, :, j * LANES:(j + 1) * LANES]
            p = (w * cb).reshape(D // SUBLANES, SUBLANES, LANES).sum(axis=0)
            outs.append(p.sum(axis=0, keepdims=True))
        o_ref[0, r:r + 1, :] = jnp.concatenate(outs, axis=1) + b_ref[0]


def _ada_call(c2, ada_w, ada_b):
    return pl.pallas_call(
        _ada_body,
        grid=(DEPTH, 6 * D // ADA_TN),
        in_specs=[pl.BlockSpec((D, 2), lambda l, j: (0, 0)),
                  pl.BlockSpec((1, D, ADA_TN), lambda l, j: (l, 0, j)),
                  pl.BlockSpec((1, 1, ADA_TN), lambda l, j: (l, 0, j))],
        out_specs=pl.BlockSpec((1, 2, ADA_TN), lambda l, j: (l, 0, j)),
        out_shape=jax.ShapeDtypeStruct((DEPTH, 2, 6 * D), F32),
        compiler_params=_cparams(32, ("arbitrary", "arbitrary")),
        name="ada_mod",
    )(c2, ada_w, ada_b.reshape(DEPTH, 1, 6 * D))


PROJ_CH = 512


def _rope_piece(piece, cos, sin):
    lane = lax.broadcasted_iota(jnp.int32, piece.shape, 1)
    first = (lane & 16) == 0
    partner = jnp.where(first, pltpu.roll(piece, LANES - 16, 1), pltpu.roll(piece, 16, 1))
    return piece * cos + partner * sin


def _proj_body(*refs, mode, n_rope, n_q):
    if mode == "rope":
        x_ref, g_ref, mod_ref, w_ref, cos_ref, sin_ref, o_ref = refs
    else:
        x_ref, g_ref, mod_ref, w_ref, o_ref = refs
    a = _norm_mod(x_ref[...], g_ref[...], mod_ref[0, 0:1, :], mod_ref[0, 1:2, :]).astype(BF16)
    nout = w_ref.shape[1]
    for c in range(nout // PROJ_CH):
        acc = jnp.dot(a, w_ref[:, c * PROJ_CH:(c + 1) * PROJ_CH], preferred_element_type=F32)
        if mode == "plain":
            o_ref[:, c * PROJ_CH:(c + 1) * PROJ_CH] = acc
            continue
        for k in range(PROJ_CH // LANES):
            col0 = c * PROJ_CH + k * LANES
            piece = acc[:, k * LANES:(k + 1) * LANES]
            if col0 < n_rope:
                piece = _rope_piece(piece, cos_ref[...], sin_ref[...])
            if col0 < n_q:
                piece = piece * (SQRT_SCALE * LOG2E)
            o_ref[col0 // LANES] = piece.astype(BF16)


def _proj_call(h, g, mod, w, *, mode, rope=None, n_rope=0, n_q=0, name):
    n = h.shape[0]
    nout = w.shape[1]
    in_specs = [pl.BlockSpec((TM, D), lambda i: (i, 0)),
                pl.BlockSpec((1, D), lambda i: (0, 0)),
                _mod_spec(),
                _resident((D, nout), lambda i: (0, 0))]
    args = [h, g.reshape(1, D), mod[:, 0:3], w]
    if mode == "rope":
        in_specs += [pl.BlockSpec((TM, LANES), lambda i: (i, 0))] * 2
        args += list(rope)
    if mode == "plain":
        out_spec = pl.BlockSpec((TM, nout), lambda i: (i, 0))
        out_shape = jax.ShapeDtypeStruct((n, nout), F32)
        out_bytes = TM * nout * 4
    else:
        out_spec = pl.BlockSpec((nout // LANES, TM, LANES), lambda i: (0, i, 0))
        out_shape = jax.ShapeDtypeStruct((nout // LANES, n, LANES), BF16)
        out_bytes = TM * nout * 2
    vmem = (D * nout * 2 + 2 * TM * D * 4 + 2 * out_bytes) / MIB + 12
    return pl.pallas_call(
        functools.partial(_proj_body, mode=mode, n_rope=n_rope, n_q=n_q),
        grid=(n // TM,),
        in_specs=in_specs,
        out_specs=out_spec,
        out_shape=out_shape,
        compiler_params=_cparams(vmem),
        name=name,
    )(*args)


def _lane_lo(shape):
    return lax.broadcasted_iota(jnp.int32, shape, len(shape) - 1) < HEAD_DIM


def _pipeline_pairs(scores, probs, output):
    s_next = scores(0)
    pending = None
    for hp in range(HP):
        s_cur = s_next
        if hp + 1 < HP:
            s_next = scores(hp + 1)
        cur = probs(hp, s_cur)
        if pending is not None:
            output(hp - 1, *pending)
        pending = cur
    output(HP - 1, *pending)


def _attn_a_body(sink_ref, q_ref, kp_ref, kc_ref, kn_ref, kx_ref, vp_ref, vc_ref, vn_ref, vx_ref, o_ref, *, nb):
    b = pl.program_id(0)
    first_lat = CTX // QB
    is_lat = b >= first_lat
    prev_ok = b >= first_lat + 1
    next_ok = jnp.logical_and(is_lat, b <= nb - 2)
    nloc = 3 * QB
    nkeys = nloc + CTX
    qi = lax.broadcasted_iota(jnp.int32, (QB, nkeys), 0)
    kj = lax.broadcasted_iota(jnp.int32, (QB, nkeys), 1)
    rel = qi + WINDOW - kj
    band = jnp.abs(rel) <= WINDOW
    seg_ok = jnp.where(kj < QB, prev_ok.astype(jnp.int32),
                       jnp.where(kj < 2 * QB, is_lat.astype(jnp.int32), next_ok.astype(jnp.int32)))
    ok = jnp.logical_or(kj >= nloc, jnp.logical_and(band, seg_ok > 0))
    mask_add = jnp.where(ok, 0.0, NEG)
    lo = _lane_lo((QB, LANES))
    zero = jnp.zeros((QB, LANES), BF16)
    def scores(hp):
        kvh = hp // (A_GROUP // 2)
        keys = jnp.concatenate([kp_ref[kvh], kc_ref[kvh], kn_ref[kvh], kx_ref[kvh]], axis=0)
        q2 = q_ref[hp]
        qs = jnp.concatenate([jnp.where(lo, q2, zero), jnp.where(lo, zero, q2)], axis=0)
        return lax.dot_general(qs, keys, (((1,), (1,)), ((), ())), preferred_element_type=F32)

    def probs(hp, s):
        sink = jnp.concatenate([jnp.full((1, QB, 1), sink_ref[2 * hp + hh] * LOG2E, F32) for hh in range(2)], axis=0)
        s3 = s.reshape(2, QB, nkeys) + mask_add[None]
        m = jnp.maximum(jnp.max(s3, axis=-1, keepdims=True), sink)
        e = jnp.exp2(s3 - m)
        denom = jnp.sum(e, axis=-1, keepdims=True) + jnp.exp2(sink - m)
        return e.astype(BF16).reshape(2 * QB, nkeys), denom

    def output(hp, e, denom):
        kvh = hp // (A_GROUP // 2)
        vals = jnp.concatenate([vp_ref[kvh], vc_ref[kvh], vn_ref[kvh], vx_ref[kvh]], axis=0)
        o = jnp.dot(e, vals, preferred_element_type=F32).reshape(2, QB, LANES) / denom
        o_ref[hp] = jnp.where(lo, o[0], o[1]).astype(BF16)

    _pipeline_pairs(scores, probs, output)


def _attn_a_call(qkv3, sink):
    n = qkv3.shape[1]
    nb = n // QB
    first_lat = CTX // QB
    kblk, vblk = HP // A_KV_HEADS, HP // A_KV_HEADS + 1

    def kv_specs(blk):
        return [pl.BlockSpec((A_KV_HEADS, QB, LANES), lambda b: (blk, jnp.maximum(b - 1, first_lat), 0)),
                pl.BlockSpec((A_KV_HEADS, QB, LANES), lambda b: (blk, b, 0)),
                pl.BlockSpec((A_KV_HEADS, QB, LANES), lambda b: (blk, jnp.minimum(b + 1, nb - 1), 0)),
                pl.BlockSpec((A_KV_HEADS, CTX, LANES), lambda b: (blk, 0, 0))]

    return pl.pallas_call(
        functools.partial(_attn_a_body, nb=nb),
        grid=(nb,),
        in_specs=[pl.BlockSpec(memory_space=pltpu.SMEM),
                  pl.BlockSpec((HP, QB, LANES), lambda b: (0, b, 0))] + kv_specs(kblk) + kv_specs(vblk),
        out_specs=pl.BlockSpec((HP, QB, LANES), lambda b: (0, b, 0)),
        out_shape=jax.ShapeDtypeStruct((HP, n, LANES), BF16),
        compiler_params=_cparams(40),
        name="attn_window",
    )(sink, qkv3, *([qkv3] * 8))


NA_TAB = 2 * NB_KH
RPB_W = 2 * NB_KW - 1
RPB_H = 2 * NB_KH - 1


def _na_table_body(rpb_ref, o_ref):
    h = pl.program_id(0)
    shape = (GRID_W, LANES)
    c = lax.broadcasted_iota(jnp.int32, shape, 0)
    lane = lax.broadcasted_iota(jnp.int32, shape, 1)
    kc = lane & (GRID_W - 1)
    hi = lane >= GRID_W
    cs = jnp.clip(c - NB_KW // 2, 0, GRID_W - NB_KW)
    colok = jnp.logical_and(kc >= cs, kc < cs + NB_KW)
    diff = kc - c + (NB_KW - 1)
    neg = jnp.full(shape, NEG, F32)
    rows = [neg]
    for d in range(RPB_H):
        acc = neg
        for j in range(RPB_W):
            acc = jnp.where(diff == j, rpb_ref[h * (RPB_H * RPB_W) + d * RPB_W + j] * LOG2E, acc)
        rows.append(jnp.where(colok, acc, NEG))
    rows.append(neg)
    for t in range(NA_TAB):
        o_ref[0, t] = jnp.where(hi, rows[t + 1], rows[t])


def _na_table_call(rpb):
    return pl.pallas_call(
        _na_table_body,
        grid=(HEADS,),
        in_specs=[pl.BlockSpec(memory_space=pltpu.SMEM)],
        out_specs=pl.BlockSpec((1, NA_TAB, GRID_W, LANES), lambda h: (h, 0, 0, 0)),
        out_shape=jax.ShapeDtypeStruct((HEADS, NA_TAB, GRID_W, LANES), F32),
        compiler_params=_cparams(16),
        name="na_bias_table",
    )(rpb.reshape(-1))


def _attn_b_body(tab_ref, q_ref, k0, k1, k2, k3, k4, kx_ref, v0, v1, v2, v3, v4, vx_ref, o_ref, *, nbl):
    b = pl.program_id(0)
    first_lat = CTX // QB
    is_lat = b >= first_lat
    bl = jnp.maximum(b - first_lat, 0)
    ws = jnp.clip(bl - 2, 0, nbl - NA_KBLK)
    n_rows = 2 * nbl
    nloc = NA_KBLK * QB
    rs = [jnp.clip(2 * bl + qr - NB_KH // 2, 0, n_rows - NB_KH) for qr in range(2)]
    tidx = [[jnp.clip(2 * (ws + j) - (2 * bl + qr) + (NB_KH - 1), -1, NA_TAB - 2) + 1 for j in range(NA_KBLK)]
            for qr in range(2)]
    qrow = lax.broadcasted_iota(jnp.int32, (QB, nloc), 0)
    kcol = lax.broadcasted_iota(jnp.int32, (QB, nloc), 1)
    krow = 2 * ws + jnp.right_shift(kcol, 6)
    rs_q = jnp.where(qrow < GRID_W, rs[0], rs[1])
    row_ok = jnp.logical_and(jnp.logical_and(krow >= rs_q, krow < rs_q + NB_KH), is_lat)
    row_ok2 = jnp.concatenate([row_ok, row_ok], axis=0)
    lo = _lane_lo((QB, LANES))
    zero = jnp.zeros((QB, LANES), BF16)
    kl = (k0, k1, k2, k3, k4)
    vl = (v0, v1, v2, v3, v4)

    def scores(hp):
        q2 = q_ref[hp]
        qs = jnp.concatenate([jnp.where(lo, q2, zero), jnp.where(lo, zero, q2)], axis=0)
        keys = jnp.concatenate([r[hp] for r in kl] + [kx_ref[hp]], axis=0)
        return lax.dot_general(qs, keys, (((1,), (1,)), ((), ())), preferred_element_type=F32)

    def probs(hp, s):
        bias = jnp.concatenate(
            [jnp.concatenate([tab_ref[2 * hp + hh, tidx[qr][j]] for j in range(NA_KBLK)], axis=1)
             for hh in range(2) for qr in range(2)], axis=0)
        s_loc = jnp.where(row_ok2, s[:, :nloc] + bias, NEG)
        s_ctx = s[:, nloc:]
        m = jnp.maximum(jnp.max(s_loc, axis=-1, keepdims=True), jnp.max(s_ctx, axis=-1, keepdims=True))
        e_loc = jnp.exp2(s_loc - m)
        e_ctx = jnp.exp2(s_ctx - m)
        denom = jnp.sum(e_loc, axis=-1, keepdims=True) + jnp.sum(e_ctx, axis=-1, keepdims=True)
        return jnp.concatenate([e_loc, e_ctx], axis=1).astype(BF16), denom

    def output(hp, p, denom):
        vals = jnp.concatenate([r[hp] for r in vl] + [vx_ref[hp]], axis=0)
        o = jnp.dot(p, vals, preferred_element_type=F32) / denom
        o_ref[hp] = jnp.where(lo, o[:QB], o[QB:]).astype(BF16)

    _pipeline_pairs(scores, probs, output)


def _attn_b_call(qkv3, table):
    n = qkv3.shape[1]
    nb = n // QB
    first_lat = CTX // QB
    nbl = nb - first_lat

    def win(blk, i):
        return pl.BlockSpec(
            (HP, QB, LANES),
            lambda b: (blk, first_lat + jnp.clip(jnp.maximum(b - first_lat, 0) - 2, 0, nbl - NA_KBLK) + i, 0))

    def kv_specs(blk):
        return [win(blk, i) for i in range(NA_KBLK)] + [pl.BlockSpec((HP, CTX, LANES), lambda b: (blk, 0, 0))]

    return pl.pallas_call(
        functools.partial(_attn_b_body, nbl=nbl),
        grid=(nb,),
        in_specs=[_resident((HEADS, NA_TAB, GRID_W, LANES), lambda b: (0, 0, 0, 0)),
                  pl.BlockSpec((HP, QB, LANES), lambda b: (0, b, 0))] + kv_specs(1) + kv_specs(2),
        out_specs=pl.BlockSpec((HP, QB, LANES), lambda b: (0, b, 0)),
        out_shape=jax.ShapeDtypeStruct((HP, n, LANES), BF16),
        compiler_params=_cparams(52),
        name="attn_neighbourhood",
    )(table, qkv3, *([qkv3] * 12))


def _oproj_body(*refs, mode):
    if mode == "attn":
        o_ref, w_ref, h_ref, mod_ref, out_ref = refs
        z = jnp.concatenate([o_ref[c] for c in range(HP)], axis=1)
    else:
        ug_ref, hs_ref, w_ref, h_ref, mod_ref, out_ref = refs
        z = (jax.nn.gelu(ug_ref[...]) * hs_ref[...]).astype(BF16)
    y = jnp.dot(z, w_ref[...], preferred_element_type=F32)
    out_ref[...] = h_ref[...] + mod_ref[0, 2:3, :] * y


def _oproj_call(mix_in, w, h, mod, *, mode, name):
    n = h.shape[0]
    kdim = w.shape[0]
    if mode == "attn":
        in_specs = [pl.BlockSpec((HP, TM, LANES), lambda i: (0, i, 0))]
        args = [mix_in]
    else:
        u, hs = mix_in
        in_specs = [pl.BlockSpec((TM, D_RNN), lambda i: (i, 0)), pl.BlockSpec((TM, D_RNN), lambda i: (i, 0))]
        args = [u, hs]
    in_specs += [_resident((kdim, D), lambda i: (0, 0)), pl.BlockSpec((TM, D), lambda i: (i, 0)), _mod_spec()]
    return pl.pallas_call(
        functools.partial(_oproj_body, mode=mode),
        grid=(n // TM,),
        in_specs=in_specs,
        out_specs=pl.BlockSpec((TM, D), lambda i: (i, 0)),
        out_shape=jax.ShapeDtypeStruct((n, D), F32),
        compiler_params=_cparams(48),
        name=name,
    )(*args, w, h, mod[:, 0:3])


def _shift_rows(x, s, fill, reverse):
    t = x.shape[0]
    if s % SUBLANES == 0:
        pad = jnp.full((s, x.shape[1]), fill, x.dtype)
        return jnp.concatenate([x[s:], pad] if reverse else [pad, x[:t - s]], axis=0)
    row = lax.broadcasted_iota(jnp.int32, x.shape, 0)
    if reverse:
        return jnp.where(row >= t - s, fill, pltpu.roll(x, t - s, 0))
    return jnp.where(row < s, fill, pltpu.roll(x, s, 0))


def _chunk_scan(a, b, reverse):
    s = 1
    while s < a.shape[0]:
        a_s = _shift_rows(a, s, 1.0, reverse)
        b_s = _shift_rows(b, s, 0.0, reverse)
        b = a * b_s + b
        a = a * a_s
        s *= 2
    return a, b


def _group_scan(a, b, h_in, reverse, scr_ref):
    t, c = a.shape
    g = t // SUBLANES
    a3 = a.reshape(g, SUBLANES, c)
    b3 = b.reshape(g, SUBLANES, c)
    sub = lax.broadcasted_iota(jnp.int32, a3.shape, 1)
    s = 1
    while s < SUBLANES:
        edge = (sub >= SUBLANES - s) if reverse else (sub < s)
        shift = SUBLANES - s if reverse else s
        a_s = jnp.where(edge, 1.0, pltpu.roll(a3, shift, 1))
        b_s = jnp.where(edge, 0.0, pltpu.roll(b3, shift, 1))
        b3 = a3 * b_s + b3
        a3 = a3 * a_s
        s *= 2
    a_loc = a3.reshape(t, c)
    b_loc = b3.reshape(t, c)
    edge_row = 0 if reverse else SUBLANES - 1

    def boundary_rows(i, x):
        for j in range(c // LANES):
            scr_ref[i, j] = x[:, j * LANES:(j + 1) * LANES]
        return jnp.concatenate([scr_ref[i, j, pl.ds(edge_row, g, stride=SUBLANES), :] for j in range(c // LANES)],
                               axis=1)

    ga, gb = _chunk_scan(boundary_rows(0, a_loc), boundary_rows(1, b_loc), reverse)
    leaving = gb + ga * h_in
    entering = _shift_rows(leaving, 1, 0.0, reverse)
    grow = lax.broadcasted_iota(jnp.int32, leaving.shape, 0)
    entering = jnp.where(grow == (g - 1 if reverse else 0), h_in, entering)
    carry = jnp.broadcast_to(entering[:, None, :], (g, SUBLANES, c))
    h = (a3 * carry + b3).reshape(t, c)
    return h, (leaving[0:1, :] if reverse else leaving[g - 1:g, :])


def _lru_body(x_ref, cw_ref, cb_ref, wa_ref, wx_ref, ba_ref, bx_ref, lam_ref, o_ref, scr_ref, xc_ref, *, nchunks):
    n = nchunks * SCAN_TC
    halo = SUBLANES
    cw = cw_ref[...]
    cb = cb_ref[...]
    sp = jax.nn.softplus(-lam_ref[...])

    def conv_chunk(ci):
        r0 = pl.multiple_of(ci * SCAN_TC, SCAN_TC)
        lat_first = CTX // SCAN_TC
        prev_ok = jnp.logical_and(ci != 0, ci != lat_first).astype(F32)
        next_ok = jnp.logical_and(ci != lat_first - 1, ci != nchunks - 1).astype(F32)
        p0 = pl.multiple_of(jnp.maximum(r0 - halo, 0), halo)
        n0 = pl.multiple_of(jnp.minimum(r0 + SCAN_TC, n - halo), halo)
        xe = jnp.concatenate([x_ref[pl.ds(p0, halo), :] * prev_ok,
                              x_ref[pl.ds(r0, SCAN_TC), :],
                              x_ref[pl.ds(n0, halo), :] * next_ok], axis=0)
        y = cb
        for j in range(CONV_W):
            off = halo + j - CONV_LEFT
            y = y + xe[off:off + SCAN_TC, :] * cw[j:j + 1, :]
        return r0, y

    def direction(d, ci, h_in):
        if d == 0:
            r0, xc = conv_chunk(ci)
            xc_ref[pl.ds(r0, SCAN_TC), :] = xc
        else:
            r0 = pl.multiple_of(ci * SCAN_TC, SCAN_TC)
            xc = xc_ref[pl.ds(r0, SCAN_TC), :]
        xb = xc.astype(BF16)
        r = jax.nn.sigmoid(jnp.dot(xb, wa_ref[d, 0], preferred_element_type=F32) + ba_ref[d:d + 1, :])
        ig = jax.nn.sigmoid(jnp.dot(xb, wx_ref[d, 0], preferred_element_type=F32) + bx_ref[d:d + 1, :])
        log_a = -LRU_C * r * sp[d:d + 1, :]
        a = jnp.exp(log_a)
        bb = jnp.sqrt(1.0 - a * a) * (ig * xc)
        h, h_out = _group_scan(a, bb, h_in, d == 1, scr_ref)
        return r0, h, h_out

    def fwd(ci, h_in):
        r0, h, h_out = direction(0, ci, h_in)
        o_ref[pl.ds(r0, SCAN_TC), :] = h
        return h_out

    def bwd(ci, h_in):
        r0, h, h_out = direction(1, ci, h_in)
        o_ref[pl.ds(r0, SCAN_TC), :] += h
        return h_out

    h0 = jnp.zeros((1, RNN_BW), F32)
    lax.fori_loop(0, nchunks, fwd, h0)
    nctx = CTX // SCAN_TC
    hc = lax.fori_loop(0, nctx, lambda k, hh: bwd(nctx - 1 - k, hh), h0)
    lax.fori_loop(0, nchunks - nctx, lambda k, hh: bwd(nchunks - 1 - k, hh), hc)


def _lru_call(u, conv_w, conv_b, wa, wx, ba, bx, lam):
    n = u.shape[0]
    return pl.pallas_call(
        functools.partial(_lru_body, nchunks=n // SCAN_TC),
        grid=(RNN_BLOCKS,),
        in_specs=[_resident((n, RNN_BW), lambda m: (0, RNN_BLOCKS + m)),
                  pl.BlockSpec((CONV_W, RNN_BW), lambda m: (0, m)),
                  pl.BlockSpec((1, RNN_BW), lambda m: (0, m)),
                  pl.BlockSpec((2, 1, RNN_BW, RNN_BW), lambda m: (0, m, 0, 0)),
                  pl.BlockSpec((2, 1, RNN_BW, RNN_BW), lambda m: (0, m, 0, 0)),
                  pl.BlockSpec((2, RNN_BW), lambda m: (0, m)),
                  pl.BlockSpec((2, RNN_BW), lambda m: (0, m)),
                  pl.BlockSpec((2, RNN_BW), lambda m: (0, m))],
        out_specs=_resident((n, RNN_BW), lambda m: (0, m)),
        scratch_shapes=[pltpu.VMEM((2, RNN_BW // LANES, SCAN_TC, LANES), F32), pltpu.VMEM((n, RNN_BW), F32)],
        out_shape=jax.ShapeDtypeStruct((n, D_RNN), F32),
        compiler_params=_cparams(58),
        name="rglru_scan",
    )(u, conv_w, conv_b.reshape(1, D_RNN), wa, wx, ba, bx, lam)


def _pack_rows(ref, x):
    rows = x.shape[0]
    lo = lax.bitcast_convert_type(x[:, :D // 2].astype(BF16).astype(F32), jnp.uint32)
    hi = lax.bitcast_convert_type(x[:, D // 2:].astype(BF16).astype(F32), jnp.uint32)
    w = hi | (lo >> 16)
    for s in range(XROW_TILE):
        ref[pl.ds(s, rows, stride=XROW_TILE), :] = w[:, s * LANES:(s + 1) * LANES]


def _unpack_rows(ref, rows, dtype):
    w = jnp.concatenate([ref[pl.ds(s, rows, stride=XROW_TILE), :] for s in range(XROW_TILE)], axis=1)
    lo = lax.bitcast_convert_type(w << 16, F32)
    hi = lax.bitcast_convert_type(w & jnp.uint32(0xFFFF0000), F32)
    return jnp.concatenate([lo, hi], axis=1).astype(dtype)


def _router_body(h_ref, g_ref, mod_ref, wrh_ref, wrl_ref, br_ref, f_ref, r_ref, cnt_ref, base_ref):
    @pl.when(pl.program_id(0) == 0)
    def _():
        base_ref[...] = jnp.zeros(base_ref.shape, F32)

    f = _norm_mod(h_ref[...], g_ref[...], mod_ref[0, 0:1, :], mod_ref[0, 1:2, :])
    _pack_rows(f_ref, f)
    f_hi = f.astype(BF16)
    f_lo = (f - f_hi.astype(F32)).astype(BF16)
    logits = (jnp.dot(jnp.concatenate([f_hi, f_lo], axis=0), wrh_ref[...], preferred_element_type=F32).reshape(2, TM, LANES).sum(axis=0)
              + jnp.dot(f_hi, wrl_ref[...], preferred_element_type=F32) + br_ref[...])
    lane = lax.broadcasted_iota(jnp.int32, logits.shape, 1)
    ninf = -jnp.inf
    is_g = lane < N_GROUPS
    gl = jnp.where(is_g, logits, ninf)
    gmax = jnp.max(gl, axis=-1, keepdims=True)
    gsel = jnp.min(jnp.where(gl == gmax, lane, LANES), axis=-1, keepdims=True)
    gsum = jnp.sum(jnp.where(is_g, jnp.exp(gl - gmax), 0.0), axis=-1, keepdims=True)
    g_w = 1.0 / gsum
    e_lo = N_GROUPS + EXP_PER_GROUP * gsel
    in_grp = jnp.logical_and(lane >= e_lo, lane < e_lo + EXP_PER_GROUP)
    el = jnp.where(in_grp, logits, ninf)
    m1 = jnp.max(el, axis=-1, keepdims=True)
    i1 = jnp.min(jnp.where(el == m1, lane, LANES), axis=-1, keepdims=True)
    el2 = jnp.where(lane == i1, ninf, el)
    m2 = jnp.max(el2, axis=-1, keepdims=True)
    i2 = jnp.min(jnp.where(el2 == m2, lane, LANES), axis=-1, keepdims=True)
    t = jnp.exp(m2 - m1)
    w1 = g_w / (1.0 + t)
    w2 = g_w * t / (1.0 + t)
    oh1 = lane == i1
    oh2 = lane == i2
    rr = lax.broadcasted_iota(jnp.int32, (TM, TM), 0)
    cc = lax.broadcasted_iota(jnp.int32, (TM, TM), 1)
    tri = (cc < rr).astype(BF16)
    pre1 = jnp.dot(tri, oh1.astype(BF16), preferred_element_type=F32)
    pre2 = jnp.dot(tri, oh2.astype(BF16), preferred_element_type=F32)
    base = base_ref[...]
    cnt1 = jnp.sum(oh1.astype(F32), axis=0, keepdims=True)
    cnt2 = jnp.sum(oh2.astype(F32), axis=0, keepdims=True)
    rank1 = jnp.sum(jnp.where(oh1, pre1 + base, 0.0), axis=-1, keepdims=True)
    rank2 = jnp.sum(jnp.where(oh2, pre2 + (base + cnt1), 0.0), axis=-1, keepdims=True)
    total = base + cnt1 + cnt2
    base_ref[...] = total
    cnt_ref[...] = total
    cols = [(i1 - N_GROUPS).astype(F32), (i2 - N_GROUPS).astype(F32), w1, w2, rank1, rank2]
    out = jnp.zeros(logits.shape, F32)
    for k, v in enumerate(cols):
        out = jnp.where(lane == k, v, out)
    r_ref[...] = out


def _router_call(h, g, mod, wr, br):
    n = h.shape[0]
    wr_hi = wr.astype(BF16)
    wr_lo = (wr - wr_hi.astype(F32)).astype(BF16)
    return pl.pallas_call(
        _router_body,
        grid=(n // TM,),
        in_specs=[pl.BlockSpec((TM, D), lambda i: (i, 0)),
                  pl.BlockSpec((1, D), lambda i: (0, 0)),
                  _mod_spec(),
                  pl.BlockSpec((D, LANES), lambda i: (0, 0)),
                  pl.BlockSpec((D, LANES), lambda i: (0, 0)),
                  pl.BlockSpec((1, LANES), lambda i: (0, 0))],
        out_specs=[pl.BlockSpec((TM * XROW_TILE, LANES), lambda i: (i, 0)),
                   pl.BlockSpec((TM, LANES), lambda i: (i, 0)),
                   pl.BlockSpec((1, LANES), lambda i: (0, 0))],
        out_shape=[jax.ShapeDtypeStruct((n * XROW_TILE, LANES), jnp.uint32),
                   jax.ShapeDtypeStruct((n, LANES), F32),
                   jax.ShapeDtypeStruct((1, LANES), F32)],
        scratch_shapes=[pltpu.VMEM((1, LANES), F32)],
        compiler_params=_cparams(32),
        name="moe_router",
    )(h, g.reshape(1, D), mod[:, 3:6], wr_hi, wr_lo, br)


def _dispatch_body(dest_ref, pad_lo_ref, pad_n_ref, f_ref, xg_ref, z_ref, sem):
    i = pl.program_id(0)
    base = i * TM

    def issue(j, carry):
        for k in range(TOP_K):
            d = dest_ref[(base + j) * TOP_K + k]
            pltpu.make_async_copy(f_ref.at[pl.ds(j * XROW_TILE, XROW_TILE)],
                                  xg_ref.at[pl.ds(d * XROW_TILE, XROW_TILE)], sem).start(priority=k)
        return carry

    def wait_one(carry):
        pltpu.make_async_copy(z_ref, xg_ref.at[pl.ds(0, XROW_TILE)], sem).wait()
        return carry

    lax.fori_loop(0, TM, issue, 0, unroll=DMA_UNROLL)

    @pl.when(i == 0)
    def _():
        z_ref[...] = jnp.zeros(z_ref.shape, jnp.uint32)

        def fill(e, carry):
            lo = pad_lo_ref[e]
            n_fill = pad_n_ref[e]

            def batch(bi, c):
                s0 = bi * MOE_MB
                cnt = jnp.minimum(n_fill - s0, MOE_MB)

                def one(s, c2):
                    pltpu.make_async_copy(z_ref, xg_ref.at[pl.ds((lo + s0 + s) * XROW_TILE, XROW_TILE)], sem).start()
                    return c2

                lax.fori_loop(0, cnt, one, 0)
                lax.fori_loop(0, cnt, lambda s, c2: wait_one(c2), 0)
                return c

            lax.fori_loop(0, (n_fill + (MOE_MB - 1)) // MOE_MB, batch, 0)
            return carry

        lax.fori_loop(0, N_EXPERTS + 1, fill, 0)

    lax.fori_loop(0, TM * TOP_K, lambda j, c: wait_one(c), 0, unroll=DMA_UNROLL)


def _dispatch_call(dest, pad_lo, pad_n, f2, p):
    n = f2.shape[0] // XROW_TILE
    return pl.pallas_call(
        _dispatch_body,
        grid_spec=pltpu.PrefetchScalarGridSpec(
            num_scalar_prefetch=3,
            grid=(n // TM,),
            in_specs=[pl.BlockSpec((TM * XROW_TILE, LANES), lambda i, d, lo, nn: (i, 0))],
            out_specs=pl.BlockSpec(memory_space=pl.ANY),
            scratch_shapes=[pltpu.VMEM((XROW_TILE, LANES), jnp.uint32), pltpu.SemaphoreType.DMA(())]),
        out_shape=jax.ShapeDtypeStruct((p * XROW_TILE, LANES), jnp.uint32),
        compiler_params=_cparams(16),
        name="moe_dispatch",
    )(dest, pad_lo, pad_n, f2)


def _expert_weights(b, be_ref, ne_ref, nu_ref, w_hbm, wf_ref, wb_ref, sem, slot_ref, layer):
    e = be_ref[b]
    first = jnp.logical_or(b == 0, e != be_ref[jnp.maximum(b - 1, 0)])

    def fetch(expert, slot):
        return pltpu.make_async_copy(w_hbm.at[layer, expert], wf_ref.at[slot], sem.at[slot])

    @pl.when(b == 0)
    def _():
        slot_ref[0] = 0
        fetch(e, 0).start()

    @pl.when(jnp.logical_and(first, b < nu_ref[0]))
    def _():
        s = slot_ref[0]
        fetch(e, s).wait()
        wb_ref[...] = wf_ref[s].astype(BF16)
        ne = ne_ref[b]

        @pl.when(ne != e)
        def _():
            fetch(ne, 1 - s).start()

        slot_ref[0] = 1 - s


def _ffn_up_body(be_ref, ne_ref, nu_ref, x_ref, w_hbm, a_ref, wf_ref, wb_ref, sem, slot_ref, *, layer):
    b = pl.program_id(0)
    _expert_weights(b, be_ref, ne_ref, nu_ref, w_hbm, wf_ref, wb_ref, sem, slot_ref, layer)

    @pl.when(b < nu_ref[0])
    def _():
        hgu = jnp.dot(_unpack_rows(x_ref, MOE_MB, BF16), wb_ref[...], preferred_element_type=F32)
        g = hgu[:, :D_EXPERT]
        u = hgu[:, D_EXPERT:]
        a_ref[...] = (g * jax.nn.sigmoid(g) * u).astype(BF16)

    @pl.when(b >= nu_ref[0])
    def _():
        a_ref[...] = jnp.zeros(a_ref.shape, BF16)


def _expert_scratch(rows, cols):
    return [pltpu.VMEM((2, rows, cols), F32), pltpu.VMEM((rows, cols), BF16),
            pltpu.SemaphoreType.DMA((2,)), pltpu.SMEM((1,), jnp.int32)]


def _ffn_up_call(block_e, next_e, n_used, xg, w_gu, layer):
    p = xg.shape[0] // XROW_TILE
    return pl.pallas_call(
        functools.partial(_ffn_up_body, layer=layer),
        grid_spec=pltpu.PrefetchScalarGridSpec(
            num_scalar_prefetch=3,
            grid=(p // MOE_MB,),
            in_specs=[pl.BlockSpec((MOE_MB * XROW_TILE, LANES),
                                   lambda b, be, ne, nu: (jnp.minimum(b, nu[0] - 1), 0)),
                      pl.BlockSpec(memory_space=pl.ANY)],
            out_specs=pl.BlockSpec((MOE_MB, D_EXPERT), lambda b, be, ne, nu: (b, 0)),
            scratch_shapes=_expert_scratch(D, 2 * D_EXPERT)),
        out_shape=jax.ShapeDtypeStruct((p, D_EXPERT), BF16),
        compiler_params=_cparams(48),
        name="moe_ffn_up",
    )(block_e, next_e, n_used, xg, w_gu)


def _ffn_down_body(be_ref, ne_ref, nu_ref, a_ref, w_hbm, y_ref, wf_ref, wb_ref, sem, slot_ref, *, layer):
    b = pl.program_id(0)
    _expert_weights(b, be_ref, ne_ref, nu_ref, w_hbm, wf_ref, wb_ref, sem, slot_ref, layer)

    @pl.when(b < nu_ref[0])
    def _():
        _pack_rows(y_ref, jnp.dot(a_ref[...], wb_ref[...], preferred_element_type=F32))

    @pl.when(b >= nu_ref[0])
    def _():
        y_ref[...] = jnp.zeros(y_ref.shape, jnp.uint32)


def _ffn_down_call(block_e, next_e, n_used, act, w_down, layer):
    p = act.shape[0]
    return pl.pallas_call(
        functools.partial(_ffn_down_body, layer=layer),
        grid_spec=pltpu.PrefetchScalarGridSpec(
            num_scalar_prefetch=3,
            grid=(p // MOE_MB,),
            in_specs=[pl.BlockSpec((MOE_MB, D_EXPERT), lambda b, be, ne, nu: (b, 0)),
                      pl.BlockSpec(memory_space=pl.ANY)],
            out_specs=pl.BlockSpec((MOE_MB * XROW_TILE, LANES), lambda b, be, ne, nu: (b, 0)),
            scratch_shapes=_expert_scratch(D_EXPERT, D)),
        out_shape=jax.ShapeDtypeStruct((p * XROW_TILE, LANES), jnp.uint32),
        compiler_params=_cparams(32),
        name="moe_ffn_down",
    )(block_e, next_e, n_used, act, w_down)


def _combine_body(dest_ref, h_ref, r_ref, mod_ref, g_ref, yb_ref, o_ref, buf_ref, sem, *, final):
    i = pl.program_id(0)
    slot = i % 2

    def gather(tile, dst_slot):
        base = tile * TM

        def issue(j, carry):
            for k in range(TOP_K):
                d = dest_ref[(base + j) * TOP_K + k]
                pltpu.make_async_copy(yb_ref.at[pl.ds(d * XROW_TILE, XROW_TILE)],
                                      buf_ref.at[dst_slot, k, pl.ds(j * XROW_TILE, XROW_TILE)],
                                      sem.at[dst_slot]).start(priority=k)
            return carry

        lax.fori_loop(0, TM, issue, 0, unroll=DMA_UNROLL)

    @pl.when(i == 0)
    def _():
        gather(0, 0)

    @pl.when(i + 1 < pl.num_programs(0))
    def _():
        gather(i + 1, 1 - slot)

    def drain(j, carry):
        pltpu.make_async_copy(yb_ref.at[pl.ds(0, XROW_TILE)], buf_ref.at[slot, 0, pl.ds(0, XROW_TILE)],
                              sem.at[slot]).wait()
        return carry

    lax.fori_loop(0, TM * TOP_K, drain, 0, unroll=DMA_UNROLL)
    r = r_ref[...]
    y = sum(r[:, TOP_K + k:TOP_K + k + 1] * _unpack_rows(buf_ref.at[slot, k], TM, F32) for k in range(TOP_K))
    h_new = h_ref[...] + mod_ref[0, 2:3, :] * y
    if final:
        ms = jnp.mean(h_new * h_new, axis=-1, keepdims=True)
        h_new = h_new * lax.rsqrt(ms + EPS) * g_ref[...]
    o_ref[...] = h_new


def _combine_call(dest, h, route, mod, yb, final_g=None):
    n = h.shape[0]
    final = final_g is not None
    first = CTX // TM
    out_rows = n - CTX if final else n
    out_map = (lambda i, d: (jnp.maximum(i - first, 0), 0)) if final else (lambda i, d: (i, 0))
    g = final_g if final else jnp.ones((D,), F32)
    return pl.pallas_call(
        functools.partial(_combine_body, final=final),
        grid_spec=pltpu.PrefetchScalarGridSpec(
            num_scalar_prefetch=1,
            grid=(n // TM,),
            in_specs=[pl.BlockSpec((TM, D), lambda i, d: (i, 0)),
                      pl.BlockSpec((TM, LANES), lambda i, d: (i, 0)),
                      pl.BlockSpec((1, 3, D), lambda i, d: (jnp.minimum(i, 1), 0, 0)),
                      pl.BlockSpec((1, D), lambda i, d: (0, 0)),
                      pl.BlockSpec(memory_space=pl.ANY)],
            out_specs=pl.BlockSpec((TM, D), out_map),
            scratch_shapes=[pltpu.VMEM((2, TOP_K, TM * XROW_TILE, LANES), jnp.uint32),
                            pltpu.SemaphoreType.DMA((2,))]),
        out_shape=jax.ShapeDtypeStruct((out_rows, D), F32),
        compiler_params=_cparams(32),
        name="moe_combine",
    )(dest, h, route, mod[:, 3:6], g.reshape(1, D), yb)


def _dispatch_plan(route, cnt, n):
    experts = jnp.arange(N_EXPERTS, dtype=jnp.int32)
    counts = cnt[0, N_GROUPS:N_GROUPS + N_EXPERTS].astype(jnp.int32)
    padded = (counts + MOE_MB - 1) // MOE_MB * MOE_MB
    end_pad = jnp.cumsum(padded)
    start_pad = end_pad - padded
    n_blocks = -(-(n * TOP_K + N_EXPERTS * (MOE_MB - 1)) // MOE_MB)
    p = n_blocks * MOE_MB
    blk_start = jnp.arange(n_blocks, dtype=jnp.int32) * MOE_MB
    block_e = jnp.minimum(jnp.sum(end_pad[None, :] <= blk_start[:, None], axis=1), N_EXPERTS - 1).astype(jnp.int32)
    n_used = (end_pad[-1:] // MOE_MB).astype(jnp.int32)
    group_end = end_pad[block_e] // MOE_MB
    next_e = jnp.where(group_end < n_used[0], block_e[jnp.minimum(group_end, n_blocks - 1)], block_e)
    e_idx = route[:, 0:TOP_K].astype(jnp.int32)
    rank = route[:, 2 * TOP_K:3 * TOP_K].astype(jnp.int32)
    start = jnp.sum(jnp.where(e_idx[..., None] == experts, start_pad, 0), axis=-1)
    dest = (start + rank).reshape(n * TOP_K)
    pad_lo = jnp.concatenate([start_pad + counts, end_pad[-1:]]).astype(jnp.int32)
    pad_n = jnp.concatenate([padded - counts, p - end_pad[-1:]]).astype(jnp.int32)
    return p, block_e, next_e, n_used, dest, pad_lo, pad_n


def _moe_layer(h, g, mod, wr, br, w_gu, w_down, layer, final_g=None):
    n = h.shape[0]
    f2, route, cnt = _router_call(h, g, mod, wr, br)
    p, block_e, next_e, n_used, dest, pad_lo, pad_n = _dispatch_plan(route, cnt, n)
    xg = _dispatch_call(dest, pad_lo, pad_n, f2, p)
    act = _ffn_up_call(block_e, next_e, n_used, xg, w_gu, layer)
    yb = _ffn_down_call(block_e, next_e, n_used, act, w_down, layer)
    return _combine_call(dest, h, route, mod, yb, final_g)


def _rope_tables(l):
    quarter = HEAD_DIM // 4
    inv = ROPE_BASE ** (-jnp.arange(quarter, dtype=F32) / quarter)
    n_rows = l // GRID_W
    ang_r = jnp.arange(n_rows, dtype=F32)[:, None] * inv
    ang_c = jnp.arange(GRID_W, dtype=F32)[:, None] * inv
    by_row = lambda t: jnp.broadcast_to(t[:, None, :], (n_rows, GRID_W, quarter)).reshape(l, quarter)
    by_col = lambda t: jnp.broadcast_to(t[None, :, :], (n_rows, GRID_W, quarter)).reshape(l, quarter)
    cr, sr, cc, sc = by_row(jnp.cos(ang_r)), by_row(jnp.sin(ang_r)), by_col(jnp.cos(ang_c)), by_col(jnp.sin(ang_c))
    cos = jnp.concatenate([cr, cr, cc, cc], axis=1)
    sin = jnp.concatenate([-sr, sr, -sc, sc], axis=1)
    cos = jnp.concatenate([jnp.ones((CTX, HEAD_DIM), F32), cos], axis=0)
    sin = jnp.concatenate([jnp.zeros((CTX, HEAD_DIM), F32), sin], axis=0)
    return jnp.tile(cos, (1, 2)), jnp.tile(sin, (1, 2))


def _attn_a_weights(w_qkv):
    nq = HEADS * HEAD_DIM
    nkv = A_KV_HEADS * HEAD_DIM
    wq = w_qkv[:, :nq]
    wk = w_qkv[:, nq:nq + nkv].reshape(D, A_KV_HEADS, 1, HEAD_DIM)
    wv = w_qkv[:, nq + nkv:].reshape(D, A_KV_HEADS, 1, HEAD_DIM)
    dup = lambda w: jnp.broadcast_to(w, (D, A_KV_HEADS, 2, HEAD_DIM)).reshape(D, 2 * nkv)
    return jnp.concatenate([wq, dup(wk), dup(wv)], axis=1).astype(BF16)


def kernel(x, c, ctx, c_ctx, ada_w, ada_b, norm_mix_g, norm_ffn_g, router_group_w, router_group_b,
           router_expert_w, router_expert_b, moe_w_gu, moe_w_down, attn_w_qkv, attn_w_o, attn_sink,
           na_w_qkv, na_w_o, na_rpb, rnn_w_in, rnn_conv_w, rnn_conv_b, rnn_wa, rnn_ba, rnn_wx, rnn_bx,
           rnn_lam, rnn_w_out, final_norm_g):
    batch, l, _ = x.shape
    assert batch == 1 and ctx.shape[1] == CTX and l % (QB * 2) == 0 and l // QB >= NA_KBLK
    h = jnp.concatenate([ctx[0], x[0]], axis=0)
    c2 = jnp.stack([c_ctx, c[0]], axis=1)
    mods = _ada_call(c2, ada_w, ada_b).reshape(DEPTH, 2, 6, D)
    rope = _rope_tables(l)
    pad_r = LANES - N_GROUPS - N_EXPERTS
    for i in range(DEPTH):
        kind, j = i % 3, i // 3
        mod = mods[i]
        if kind == 0:
            w = _attn_a_weights(attn_w_qkv[j])
            nq = HEADS * HEAD_DIM
            qkv3 = _proj_call(h, norm_mix_g[i], mod, w, mode="rope", rope=rope,
                              n_rope=nq + 2 * A_KV_HEADS * HEAD_DIM, n_q=nq, name="proj_window")
            o3 = _attn_a_call(qkv3, attn_sink[j])
            h = _oproj_call(o3, attn_w_o[j].astype(BF16), h, mod, mode="attn", name="oproj_window")
        elif kind == 1:
            qkv3 = _proj_call(h, norm_mix_g[i], mod, na_w_qkv[j].astype(BF16), mode="cols",
                              n_q=HEADS * HEAD_DIM, name="proj_neighbourhood")
            o3 = _attn_b_call(qkv3, _na_table_call(na_rpb[j]))
            h = _oproj_call(o3, na_w_o[j].astype(BF16), h, mod, mode="attn", name="oproj_neighbourhood")
        else:
            u = _proj_call(h, norm_mix_g[i], mod, rnn_w_in[j].astype(BF16), mode="plain", name="proj_rglru")
            hs = _lru_call(u, rnn_conv_w[j], rnn_conv_b[j], rnn_wa[j].astype(BF16), rnn_wx[j].astype(BF16),
                           rnn_ba[j], rnn_bx[j], rnn_lam[j])
            h = _oproj_call((u, hs), rnn_w_out[j].astype(BF16), h, mod, mode="rnn", name="oproj_rglru")
        wr = jnp.concatenate([router_group_w[i], router_expert_w[i], jnp.zeros((D, pad_r), F32)], axis=1)
        br = jnp.concatenate([router_group_b[i], router_expert_b[i], jnp.zeros((pad_r,), F32)]).reshape(1, LANES)
        h = _moe_layer(h, norm_ffn_g[i], mod, wr, br, moe_w_gu, moe_w_down, i,
                       final_norm_g if i == DEPTH - 1 else None)
    return h[None]
```

```python
import functools

import jax
import jax.numpy as jnp
from jax import lax
from jax.experimental import pallas as pl
from jax.experimental.pallas import tpu as pltpu

F32 = jnp.float32
BF16 = jnp.bfloat16

D = 2048
DEPTH = 4
GRID_W = 64
CTX = 256
HEADS = 32
HEAD_DIM = 64
A_KV_HEADS = 4
A_GROUP = HEADS // A_KV_HEADS
WINDOW = 128
NB_KH = 8
NB_KW = 16
D_RNN = 2560
RNN_BLOCKS = 10
RNN_BW = D_RNN // RNN_BLOCKS
CONV_W = 4
CONV_LEFT = 2
LRU_C = 8.0
N_GROUPS = 4
EXP_PER_GROUP = 8
N_EXPERTS = N_GROUPS * EXP_PER_GROUP
TOP_K = 2
D_EXPERT = 768
ROPE_BASE = 10000.0
EPS = 1e-6
NEG = -1e30

LANES = 128
SUBLANES = 8
MIB = 1024 * 1024

TM = 256
QB = 128
HP = HEADS // 2
MOE_MB = 256
ROW_TILE = D // LANES
XROW_TILE = ROW_TILE // 2
DMA_UNROLL = 8
FFN_CHUNK = 256
SCAN_TC = 256
NA_KBLK = 5
SQRT_SCALE = HEAD_DIM ** -0.5
LOG2E = 1.4426950408889634


def _cparams(vmem_mib, sem=("arbitrary",)):
    return pltpu.CompilerParams(dimension_semantics=sem, vmem_limit_bytes=int(vmem_mib * MIB))


def _resident(block_shape, index_map):
    return pl.BlockSpec(block_shape, index_map, pipeline_mode=pl.Buffered(1))


def _mod_spec():
    return pl.BlockSpec((1, 3, D), lambda i: (jnp.minimum(i, 1), 0, 0))


def _norm_mod(x, g, shift, scale):
    ms = jnp.mean(x * x, axis=-1, keepdims=True)
    y = x * lax.rsqrt(ms + EPS) * g
    return y * (1.0 + scale) + shift


ADA_TN = 1024


def _ada_body(c_ref, w_ref, b_ref, o_ref):
    c = c_ref[...]
    cs = c * jax.nn.sigmoid(c)
    for r in range(2):
        cb = jnp.broadcast_to(cs[:, r:r + 1], (D, LANES))
        outs = []
        for j in range(ADA_TN // LANES):
            w = w_ref[0, :, j * LANES:(j + 1) * LANES]
            p = (w * cb).reshape(D // SUBLANES, SUBLANES, LANES).sum(axis=0)
            outs.append(p.sum(axis=0, keepdims=True))
        o_ref[0, r:r + 1, :] = jnp.concatenate(outs, axis=1) + b_ref[0]


def _ada_call(c2, ada_w, ada_b):
    return pl.pallas_call(
        _ada_body,
        grid=(DEPTH, 6 * D // ADA_TN),
        in_specs=[pl.BlockSpec((D, 2), lambda l, j: (0, 0)),
                  pl.BlockSpec((1, D, ADA_TN), lambda l, j: (l, 0, j)),
                  pl.BlockSpec((1, 1, ADA_TN), lambda l, j: (l, 0, j))],
        out_specs=pl.BlockSpec((1, 2, ADA_TN), lambda l, j: (l, 0, j)),
        out_shape=jax.ShapeDtypeStruct((DEPTH, 2, 6 * D), F32),
        compiler_params=_cparams(32, ("arbitrary", "arbitrary")),
        name="ada_mod",
    )(c2, ada_w, ada_b.reshape(DEPTH, 1, 6 * D))


PROJ_CH = 512


def _rope_piece(piece, cos, sin):
    lane = lax.broadcasted_iota(jnp.int32, piece.shape, 1)
    first = (lane & 16) == 0
    partner = jnp.where(first, pltpu.roll(piece, LANES - 16, 1), pltpu.roll(piece, 16, 1))
    return piece * cos + partner * sin


def _proj_body(*refs, mode, n_rope, n_q):
    if mode == "rope":
        x_ref, g_ref, mod_ref, w_ref, cos_ref, sin_ref, o_ref = refs
    else:
        x_ref, g_ref, mod_ref, w_ref, o_ref = refs
    a = _norm_mod(x_ref[...], g_ref[...], mod_ref[0, 0:1, :], mod_ref[0, 1:2, :]).astype(BF16)
    nout = w_ref.shape[1]
    for c in range(nout // PROJ_CH):
        acc = jnp.dot(a, w_ref[:, c * PROJ_CH:(c + 1) * PROJ_CH], preferred_element_type=F32)
        if mode == "plain":
            o_ref[:, c * PROJ_CH:(c + 1) * PROJ_CH] = acc
            continue
        for k in range(PROJ_CH // LANES):
            col0 = c * PROJ_CH + k * LANES
            piece = acc[:, k * LANES:(k + 1) * LANES]
            if col0 < n_rope:
                piece = _rope_piece(piece, cos_ref[...], sin_ref[...])
            if col0 < n_q:
                piece = piece * (SQRT_SCALE * LOG2E)
            o_ref[col0 // LANES] = piece.astype(BF16)


def _proj_call(h, g, mod, w, *, mode, rope=None, n_rope=0, n_q=0, name):
    n = h.shape[0]
    nout = w.shape[1]
    in_specs = [pl.BlockSpec((TM, D), lambda i: (i, 0)),
                pl.BlockSpec((1, D), lambda i: (0, 0)),
                _mod_spec(),
                _resident((D, nout), lambda i: (0, 0))]
    args = [h, g.reshape(1, D), mod[:, 0:3], w]
    if mode == "rope":
        in_specs += [pl.BlockSpec((TM, LANES), lambda i: (i, 0))] * 2
        args += list(rope)
    if mode == "plain":
        out_spec = pl.BlockSpec((TM, nout), lambda i: (i, 0))
        out_shape = jax.ShapeDtypeStruct((n, nout), F32)
        out_bytes = TM * nout * 4
    else:
        out_spec = pl.BlockSpec((nout // LANES, TM, LANES), lambda i: (0, i, 0))
        out_shape = jax.ShapeDtypeStruct((nout // LANES, n, LANES), BF16)
        out_bytes = TM * nout * 2
    vmem = (D * nout * 2 + 2 * TM * D * 4 + 2 * out_bytes) / MIB + 12
    return pl.pallas_call(
        functools.partial(_proj_body, mode=mode, n_rope=n_rope, n_q=n_q),
        grid=(n // TM,),
        in_specs=in_specs,
        out_specs=out_spec,
        out_shape=out_shape,
        compiler_params=_cparams(vmem),
        name=name,
    )(*args)


def _lane_lo(shape):
    return lax.broadcasted_iota(jnp.int32, shape, len(shape) - 1) < HEAD_DIM


def _pipeline_pairs(scores, probs, output):
    s_next = scores(0)
    pending = None
    for hp in range(HP):
        s_cur = s_next
        if hp + 1 < HP:
            s_next = scores(hp + 1)
        cur = probs(hp, s_cur)
        if pending is not None:
            output(hp - 1, *pending)
        pending = cur
    output(HP - 1, *pending)


def _attn_a_body(sink_ref, q_ref, kp_ref, kc_ref, kn_ref, kx_ref, vp_ref, vc_ref, vn_ref, vx_ref, o_ref, *, nb):
    b = pl.program_id(0)
    first_lat = CTX // QB
    is_lat = b >= first_lat
    prev_ok = b >= first_lat + 1
    next_ok = jnp.logical_and(is_lat, b <= nb - 2)
    nloc = 3 * QB
    nkeys = nloc + CTX
    qi = lax.broadcasted_iota(jnp.int32, (QB, nkeys), 0)
    kj = lax.broadcasted_iota(jnp.int32, (QB, nkeys), 1)
    rel = qi + WINDOW - kj
    band = jnp.abs(rel) <= WINDOW
    seg_ok = jnp.where(kj < QB, prev_ok.astype(jnp.int32),
                       jnp.where(kj < 2 * QB, is_lat.astype(jnp.int32), next_ok.astype(jnp.int32)))
    ok = jnp.logical_or(kj >= nloc, jnp.logical_and(band, seg_ok > 0))
    mask_add = jnp.where(ok, 0.0, NEG)
    lo = _lane_lo((QB, LANES))
    zero = jnp.zeros((QB, LANES), BF16)
    def scores(hp):
        kvh = hp // (A_GROUP // 2)
        keys = jnp.concatenate([kp_ref[kvh], kc_ref[kvh], kn_ref[kvh], kx_ref[kvh]], axis=0)
        q2 = q_ref[hp]
        qs = jnp.concatenate([jnp.where(lo, q2, zero), jnp.where(lo, zero, q2)], axis=0)
        return lax.dot_general(qs, keys, (((1,), (1,)), ((), ())), preferred_element_type=F32)

    def probs(hp, s):
        sink = jnp.concatenate([jnp.full((1, QB, 1), sink_ref[2 * hp + hh] * LOG2E, F32) for hh in range(2)], axis=0)
        s3 = s.reshape(2, QB, nkeys) + mask_add[None]
        m = jnp.maximum(jnp.max(s3, axis=-1, keepdims=True), sink)
        e = jnp.exp2(s3 - m)
        denom = jnp.sum(e, axis=-1, keepdims=True) + jnp.exp2(sink - m)
        return e.astype(BF16).reshape(2 * QB, nkeys), denom

    def output(hp, e, denom):
        kvh = hp // (A_GROUP // 2)
        vals = jnp.concatenate([vp_ref[kvh], vc_ref[kvh], vn_ref[kvh], vx_ref[kvh]], axis=0)
        o = jnp.dot(e, vals, preferred_element_type=F32).reshape(2, QB, LANES) / denom
        o_ref[hp] = jnp.where(lo, o[0], o[1]).astype(BF16)

    _pipeline_pairs(scores, probs, output)


def _attn_a_call(qkv3, sink):
    n = qkv3.shape[1]
    nb = n // QB
    first_lat = CTX // QB
    kblk, vblk = HP // A_KV_HEADS, HP // A_KV_HEADS + 1

    def kv_specs(blk):
        return [pl.BlockSpec((A_KV_HEADS, QB, LANES), lambda b: (blk, jnp.maximum(b - 1, first_lat), 0)),
                pl.BlockSpec((A_KV_HEADS, QB, LANES), lambda b: (blk, b, 0)),
                pl.BlockSpec((A_KV_HEADS, QB, LANES), lambda b: (blk, jnp.minimum(b + 1, nb - 1), 0)),
                pl.BlockSpec((A_KV_HEADS, CTX, LANES), lambda b: (blk, 0, 0))]

    return pl.pallas_call(
        functools.partial(_attn_a_body, nb=nb),
        grid=(nb,),
        in_specs=[pl.BlockSpec(memory_space=pltpu.SMEM),
                  pl.BlockSpec((HP, QB, LANES), lambda b: (0, b, 0))] + kv_specs(kblk) + kv_specs(vblk),
        out_specs=pl.BlockSpec((HP, QB, LANES), lambda b: (0, b, 0)),
        out_shape=jax.ShapeDtypeStruct((HP, n, LANES), BF16),
        compiler_params=_cparams(40),
        name="attn_window",
    )(sink, qkv3, *([qkv3] * 8))


NA_TAB = 2 * NB_KH
RPB_W = 2 * NB_KW - 1
RPB_H = 2 * NB_KH - 1


def _na_table_body(rpb_ref, o_ref):
    h = pl.program_id(0)
    shape = (GRID_W, LANES)
    c = lax.broadcasted_iota(jnp.int32, shape, 0)
    lane = lax.broadcasted_iota(jnp.int32, shape, 1)
    kc = lane & (GRID_W - 1)
    hi = lane >= GRID_W
    cs = jnp.clip(c - NB_KW // 2, 0, GRID_W - NB_KW)
    colok = jnp.logical_and(kc >= cs, kc < cs + NB_KW)
    diff = kc - c + (NB_KW - 1)
    neg = jnp.full(shape, NEG, F32)
    rows = [neg]
    for d in range(RPB_H):
        acc = neg
        for j in range(RPB_W):
            acc = jnp.where(diff == j, rpb_ref[h * (RPB_H * RPB_W) + d * RPB_W + j] * LOG2E, acc)
        rows.append(jnp.where(colok, acc, NEG))
    rows.append(neg)
    for t in range(NA_TAB):
        o_ref[0, t] = jnp.where(hi, rows[t + 1], rows[t])


def _na_table_call(rpb):
    return pl.pallas_call(
        _na_table_body,
        grid=(HEADS,),
        in_specs=[pl.BlockSpec(memory_space=pltpu.SMEM)],
        out_specs=pl.BlockSpec((1, NA_TAB, GRID_W, LANES), lambda h: (h, 0, 0, 0)),
        out_shape=jax.ShapeDtypeStruct((HEADS, NA_TAB, GRID_W, LANES), F32),
        compiler_params=_cparams(16),
        name="na_bias_table",
    )(rpb.reshape(-1))


def _attn_b_body(tab_ref, q_ref, k0, k1, k2, k3, k4, kx_ref, v0, v1, v2, v3, v4, vx_ref, o_ref, *, nbl):
    b = pl.program_id(0)
    first_lat = CTX // QB
    is_lat = b >= first_lat
    bl = jnp.maximum(b - first_lat, 0)
    ws = jnp.clip(bl - 2, 0, nbl - NA_KBLK)
    n_rows = 2 * nbl
    nloc = NA_KBLK * QB
    rs = [jnp.clip(2 * bl + qr - NB_KH // 2, 0, n_rows - NB_KH) for qr in range(2)]
    tidx = [[jnp.clip(2 * (ws + j) - (2 * bl + qr) + (NB_KH - 1), -1, NA_TAB - 2) + 1 for j in range(NA_KBLK)]
            for qr in range(2)]
    qrow = lax.broadcasted_iota(jnp.int32, (QB, nloc), 0)
    kcol = lax.broadcasted_iota(jnp.int32, (QB, nloc), 1)
    krow = 2 * ws + jnp.right_shift(kcol, 6)
    rs_q = jnp.where(qrow < GRID_W, rs[0], rs[1])
    row_ok = jnp.logical_and(jnp.logical_and(krow >= rs_q, krow < rs_q + NB_KH), is_lat)
    row_ok2 = jnp.concatenate([row_ok, row_ok], axis=0)
    lo = _lane_lo((QB, LANES))
    zero = jnp.zeros((QB, LANES), BF16)
    kl = (k0, k1, k2, k3, k4)
    vl = (v0, v1, v2, v3, v4)

    def scores(hp):
        q2 = q_ref[hp]
        qs = jnp.concatenate([jnp.where(lo, q2, zero), jnp.where(lo, zero, q2)], axis=0)
        keys = jnp.concatenate([r[hp] for r in kl] + [kx_ref[hp]], axis=0)
        return lax.dot_general(qs, keys, (((1,), (1,)), ((), ())), preferred_element_type=F32)

    def probs(hp, s):
        bias = jnp.concatenate(
            [jnp.concatenate([tab_ref[2 * hp + hh, tidx[qr][j]] for j in range(NA_KBLK)], axis=1)
             for hh in range(2) for qr in range(2)], axis=0)
        s_loc = jnp.where(row_ok2, s[:, :nloc] + bias, NEG)
        s_ctx = s[:, nloc:]
        m = jnp.maximum(jnp.max(s_loc, axis=-1, keepdims=True), jnp.max(s_ctx, axis=-1, keepdims=True))
        e_loc = jnp.exp2(s_loc - m)
        e_ctx = jnp.exp2(s_ctx - m)
        denom = jnp.sum(e_loc, axis=-1, keepdims=True) + jnp.sum(e_ctx, axis=-1, keepdims=True)
        return jnp.concatenate([e_loc, e_ctx], axis=1).astype(BF16), denom

    def output(hp, p, denom):
        vals = jnp.concatenate([r[hp] for r in vl] + [vx_ref[hp]], axis=0)
        o = jnp.dot(p, vals, preferred_element_type=F32) / denom
        o_ref[hp] = jnp.where(lo, o[:QB], o[QB:]).astype(BF16)

    _pipeline_pairs(scores, probs, output)


def _attn_b_call(qkv3, table):
    n = qkv3.shape[1]
    nb = n // QB
    first_lat = CTX // QB
    nbl = nb - first_lat

    def win(blk, i):
        return pl.BlockSpec(
            (HP, QB, LANES),
            lambda b: (blk, first_lat + jnp.clip(jnp.maximum(b - first_lat, 0) - 2, 0, nbl - NA_KBLK) + i, 0))

    def kv_specs(blk):
        return [win(blk, i) for i in range(NA_KBLK)] + [pl.BlockSpec((HP, CTX, LANES), lambda b: (blk, 0, 0))]

    return pl.pallas_call(
        functools.partial(_attn_b_body, nbl=nbl),
        grid=(nb,),
        in_specs=[_resident((HEADS, NA_TAB, GRID_W, LANES), lambda b: (0, 0, 0, 0)),
                  pl.BlockSpec((HP, QB, LANES), lambda b: (0, b, 0))] + kv_specs(1) + kv_specs(2),
        out_specs=pl.BlockSpec((HP, QB, LANES), lambda b: (0, b, 0)),
        out_shape=jax.ShapeDtypeStruct((HP, n, LANES), BF16),
        compiler_params=_cparams(52),
        name="attn_neighbourhood",
    )(table, qkv3, *([qkv3] * 12))


def _oproj_body(*refs, mode):
    if mode == "attn":
        o_ref, w_ref, h_ref, mod_ref, out_ref = refs
        z = jnp.concatenate([o_ref[c] for c in range(HP)], axis=1)
    else:
        ug_ref, hs_ref, w_ref, h_ref, mod_ref, out_ref = refs
        z = (jax.nn.gelu(ug_ref[...]) * hs_ref[...]).astype(BF16)
    y = jnp.dot(z, w_ref[...], preferred_element_type=F32)
    out_ref[...] = h_ref[...] + mod_ref[0, 2:3, :] * y


def _oproj_call(mix_in, w, h, mod, *, mode, name):
    n = h.shape[0]
    kdim = w.shape[0]
    if mode == "attn":
        in_specs = [pl.BlockSpec((HP, TM, LANES), lambda i: (0, i, 0))]
        args = [mix_in]
    else:
        u, hs = mix_in
        in_specs = [pl.BlockSpec((TM, D_RNN), lambda i: (i, 0)), pl.BlockSpec((TM, D_RNN), lambda i: (i, 0))]
        args = [u, hs]
    in_specs += [_resident((kdim, D), lambda i: (0, 0)), pl.BlockSpec((TM, D), lambda i: (i, 0)), _mod_spec()]
    return pl.pallas_call(
        functools.partial(_oproj_body, mode=mode),
        grid=(n // TM,),
        in_specs=in_specs,
        out_specs=pl.BlockSpec((TM, D), lambda i: (i, 0)),
        out_shape=jax.ShapeDtypeStruct((n, D), F32),
        compiler_params=_cparams(48),
        name=name,
    )(*args, w, h, mod[:, 0:3])


def _shift_rows(x, s, fill, reverse):
    t = x.shape[0]
    if s % SUBLANES == 0:
        pad = jnp.full((s, x.shape[1]), fill, x.dtype)
        return jnp.concatenate([x[s:], pad] if reverse else [pad, x[:t - s]], axis=0)
    row = lax.broadcasted_iota(jnp.int32, x.shape, 0)
    if reverse:
        return jnp.where(row >= t - s, fill, pltpu.roll(x, t - s, 0))
    return jnp.where(row < s, fill, pltpu.roll(x, s, 0))


def _chunk_scan(a, b, reverse):
    s = 1
    while s < a.shape[0]:
        a_s = _shift_rows(a, s, 1.0, reverse)
        b_s = _shift_rows(b, s, 0.0, reverse)
        b = a * b_s + b
        a = a * a_s
        s *= 2
    return a, b


def _group_scan(a, b, h_in, reverse, scr_ref):
    t, c = a.shape
    g = t // SUBLANES
    a3 = a.reshape(g, SUBLANES, c)
    b3 = b.reshape(g, SUBLANES, c)
    sub = lax.broadcasted_iota(jnp.int32, a3.shape, 1)
    s = 1
    while s < SUBLANES:
        edge = (sub >= SUBLANES - s) if reverse else (sub < s)
        shift = SUBLANES - s if reverse else s
        a_s = jnp.where(edge, 1.0, pltpu.roll(a3, shift, 1))
        b_s = jnp.where(edge, 0.0, pltpu.roll(b3, shift, 1))
        b3 = a3 * b_s + b3
        a3 = a3 * a_s
        s *= 2
    a_loc = a3.reshape(t, c)
    b_loc = b3.reshape(t, c)
    edge_row = 0 if reverse else SUBLANES - 1

    def boundary_rows(i, x):
        for j in range(c // LANES):
            scr_ref[i, j] = x[:, j * LANES:(j + 1) * LANES]
        return jnp.concatenate([scr_ref[i, j, pl.ds(edge_row, g, stride=SUBLANES), :] for j in range(c // LANES)],
                               axis=1)

    ga, gb = _chunk_scan(boundary_rows(0, a_loc), boundary_rows(1, b_loc), reverse)
    leaving = gb + ga * h_in
    entering = _shift_rows(leaving, 1, 0.0, reverse)
    grow = lax.broadcasted_iota(jnp.int32, leaving.shape, 0)
    entering = jnp.where(grow == (g - 1 if reverse else 0), h_in, entering)
    carry = jnp.broadcast_to(entering[:, None, :], (g, SUBLANES, c))
    h = (a3 * carry + b3).reshape(t, c)
    return h, (leaving[0:1, :] if reverse else leaving[g - 1:g, :])


def _lru_body(x_ref, cw_ref, cb_ref, wa_ref, wx_ref, ba_ref, bx_ref, lam_ref, o_ref, scr_ref, xc_ref, *, nchunks):
    n = nchunks * SCAN_TC
    halo = SUBLANES
    cw = cw_ref[...]
    cb = cb_ref[...]
    sp = jax.nn.softplus(-lam_ref[...])

    def conv_chunk(ci):
        r0 = pl.multiple_of(ci * SCAN_TC, SCAN_TC)
        lat_first = CTX // SCAN_TC
        prev_ok = jnp.logical_and(ci != 0, ci != lat_first).astype(F32)
        next_ok = jnp.logical_and(ci != lat_first - 1, ci != nchunks - 1).astype(F32)
        p0 = pl.multiple_of(jnp.maximum(r0 - halo, 0), halo)
        n0 = pl.multiple_of(jnp.minimum(r0 + SCAN_TC, n - halo), halo)
        xe = jnp.concatenate([x_ref[pl.ds(p0, halo), :] * prev_ok,
                              x_ref[pl.ds(r0, SCAN_TC), :],
                              x_ref[pl.ds(n0, halo), :] * next_ok], axis=0)
        y = cb
        for j in range(CONV_W):
            off = halo + j - CONV_LEFT
            y = y + xe[off:off + SCAN_TC, :] * cw[j:j + 1, :]
        return r0, y

    def direction(d, ci, h_in):
        if d == 0:
            r0, xc = conv_chunk(ci)
            xc_ref[pl.ds(r0, SCAN_TC), :] = xc
        else:
            r0 = pl.multiple_of(ci * SCAN_TC, SCAN_TC)
            xc = xc_ref[pl.ds(r0, SCAN_TC), :]
        xb = xc.astype(BF16)
        r = jax.nn.sigmoid(jnp.dot(xb, wa_ref[d, 0], preferred_element_type=F32) + ba_ref[d:d + 1, :])
        ig = jax.nn.sigmoid(jnp.dot(xb, wx_ref[d, 0], preferred_element_type=F32) + bx_ref[d:d + 1, :])
        log_a = -LRU_C * r * sp[d:d + 1, :]
        a = jnp.exp(log_a)
        bb = jnp.sqrt(1.0 - a * a) * (ig * xc)
        h, h_out = _group_scan(a, bb, h_in, d == 1, scr_ref)
        return r0, h, h_out

    def fwd(ci, h_in):
        r0, h, h_out = direction(0, ci, h_in)
        o_ref[pl.ds(r0, SCAN_TC), :] = h
        return h_out

    def bwd(ci, h_in):
        r0, h, h_out = direction(1, ci, h_in)
        o_ref[pl.ds(r0, SCAN_TC), :] += h
        return h_out

    h0 = jnp.zeros((1, RNN_BW), F32)
    lax.fori_loop(0, nchunks, fwd, h0)
    nctx = CTX // SCAN_TC
    hc = lax.fori_loop(0, nctx, lambda k, hh: bwd(nctx - 1 - k, hh), h0)
    lax.fori_loop(0, nchunks - nctx, lambda k, hh: bwd(nchunks - 1 - k, hh), hc)


def _lru_call(u, conv_w, conv_b, wa, wx, ba, bx, lam):
    n = u.shape[0]
    return pl.pallas_call(
        functools.partial(_lru_body, nchunks=n // SCAN_TC),
        grid=(RNN_BLOCKS,),
        in_specs=[_resident((n, RNN_BW), lambda m: (0, RNN_BLOCKS + m)),
                  pl.BlockSpec((CONV_W, RNN_BW), lambda m: (0, m)),
                  pl.BlockSpec((1, RNN_BW), lambda m: (0, m)),
                  pl.BlockSpec((2, 1, RNN_BW, RNN_BW), lambda m: (0, m, 0, 0)),
                  pl.BlockSpec((2, 1, RNN_BW, RNN_BW), lambda m: (0, m, 0, 0)),
                  pl.BlockSpec((2, RNN_BW), lambda m: (0, m)),
                  pl.BlockSpec((2, RNN_BW), lambda m: (0, m)),
                  pl.BlockSpec((2, RNN_BW), lambda m: (0, m))],
        out_specs=_resident((n, RNN_BW), lambda m: (0, m)),
        scratch_shapes=[pltpu.VMEM((2, RNN_BW // LANES, SCAN_TC, LANES), F32), pltpu.VMEM((n, RNN_BW), F32)],
        out_shape=jax.ShapeDtypeStruct((n, D_RNN), F32),
        compiler_params=_cparams(58),
        name="rglru_scan",
    )(u, conv_w, conv_b.reshape(1, D_RNN), wa, wx, ba, bx, lam)


def _pack_words(ref, lo, hi, first_tile):
    rows = lo.shape[0]
    lo = lax.bitcast_convert_type(lo.astype(BF16).astype(F32), jnp.uint32)
    hi = lax.bitcast_convert_type(hi.astype(BF16).astype(F32), jnp.uint32)
    w = hi | (lo >> 16)
    for s in range(lo.shape[1] // LANES):
        ref[pl.ds(first_tile + s, rows, stride=XROW_TILE), :] = w[:, s * LANES:(s + 1) * LANES]


def _pack_rows(ref, x):
    _pack_words(ref, x[:, :D // 2], x[:, D // 2:], 0)


def _unpack_rows(ref, rows, dtype):
    w = jnp.concatenate([ref[pl.ds(s, rows, stride=XROW_TILE), :] for s in range(XROW_TILE)], axis=1)
    lo = lax.bitcast_convert_type(w << 16, F32)
    hi = lax.bitcast_convert_type(w & jnp.uint32(0xFFFF0000), F32)
    return jnp.concatenate([lo, hi], axis=1).astype(dtype)


def _router_body(h_ref, g_ref, mod_ref, wrh_ref, wrl_ref, br_ref, f_ref, r_ref, cnt_ref, base_ref):
    @pl.when(pl.program_id(0) == 0)
    def _():
        base_ref[...] = jnp.zeros(base_ref.shape, F32)

    f = _norm_mod(h_ref[...], g_ref[...], mod_ref[0, 0:1, :], mod_ref[0, 1:2, :])
    _pack_rows(f_ref, f)
    f_hi = f.astype(BF16)
    f_lo = (f - f_hi.astype(F32)).astype(BF16)
    logits = (jnp.dot(jnp.concatenate([f_hi, f_lo], axis=0), wrh_ref[...], preferred_element_type=F32).reshape(2, TM, LANES).sum(axis=0)
              + jnp.dot(f_hi, wrl_ref[...], preferred_element_type=F32) + br_ref[...])
    lane = lax.broadcasted_iota(jnp.int32, logits.shape, 1)
    ninf = -jnp.inf
    is_g = lane < N_GROUPS
    gl = jnp.where(is_g, logits, ninf)
    gmax = jnp.max(gl, axis=-1, keepdims=True)
    gsel = jnp.min(jnp.where(gl == gmax, lane, LANES), axis=-1, keepdims=True)
    gsum = jnp.sum(jnp.where(is_g, jnp.exp(gl - gmax), 0.0), axis=-1, keepdims=True)
    g_w = 1.0 / gsum
    e_lo = N_GROUPS + EXP_PER_GROUP * gsel
    in_grp = jnp.logical_and(lane >= e_lo, lane < e_lo + EXP_PER_GROUP)
    el = jnp.where(in_grp, logits, ninf)
    m1 = jnp.max(el, axis=-1, keepdims=True)
    i1 = jnp.min(jnp.where(el == m1, lane, LANES), axis=-1, keepdims=True)
    el2 = jnp.where(lane == i1, ninf, el)
    m2 = jnp.max(el2, axis=-1, keepdims=True)
    i2 = jnp.min(jnp.where(el2 == m2, lane, LANES), axis=-1, keepdims=True)
    t = jnp.exp(m2 - m1)
    w1 = g_w / (1.0 + t)
    w2 = g_w * t / (1.0 + t)
    oh1 = lane == i1
    oh2 = lane == i2
    rr = lax.broadcasted_iota(jnp.int32, (TM, TM), 0)
    cc = lax.broadcasted_iota(jnp.int32, (TM, TM), 1)
    tri = (cc < rr).astype(BF16)
    pre1 = jnp.dot(tri, oh1.astype(BF16), preferred_element_type=F32)
    pre2 = jnp.dot(tri, oh2.astype(BF16), preferred_element_type=F32)
    base = base_ref[...]
    cnt1 = jnp.sum(oh1.astype(F32), axis=0, keepdims=True)
    cnt2 = jnp.sum(oh2.astype(F32), axis=0, keepdims=True)
    rank1 = jnp.sum(jnp.where(oh1, pre1 + base, 0.0), axis=-1, keepdims=True)
    rank2 = jnp.sum(jnp.where(oh2, pre2 + (base + cnt1), 0.0), axis=-1, keepdims=True)
    total = base + cnt1 + cnt2
    base_ref[...] = total
    cnt_ref[...] = total
    cols = [(i1 - N_GROUPS).astype(F32), (i2 - N_GROUPS).astype(F32), w1, w2, rank1, rank2]
    out = jnp.zeros(logits.shape, F32)
    for k, v in enumerate(cols):
        out = jnp.where(lane == k, v, out)
    r_ref[...] = out


def _router_call(h, g, mod, wr, br):
    n = h.shape[0]
    wr_hi = wr.astype(BF16)
    wr_lo = (wr - wr_hi.astype(F32)).astype(BF16)
    return pl.pallas_call(
        _router_body,
        grid=(n // TM,),
        in_specs=[pl.BlockSpec((TM, D), lambda i: (i, 0)),
                  pl.BlockSpec((1, D), lambda i: (0, 0)),
                  _mod_spec(),
                  pl.BlockSpec((D, LANES), lambda i: (0, 0)),
                  pl.BlockSpec((D, LANES), lambda i: (0, 0)),
                  pl.BlockSpec((1, LANES), lambda i: (0, 0))],
        out_specs=[pl.BlockSpec((TM * XROW_TILE, LANES), lambda i: (i, 0)),
                   pl.BlockSpec((TM, LANES), lambda i: (i, 0)),
                   pl.BlockSpec((1, LANES), lambda i: (0, 0))],
        out_shape=[jax.ShapeDtypeStruct((n * XROW_TILE, LANES), jnp.uint32),
                   jax.ShapeDtypeStruct((n, LANES), F32),
                   jax.ShapeDtypeStruct((1, LANES), F32)],
        scratch_shapes=[pltpu.VMEM((1, LANES), F32)],
        compiler_params=_cparams(32),
        name="moe_router",
    )(h, g.reshape(1, D), mod[:, 3:6], wr_hi, wr_lo, br)


def _dispatch_body(dest_ref, pad_lo_ref, pad_n_ref, f_ref, xg_ref, z_ref, sem, blk_sem):
    i = pl.program_id(0)
    base = i * TM

    def issue(j, carry):
        for k in range(TOP_K):
            d = dest_ref[(base + j) * TOP_K + k]
            pltpu.make_async_copy(f_ref.at[pl.ds(j * XROW_TILE, XROW_TILE)],
                                  xg_ref.at[pl.ds(d * XROW_TILE, XROW_TILE)], sem).start(priority=k)
        return carry

    z_row = z_ref.at[pl.ds(0, XROW_TILE)]

    def wait_one(carry):
        pltpu.make_async_copy(z_row, xg_ref.at[pl.ds(0, XROW_TILE)], sem).wait()
        return carry

    lax.fori_loop(0, TM, issue, 0, unroll=DMA_UNROLL)

    @pl.when(i == 0)
    def _():
        z_ref[...] = jnp.zeros(z_ref.shape, jnp.uint32)

        def fill(e, carry):
            lo = pad_lo_ref[e]
            n_fill = pad_n_ref[e]

            def batch(bi, c):
                s0 = bi * MOE_MB
                cnt = jnp.minimum(n_fill - s0, MOE_MB)

                def one(s, c2):
                    pltpu.make_async_copy(z_row, xg_ref.at[pl.ds((lo + s0 + s) * XROW_TILE, XROW_TILE)], sem).start()
                    return c2

                lax.fori_loop(0, cnt, one, 0)
                lax.fori_loop(0, cnt, lambda s, c2: wait_one(c2), 0)
                return c

            lax.fori_loop(0, (n_fill + (MOE_MB - 1)) // MOE_MB, batch, 0)
            return carry

        lax.fori_loop(0, N_EXPERTS, fill, 0)

        def block_copy(bi):
            row0 = (pad_lo_ref[N_EXPERTS] + bi * MOE_MB) * XROW_TILE
            return pltpu.make_async_copy(z_ref, xg_ref.at[pl.ds(row0, MOE_MB * XROW_TILE)], blk_sem)

        n_tail = pad_n_ref[N_EXPERTS] // MOE_MB
        lax.fori_loop(0, n_tail, lambda bi, c: (block_copy(bi).start(), c)[1], 0)
        lax.fori_loop(0, n_tail, lambda bi, c: (block_copy(bi).wait(), c)[1], 0)

    lax.fori_loop(0, TM * TOP_K, lambda j, c: wait_one(c), 0, unroll=DMA_UNROLL)


def _dispatch_call(dest, pad_lo, pad_n, f2, p):
    n = f2.shape[0] // XROW_TILE
    return pl.pallas_call(
        _dispatch_body,
        grid_spec=pltpu.PrefetchScalarGridSpec(
            num_scalar_prefetch=3,
            grid=(n // TM,),
            in_specs=[pl.BlockSpec((TM * XROW_TILE, LANES), lambda i, d, lo, nn: (i, 0))],
            out_specs=pl.BlockSpec(memory_space=pl.ANY),
            scratch_shapes=[pltpu.VMEM((MOE_MB * XROW_TILE, LANES), jnp.uint32), pltpu.SemaphoreType.DMA(()),
                            pltpu.SemaphoreType.DMA(())]),
        out_shape=jax.ShapeDtypeStruct((p * XROW_TILE, LANES), jnp.uint32),
        compiler_params=_cparams(16),
        name="moe_dispatch",
    )(dest, pad_lo, pad_n, f2)


def _expert_weights(b, be_ref, ne_ref, nu_ref, w_hbm, wf_ref, wb_ref, sem, slot_ref, layer):
    e = be_ref[b]
    first = jnp.logical_or(b == 0, e != be_ref[jnp.maximum(b - 1, 0)])

    def fetch(expert, slot):
        return pltpu.make_async_copy(w_hbm.at[layer, expert], wf_ref.at[slot], sem.at[slot])

    @pl.when(b == 0)
    def _():
        slot_ref[0] = 0
        fetch(e, 0).start()

    @pl.when(jnp.logical_and(first, b < nu_ref[0]))
    def _():
        s = slot_ref[0]
        fetch(e, s).wait()
        wb_ref[...] = wf_ref[s].astype(BF16)
        ne = ne_ref[b]

        @pl.when(ne != e)
        def _():
            fetch(ne, 1 - s).start()

        slot_ref[0] = 1 - s


def _ffn_up_body(be_ref, ne_ref, nu_ref, x_ref, w_hbm, a_ref, wf_ref, wb_ref, sem, slot_ref, *, layer):
    b = pl.program_id(0)
    _expert_weights(b, be_ref, ne_ref, nu_ref, w_hbm, wf_ref, wb_ref, sem, slot_ref, layer)

    @pl.when(b < nu_ref[0])
    def _():
        x = _unpack_rows(x_ref, MOE_MB, BF16)
        cw = FFN_CHUNK
        n_chunks = D_EXPERT // cw

        def gate_up(c):
            return (jnp.dot(x, wb_ref[:, c * cw:(c + 1) * cw], preferred_element_type=F32),
                    jnp.dot(x, wb_ref[:, D_EXPERT + c * cw:D_EXPERT + (c + 1) * cw], preferred_element_type=F32))

        nxt = gate_up(0)
        for c in range(n_chunks):
            g, u = nxt
            if c + 1 < n_chunks:
                nxt = gate_up(c + 1)
            a_ref[:, c * cw:(c + 1) * cw] = (g * jax.nn.sigmoid(g) * u).astype(BF16)

    @pl.when(b >= nu_ref[0])
    def _():
        a_ref[...] = jnp.zeros(a_ref.shape, BF16)


def _expert_scratch(rows, cols):
    return [pltpu.VMEM((2, rows, cols), F32), pltpu.VMEM((rows, cols), BF16),
            pltpu.SemaphoreType.DMA((2,)), pltpu.SMEM((1,), jnp.int32)]


def _ffn_up_call(block_e, next_e, n_used, xg, w_gu, layer):
    p = xg.shape[0] // XROW_TILE
    return pl.pallas_call(
        functools.partial(_ffn_up_body, layer=layer),
        grid_spec=pltpu.PrefetchScalarGridSpec(
            num_scalar_prefetch=3,
            grid=(p // MOE_MB,),
            in_specs=[pl.BlockSpec((MOE_MB * XROW_TILE, LANES),
                                   lambda b, be, ne, nu: (jnp.minimum(b, nu[0] - 1), 0)),
                      pl.BlockSpec(memory_space=pl.ANY)],
            out_specs=pl.BlockSpec((MOE_MB, D_EXPERT), lambda b, be, ne, nu: (b, 0)),
            scratch_shapes=_expert_scratch(D, 2 * D_EXPERT)),
        out_shape=jax.ShapeDtypeStruct((p, D_EXPERT), BF16),
        compiler_params=_cparams(48),
        name="moe_ffn_up",
    )(block_e, next_e, n_used, xg, w_gu)


def _ffn_down_body(be_ref, ne_ref, nu_ref, a_ref, w_hbm, y_ref, wf_ref, wb_ref, sem, slot_ref, *, layer):
    b = pl.program_id(0)
    _expert_weights(b, be_ref, ne_ref, nu_ref, w_hbm, wf_ref, wb_ref, sem, slot_ref, layer)

    @pl.when(b < nu_ref[0])
    def _():
        a = a_ref[...]
        cw = FFN_CHUNK
        n_chunks = D // 2 // cw

        def halves(c):
            return (jnp.dot(a, wb_ref[:, c * cw:(c + 1) * cw], preferred_element_type=F32),
                    jnp.dot(a, wb_ref[:, D // 2 + c * cw:D // 2 + (c + 1) * cw], preferred_element_type=F32))

        nxt = halves(0)
        for c in range(n_chunks):
            lo, hi = nxt
            if c + 1 < n_chunks:
                nxt = halves(c + 1)
            _pack_words(y_ref, lo, hi, c * (cw // LANES))

    @pl.when(b >= nu_ref[0])
    def _():
        y_ref[...] = jnp.zeros(y_ref.shape, jnp.uint32)


def _ffn_down_call(block_e, next_e, n_used, act, w_down, layer):
    p = act.shape[0]
    return pl.pallas_call(
        functools.partial(_ffn_down_body, layer=layer),
        grid_spec=pltpu.PrefetchScalarGridSpec(
            num_scalar_prefetch=3,
            grid=(p // MOE_MB,),
            in_specs=[pl.BlockSpec((MOE_MB, D_EXPERT), lambda b, be, ne, nu: (b, 0)),
                      pl.BlockSpec(memory_space=pl.ANY)],
            out_specs=pl.BlockSpec((MOE_MB * XROW_TILE, LANES), lambda b, be, ne, nu: (b, 0)),
            scratch_shapes=_expert_scratch(D_EXPERT, D)),
        out_shape=jax.ShapeDtypeStruct((p * XROW_TILE, LANES), jnp.uint32),
        compiler_params=_cparams(32),
        name="moe_ffn_down",
    )(block_e, next_e, n_used, act, w_down)


def _combine_body(dest_ref, h_ref, r_ref, mod_ref, g_ref, yb_ref, o_ref, buf_ref, sem, *, final):
    i = pl.program_id(0)
    slot = i % 2

    def gather(tile, dst_slot):
        base = tile * TM

        def issue(j, carry):
            for k in range(TOP_K):
                d = dest_ref[(base + j) * TOP_K + k]
                pltpu.make_async_copy(yb_ref.at[pl.ds(d * XROW_TILE, XROW_TILE)],
                                      buf_ref.at[dst_slot, k, pl.ds(j * XROW_TILE, XROW_TILE)],
                                      sem.at[dst_slot]).start(priority=k)
            return carry

        lax.fori_loop(0, TM, issue, 0, unroll=DMA_UNROLL)

    @pl.when(i == 0)
    def _():
        gather(0, 0)

    @pl.when(i + 1 < pl.num_programs(0))
    def _():
        gather(i + 1, 1 - slot)

    def drain(j, carry):
        pltpu.make_async_copy(yb_ref.at[pl.ds(0, XROW_TILE)], buf_ref.at[slot, 0, pl.ds(0, XROW_TILE)],
                              sem.at[slot]).wait()
        return carry

    lax.fori_loop(0, TM * TOP_K, drain, 0, unroll=DMA_UNROLL)
    r = r_ref[...]
    y = sum(r[:, TOP_K + k:TOP_K + k + 1] * _unpack_rows(buf_ref.at[slot, k], TM, F32) for k in range(TOP_K))
    h_new = h_ref[...] + mod_ref[0, 2:3, :] * y
    if final:
        ms = jnp.mean(h_new * h_new, axis=-1, keepdims=True)
        h_new = h_new * lax.rsqrt(ms + EPS) * g_ref[...]
    o_ref[...] = h_new


def _combine_call(dest, h, route, mod, yb, final_g=None):
    n = h.shape[0]
    final = final_g is not None
    first = CTX // TM
    out_rows = n - CTX if final else n
    out_map = (lambda i, d: (jnp.maximum(i - first, 0), 0)) if final else (lambda i, d: (i, 0))
    g = final_g if final else jnp.ones((D,), F32)
    return pl.pallas_call(
        functools.partial(_combine_body, final=final),
        grid_spec=pltpu.PrefetchScalarGridSpec(
            num_scalar_prefetch=1,
            grid=(n // TM,),
            in_specs=[pl.BlockSpec((TM, D), lambda i, d: (i, 0)),
                      pl.BlockSpec((TM, LANES), lambda i, d: (i, 0)),
                      pl.BlockSpec((1, 3, D), lambda i, d: (jnp.minimum(i, 1), 0, 0)),
                      pl.BlockSpec((1, D), lambda i, d: (0, 0)),
                      pl.BlockSpec(memory_space=pl.ANY)],
            out_specs=pl.BlockSpec((TM, D), out_map),
            scratch_shapes=[pltpu.VMEM((2, TOP_K, TM * XROW_TILE, LANES), jnp.uint32),
                            pltpu.SemaphoreType.DMA((2,))]),
        out_shape=jax.ShapeDtypeStruct((out_rows, D), F32),
        compiler_params=_cparams(32),
        name="moe_combine",
    )(dest, h, route, mod[:, 3:6], g.reshape(1, D), yb)


def _dispatch_plan(route, cnt, n):
    experts = jnp.arange(N_EXPERTS, dtype=jnp.int32)
    counts = cnt[0, N_GROUPS:N_GROUPS + N_EXPERTS].astype(jnp.int32)
    padded = (counts + MOE_MB - 1) // MOE_MB * MOE_MB
    end_pad = jnp.cumsum(padded)
    start_pad = end_pad - padded
    n_blocks = -(-(n * TOP_K + N_EXPERTS * (MOE_MB - 1)) // MOE_MB)
    p = n_blocks * MOE_MB
    blk_start = jnp.arange(n_blocks, dtype=jnp.int32) * MOE_MB
    block_e = jnp.minimum(jnp.sum(end_pad[None, :] <= blk_start[:, None], axis=1), N_EXPERTS - 1).astype(jnp.int32)
    n_used = (end_pad[-1:] // MOE_MB).astype(jnp.int32)
    group_end = end_pad[block_e] // MOE_MB
    next_e = jnp.where(group_end < n_used[0], block_e[jnp.minimum(group_end, n_blocks - 1)], block_e)
    e_idx = route[:, 0:TOP_K].astype(jnp.int32)
    rank = route[:, 2 * TOP_K:3 * TOP_K].astype(jnp.int32)
    start = jnp.sum(jnp.where(e_idx[..., None] == experts, start_pad, 0), axis=-1)
    dest = (start + rank).reshape(n * TOP_K)
    pad_lo = jnp.concatenate([start_pad + counts, end_pad[-1:]]).astype(jnp.int32)
    pad_n = jnp.concatenate([padded - counts, p - end_pad[-1:]]).astype(jnp.int32)
    return p, block_e, next_e, n_used, dest, pad_lo, pad_n


def _moe_layer(h, g, mod, wr, br, w_gu, w_down, layer, final_g=None):
    n = h.shape[0]
    f2, route, cnt = _router_call(h, g, mod, wr, br)
    p, block_e, next_e, n_used, dest, pad_lo, pad_n = _dispatch_plan(route, cnt, n)
    xg = _dispatch_call(dest, pad_lo, pad_n, f2, p)
    act = _ffn_up_call(block_e, next_e, n_used, xg, w_gu, layer)
    yb = _ffn_down_call(block_e, next_e, n_used, act, w_down, layer)
    return _combine_call(dest, h, route, mod, yb, final_g)


def _rope_tables(l):
    quarter = HEAD_DIM // 4
    inv = ROPE_BASE ** (-jnp.arange(quarter, dtype=F32) / quarter)
    n_rows = l // GRID_W
    ang_r = jnp.arange(n_rows, dtype=F32)[:, None] * inv
    ang_c = jnp.arange(GRID_W, dtype=F32)[:, None] * inv
    by_row = lambda t: jnp.broadcast_to(t[:, None, :], (n_rows, GRID_W, quarter)).reshape(l, quarter)
    by_col = lambda t: jnp.broadcast_to(t[None, :, :], (n_rows, GRID_W, quarter)).reshape(l, quarter)
    cr, sr, cc, sc = by_row(jnp.cos(ang_r)), by_row(jnp.sin(ang_r)), by_col(jnp.cos(ang_c)), by_col(jnp.sin(ang_c))
    cos = jnp.concatenate([cr, cr, cc, cc], axis=1)
    sin = jnp.concatenate([-sr, sr, -sc, sc], axis=1)
    cos = jnp.concatenate([jnp.ones((CTX, HEAD_DIM), F32), cos], axis=0)
    sin = jnp.concatenate([jnp.zeros((CTX, HEAD_DIM), F32), sin], axis=0)
    return jnp.tile(cos, (1, 2)), jnp.tile(sin, (1, 2))


def _attn_a_weights(w_qkv):
    nq = HEADS * HEAD_DIM
    nkv = A_KV_HEADS * HEAD_DIM
    wq = w_qkv[:, :nq]
    wk = w_qkv[:, nq:nq + nkv].reshape(D, A_KV_HEADS, 1, HEAD_DIM)
    wv = w_qkv[:, nq + nkv:].reshape(D, A_KV_HEADS, 1, HEAD_DIM)
    dup = lambda w: jnp.broadcast_to(w, (D, A_KV_HEADS, 2, HEAD_DIM)).reshape(D, 2 * nkv)
    return jnp.concatenate([wq, dup(wk), dup(wv)], axis=1).astype(BF16)


def kernel(x, c, ctx, c_ctx, ada_w, ada_b, norm_mix_g, norm_ffn_g, router_group_w, router_group_b,
           router_expert_w, router_expert_b, moe_w_gu, moe_w_down, attn_w_qkv, attn_w_o, attn_sink,
           na_w_qkv, na_w_o, na_rpb, rnn_w_in, rnn_conv_w, rnn_conv_b, rnn_wa, rnn_ba, rnn_wx, rnn_bx,
           rnn_lam, rnn_w_out, final_norm_g):
    batch, l, _ = x.shape
    assert batch == 1 and ctx.shape[1] == CTX and l % (QB * 2) == 0 and l // QB >= NA_KBLK
    h = jnp.concatenate([ctx[0], x[0]], axis=0)
    c2 = jnp.stack([c_ctx, c[0]], axis=1)
    mods = _ada_call(c2, ada_w, ada_b).reshape(DEPTH, 2, 6, D)
    rope = _rope_tables(l)
    pad_r = LANES - N_GROUPS - N_EXPERTS
    for i in range(DEPTH):
        kind, j = i % 3, i // 3
        mod = mods[i]
        if kind == 0:
            w = _attn_a_weights(attn_w_qkv[j])
            nq = HEADS * HEAD_DIM
            qkv3 = _proj_call(h, norm_mix_g[i], mod, w, mode="rope", rope=rope,
                              n_rope=nq + 2 * A_KV_HEADS * HEAD_DIM, n_q=nq, name="proj_window")
            o3 = _attn_a_call(qkv3, attn_sink[j])
            h = _oproj_call(o3, attn_w_o[j].astype(BF16), h, mod, mode="attn", name="oproj_window")
        elif kind == 1:
            qkv3 = _proj_call(h, norm_mix_g[i], mod, na_w_qkv[j].astype(BF16), mode="cols",
                              n_q=HEADS * HEAD_DIM, name="proj_neighbourhood")
            o3 = _attn_b_call(qkv3, _na_table_call(na_rpb[j]))
            h = _oproj_call(o3, na_w_o[j].astype(BF16), h, mod, mode="attn", name="oproj_neighbourhood")
        else:
            u = _proj_call(h, norm_mix_g[i], mod, rnn_w_in[j].astype(BF16), mode="plain", name="proj_rglru")
            hs = _lru_call(u, rnn_conv_w[j], rnn_conv_b[j], rnn_wa[j].astype(BF16), rnn_wx[j].astype(BF16),
                           rnn_ba[j], rnn_bx[j], rnn_lam[j])
            h = _oproj_call((u, hs), rnn_w_out[j].astype(BF16), h, mod, mode="rnn", name="oproj_rglru")
        wr = jnp.concatenate([router_group_w[i], router_expert_w[i], jnp.zeros((D, pad_r), F32)], axis=1)
        br = jnp.concatenate([router_group_b[i], router_expert_b[i], jnp.zeros((pad_r,), F32)]).reshape(1, LANES)
        h = _moe_layer(h, norm_ffn_g[i], mod, wr, br, moe_w_gu, moe_w_down, i,
                       final_norm_g if i == DEPTH - 1 else None)
    return h[None]
```

```python
import functools

import jax
import jax.numpy as jnp
from jax import lax
from jax.experimental import pallas as pl
from jax.experimental.pallas import tpu as pltpu

F32 = jnp.float32
BF16 = jnp.bfloat16

D = 2048
DEPTH = 4
GRID_W = 64
CTX = 256
HEADS = 32
HEAD_DIM = 64
A_KV_HEADS = 4
A_GROUP = HEADS // A_KV_HEADS
WINDOW = 128
NB_KH = 8
NB_KW = 16
D_RNN = 2560
RNN_BLOCKS = 10
RNN_BW = D_RNN // RNN_BLOCKS
CONV_W = 4
CONV_LEFT = 2
LRU_C = 8.0
N_GROUPS = 4
EXP_PER_GROUP = 8
N_EXPERTS = N_GROUPS * EXP_PER_GROUP
TOP_K = 2
D_EXPERT = 768
ROPE_BASE = 10000.0
EPS = 1e-6
NEG = -1e30

LANES = 128
SUBLANES = 8
MIB = 1024 * 1024

TM = 256
QB = 128
HP = HEADS // 2
MOE_MB = 256
ROW_TILE = D // LANES
XROW_TILE = ROW_TILE // 2
DMA_UNROLL = 8
FFN_CHUNK = 256
SCAN_TC = 256
NA_KBLK = 5
SQRT_SCALE = HEAD_DIM ** -0.5
LOG2E = 1.4426950408889634


def _cparams(vmem_mib, sem=("arbitrary",)):
    return pltpu.CompilerParams(dimension_semantics=sem, vmem_limit_bytes=int(vmem_mib * MIB))


def _resident(block_shape, index_map):
    return pl.BlockSpec(block_shape, index_map, pipeline_mode=pl.Buffered(1))


def _mod_spec():
    return pl.BlockSpec((1, 3, D), lambda i: (jnp.minimum(i, 1), 0, 0))


def _norm_mod(x, g, shift, scale):
    ms = jnp.mean(x * x, axis=-1, keepdims=True)
    y = x * lax.rsqrt(ms + EPS) * g
    return y * (1.0 + scale) + shift


ADA_TN = 1024


def _ada_body(c_ref, w_ref, b_ref, o_ref):
    c = c_ref[...]
    cs = c * jax.nn.sigmoid(c)
    for r in range(2):
        cb = jnp.broadcast_to(cs[:, r:r + 1], (D, LANES))
        outs = []
        for j in range(ADA_TN // LANES):
            w = w_ref[0, :, j * LANES:(j + 1) * LANES]
            p = (w * cb).reshape(D // SUBLANES, SUBLANES, LANES).sum(axis=0)
            outs.append(p.sum(axis=0, keepdims=True))
        o_ref[0, r:r + 1, :] = jnp.concatenate(outs, axis=1) + b_ref[0]


def _ada_call(c2, ada_w, ada_b):
    return pl.pallas_call(
        _ada_body,
        grid=(DEPTH, 6 * D // ADA_TN),
        in_specs=[pl.BlockSpec((D, 2), lambda l, j: (0, 0)),
                  pl.BlockSpec((1, D, ADA_TN), lambda l, j: (l, 0, j)),
                  pl.BlockSpec((1, 1, ADA_TN), lambda l, j: (l, 0, j))],
        out_specs=pl.BlockSpec((1, 2, ADA_TN), lambda l, j: (l, 0, j)),
        out_shape=jax.ShapeDtypeStruct((DEPTH, 2, 6 * D), F32),
        compiler_params=_cparams(32, ("arbitrary", "arbitrary")),
        name="ada_mod",
    )(c2, ada_w, ada_b.reshape(DEPTH, 1, 6 * D))


PROJ_CH = 512


def _rope_piece(piece, cos, sin):
    lane = lax.broadcasted_iota(jnp.int32, piece.shape, 1)
    first = (lane & 16) == 0
    partner = jnp.where(first, pltpu.roll(piece, LANES - 16, 1), pltpu.roll(piece, 16, 1))
    return piece * cos + partner * sin


def _proj_body(*refs, mode, n_rope, n_q):
    if mode == "rope":
        x_ref, g_ref, mod_ref, w_ref, cos_ref, sin_ref, o_ref = refs
    else:
        x_ref, g_ref, mod_ref, w_ref, o_ref = refs
    a = _norm_mod(x_ref[...], g_ref[...], mod_ref[0, 0:1, :], mod_ref[0, 1:2, :]).astype(BF16)
    nout = w_ref.shape[1]
    for c in range(nout // PROJ_CH):
        acc = jnp.dot(a, w_ref[:, c * PROJ_CH:(c + 1) * PROJ_CH], preferred_element_type=F32)
        if mode == "plain":
            o_ref[:, c * PROJ_CH:(c + 1) * PROJ_CH] = acc
            continue
        for k in range(PROJ_CH // LANES):
            col0 = c * PROJ_CH + k * LANES
            piece = acc[:, k * LANES:(k + 1) * LANES]
            if col0 < n_rope:
                piece = _rope_piece(piece, cos_ref[...], sin_ref[...])
            if col0 < n_q:
                piece = piece * (SQRT_SCALE * LOG2E)
            o_ref[col0 // LANES] = piece.astype(BF16)


def _proj_call(h, g, mod, w, *, mode, rope=None, n_rope=0, n_q=0, name):
    n = h.shape[0]
    nout = w.shape[1]
    in_specs = [pl.BlockSpec((TM, D), lambda i: (i, 0)),
                pl.BlockSpec((1, D), lambda i: (0, 0)),
                _mod_spec(),
                _resident((D, nout), lambda i: (0, 0))]
    args = [h, g.reshape(1, D), mod[:, 0:3], w]
    if mode == "rope":
        in_specs += [pl.BlockSpec((TM, LANES), lambda i: (i, 0))] * 2
        args += list(rope)
    if mode == "plain":
        out_spec = pl.BlockSpec((TM, nout), lambda i: (i, 0))
        out_shape = jax.ShapeDtypeStruct((n, nout), F32)
        out_bytes = TM * nout * 4
    else:
        out_spec = pl.BlockSpec((nout // LANES, TM, LANES), lambda i: (0, i, 0))
        out_shape = jax.ShapeDtypeStruct((nout // LANES, n, LANES), BF16)
        out_bytes = TM * nout * 2
    vmem = (D * nout * 2 + 2 * TM * D * 4 + 2 * out_bytes) / MIB + 12
    return pl.pallas_call(
        functools.partial(_proj_body, mode=mode, n_rope=n_rope, n_q=n_q),
        grid=(n // TM,),
        in_specs=in_specs,
        out_specs=out_spec,
        out_shape=out_shape,
        compiler_params=_cparams(vmem),
        name=name,
    )(*args)


def _lane_lo(shape):
    return lax.broadcasted_iota(jnp.int32, shape, len(shape) - 1) < HEAD_DIM


def _pipeline_pairs(scores, probs, output):
    s_next = scores(0)
    pending = None
    for hp in range(HP):
        s_cur = s_next
        if hp + 1 < HP:
            s_next = scores(hp + 1)
        cur = probs(hp, s_cur)
        if pending is not None:
            output(hp - 1, *pending)
        pending = cur
    output(HP - 1, *pending)


def _attn_a_body(sink_ref, q_ref, kp_ref, kc_ref, kn_ref, kx_ref, vp_ref, vc_ref, vn_ref, vx_ref, o_ref, *, nb):
    b = pl.program_id(0)
    first_lat = CTX // QB
    is_lat = b >= first_lat
    prev_ok = b >= first_lat + 1
    next_ok = jnp.logical_and(is_lat, b <= nb - 2)
    nloc = 3 * QB
    nkeys = nloc + CTX
    qi = lax.broadcasted_iota(jnp.int32, (QB, nkeys), 0)
    kj = lax.broadcasted_iota(jnp.int32, (QB, nkeys), 1)
    rel = qi + WINDOW - kj
    band = jnp.abs(rel) <= WINDOW
    seg_ok = jnp.where(kj < QB, prev_ok.astype(jnp.int32),
                       jnp.where(kj < 2 * QB, is_lat.astype(jnp.int32), next_ok.astype(jnp.int32)))
    ok = jnp.logical_or(kj >= nloc, jnp.logical_and(band, seg_ok > 0))
    mask_add = jnp.where(ok, 0.0, NEG)
    lo = _lane_lo((QB, LANES))
    zero = jnp.zeros((QB, LANES), BF16)
    def scores(hp):
        kvh = hp // (A_GROUP // 2)
        keys = jnp.concatenate([kp_ref[kvh], kc_ref[kvh], kn_ref[kvh], kx_ref[kvh]], axis=0)
        q2 = q_ref[hp]
        qs = jnp.concatenate([jnp.where(lo, q2, zero), jnp.where(lo, zero, q2)], axis=0)
        return lax.dot_general(qs, keys, (((1,), (1,)), ((), ())), preferred_element_type=F32)

    def probs(hp, s):
        sink = jnp.concatenate([jnp.full((1, QB, 1), sink_ref[2 * hp + hh] * LOG2E, F32) for hh in range(2)], axis=0)
        s3 = s.reshape(2, QB, nkeys) + mask_add[None]
        m = jnp.maximum(jnp.max(s3, axis=-1, keepdims=True), sink)
        e = jnp.exp2(s3 - m)
        denom = jnp.sum(e, axis=-1, keepdims=True) + jnp.exp2(sink - m)
        return e.astype(BF16).reshape(2 * QB, nkeys), denom

    def output(hp, e, denom):
        kvh = hp // (A_GROUP // 2)
        vals = jnp.concatenate([vp_ref[kvh], vc_ref[kvh], vn_ref[kvh], vx_ref[kvh]], axis=0)
        o = jnp.dot(e, vals, preferred_element_type=F32).reshape(2, QB, LANES) / denom
        o_ref[hp] = jnp.where(lo, o[0], o[1]).astype(BF16)

    _pipeline_pairs(scores, probs, output)


def _attn_a_call(qkv3, sink):
    n = qkv3.shape[1]
    nb = n // QB
    first_lat = CTX // QB
    kblk, vblk = HP // A_KV_HEADS, HP // A_KV_HEADS + 1

    def kv_specs(blk):
        return [pl.BlockSpec((A_KV_HEADS, QB, LANES), lambda b: (blk, jnp.maximum(b - 1, first_lat), 0)),
                pl.BlockSpec((A_KV_HEADS, QB, LANES), lambda b: (blk, b, 0)),
                pl.BlockSpec((A_KV_HEADS, QB, LANES), lambda b: (blk, jnp.minimum(b + 1, nb - 1), 0)),
                pl.BlockSpec((A_KV_HEADS, CTX, LANES), lambda b: (blk, 0, 0))]

    return pl.pallas_call(
        functools.partial(_attn_a_body, nb=nb),
        grid=(nb,),
        in_specs=[pl.BlockSpec(memory_space=pltpu.SMEM),
                  pl.BlockSpec((HP, QB, LANES), lambda b: (0, b, 0))] + kv_specs(kblk) + kv_specs(vblk),
        out_specs=pl.BlockSpec((HP, QB, LANES), lambda b: (0, b, 0)),
        out_shape=jax.ShapeDtypeStruct((HP, n, LANES), BF16),
        compiler_params=_cparams(40),
        name="attn_window",
    )(sink, qkv3, *([qkv3] * 8))


NA_TAB = 2 * NB_KH
RPB_W = 2 * NB_KW - 1
RPB_H = 2 * NB_KH - 1


def _na_table_body(rpb_ref, o_ref):
    h = pl.program_id(0)
    shape = (GRID_W, LANES)
    c = lax.broadcasted_iota(jnp.int32, shape, 0)
    lane = lax.broadcasted_iota(jnp.int32, shape, 1)
    kc = lane & (GRID_W - 1)
    hi = lane >= GRID_W
    cs = jnp.clip(c - NB_KW // 2, 0, GRID_W - NB_KW)
    colok = jnp.logical_and(kc >= cs, kc < cs + NB_KW)
    diff = kc - c + (NB_KW - 1)
    neg = jnp.full(shape, NEG, F32)
    rows = [neg]
    for d in range(RPB_H):
        acc = neg
        for j in range(RPB_W):
            acc = jnp.where(diff == j, rpb_ref[h * (RPB_H * RPB_W) + d * RPB_W + j] * LOG2E, acc)
        rows.append(jnp.where(colok, acc, NEG))
    rows.append(neg)
    for t in range(NA_TAB):
        o_ref[0, t] = jnp.where(hi, rows[t + 1], rows[t])


def _na_table_call(rpb):
    return pl.pallas_call(
        _na_table_body,
        grid=(HEADS,),
        in_specs=[pl.BlockSpec(memory_space=pltpu.SMEM)],
        out_specs=pl.BlockSpec((1, NA_TAB, GRID_W, LANES), lambda h: (h, 0, 0, 0)),
        out_shape=jax.ShapeDtypeStruct((HEADS, NA_TAB, GRID_W, LANES), F32),
        compiler_params=_cparams(16),
        name="na_bias_table",
    )(rpb.reshape(-1))


def _attn_b_body(tab_ref, q_ref, k0, k1, k2, k3, k4, kx_ref, v0, v1, v2, v3, v4, vx_ref, o_ref, *, nbl):
    b = pl.program_id(0)
    first_lat = CTX // QB
    is_lat = b >= first_lat
    bl = jnp.maximum(b - first_lat, 0)
    ws = jnp.clip(bl - 2, 0, nbl - NA_KBLK)
    n_rows = 2 * nbl
    nloc = NA_KBLK * QB
    rs = [jnp.clip(2 * bl + qr - NB_KH // 2, 0, n_rows - NB_KH) for qr in range(2)]
    tidx = [[jnp.clip(2 * (ws + j) - (2 * bl + qr) + (NB_KH - 1), -1, NA_TAB - 2) + 1 for j in range(NA_KBLK)]
            for qr in range(2)]
    qrow = lax.broadcasted_iota(jnp.int32, (QB, nloc), 0)
    kcol = lax.broadcasted_iota(jnp.int32, (QB, nloc), 1)
    krow = 2 * ws + jnp.right_shift(kcol, 6)
    rs_q = jnp.where(qrow < GRID_W, rs[0], rs[1])
    row_ok = jnp.logical_and(jnp.logical_and(krow >= rs_q, krow < rs_q + NB_KH), is_lat)
    row_ok2 = jnp.concatenate([row_ok, row_ok], axis=0)
    lo = _lane_lo((QB, LANES))
    zero = jnp.zeros((QB, LANES), BF16)
    kl = (k0, k1, k2, k3, k4)
    vl = (v0, v1, v2, v3, v4)

    def scores(hp):
        q2 = q_ref[hp]
        qs = jnp.concatenate([jnp.where(lo, q2, zero), jnp.where(lo, zero, q2)], axis=0)
        keys = jnp.concatenate([r[hp] for r in kl] + [kx_ref[hp]], axis=0)
        return lax.dot_general(qs, keys, (((1,), (1,)), ((), ())), preferred_element_type=F32)

    def probs(hp, s):
        bias = jnp.concatenate(
            [jnp.concatenate([tab_ref[2 * hp + hh, tidx[qr][j]] for j in range(NA_KBLK)], axis=1)
             for hh in range(2) for qr in range(2)], axis=0)
        s_loc = jnp.where(row_ok2, s[:, :nloc] + bias, NEG)
        s_ctx = s[:, nloc:]
        m = jnp.maximum(jnp.max(s_loc, axis=-1, keepdims=True), jnp.max(s_ctx, axis=-1, keepdims=True))
        e_loc = jnp.exp2(s_loc - m)
        e_ctx = jnp.exp2(s_ctx - m)
        denom = jnp.sum(e_loc, axis=-1, keepdims=True) + jnp.sum(e_ctx, axis=-1, keepdims=True)
        return jnp.concatenate([e_loc, e_ctx], axis=1).astype(BF16), denom

    def output(hp, p, denom):
        vals = jnp.concatenate([r[hp] for r in vl] + [vx_ref[hp]], axis=0)
        o = jnp.dot(p, vals, preferred_element_type=F32) / denom
        o_ref[hp] = jnp.where(lo, o[:QB], o[QB:]).astype(BF16)

    _pipeline_pairs(scores, probs, output)


def _attn_b_call(qkv3, table):
    n = qkv3.shape[1]
    nb = n // QB
    first_lat = CTX // QB
    nbl = nb - first_lat

    def win(blk, i):
        return pl.BlockSpec(
            (HP, QB, LANES),
            lambda b: (blk, first_lat + jnp.clip(jnp.maximum(b - first_lat, 0) - 2, 0, nbl - NA_KBLK) + i, 0))

    def kv_specs(blk):
        return [win(blk, i) for i in range(NA_KBLK)] + [pl.BlockSpec((HP, CTX, LANES), lambda b: (blk, 0, 0))]

    return pl.pallas_call(
        functools.partial(_attn_b_body, nbl=nbl),
        grid=(nb,),
        in_specs=[_resident((HEADS, NA_TAB, GRID_W, LANES), lambda b: (0, 0, 0, 0)),
                  pl.BlockSpec((HP, QB, LANES), lambda b: (0, b, 0))] + kv_specs(1) + kv_specs(2),
        out_specs=pl.BlockSpec((HP, QB, LANES), lambda b: (0, b, 0)),
        out_shape=jax.ShapeDtypeStruct((HP, n, LANES), BF16),
        compiler_params=_cparams(52),
        name="attn_neighbourhood",
    )(table, qkv3, *([qkv3] * 12))


def _oproj_router_body(*refs, mode):
    n_mix = 1 if mode == "attn" else 2
    w_ref, h_ref, mod_ref, g2_ref, wrh_ref, wrl_ref, br_ref, out_ref, f_ref, r_ref, cnt_ref, base_ref = refs[n_mix:]
    if mode == "attn":
        z = jnp.concatenate([refs[0][c] for c in range(HP)], axis=1)
    else:
        z = (jax.nn.gelu(refs[0][...]) * refs[1][...]).astype(BF16)
    y = jnp.dot(z, w_ref[...], preferred_element_type=F32)
    h_new = h_ref[...] + mod_ref[0, 2:3, :] * y
    out_ref[...] = h_new
    _route_tile(h_new, g2_ref[...], mod_ref[0, 3:4, :], mod_ref[0, 4:5, :], wrh_ref, wrl_ref, br_ref,
                f_ref, r_ref, cnt_ref, base_ref)


def _oproj_router_call(mix_in, w, h, mod, g2, wr, br, *, mode, name):
    n = h.shape[0]
    kdim = w.shape[0]
    if mode == "attn":
        in_specs = [pl.BlockSpec((HP, TM, LANES), lambda i: (0, i, 0))]
        args = [mix_in]
    else:
        u, hs = mix_in
        in_specs = [pl.BlockSpec((TM, D_RNN), lambda i: (i, 0)), pl.BlockSpec((TM, D_RNN), lambda i: (i, 0))]
        args = [u, hs]
    wr_hi = wr.astype(BF16)
    wr_lo = (wr - wr_hi.astype(F32)).astype(BF16)
    in_specs += [_resident((kdim, D), lambda i: (0, 0)),
                 pl.BlockSpec((TM, D), lambda i: (i, 0)),
                 pl.BlockSpec((1, 6, D), lambda i: (jnp.minimum(i, 1), 0, 0)),
                 pl.BlockSpec((1, D), lambda i: (0, 0)),
                 pl.BlockSpec((D, LANES), lambda i: (0, 0)),
                 pl.BlockSpec((D, LANES), lambda i: (0, 0)),
                 pl.BlockSpec((1, LANES), lambda i: (0, 0))]
    return pl.pallas_call(
        functools.partial(_oproj_router_body, mode=mode),
        grid=(n // TM,),
        in_specs=in_specs,
        out_specs=[pl.BlockSpec((TM, D), lambda i: (i, 0)),
                   pl.BlockSpec((TM * XROW_TILE, LANES), lambda i: (i, 0)),
                   pl.BlockSpec((TM, LANES), lambda i: (i, 0)),
                   pl.BlockSpec((1, LANES), lambda i: (0, 0))],
        out_shape=[jax.ShapeDtypeStruct((n, D), F32),
                   jax.ShapeDtypeStruct((n * XROW_TILE, LANES), jnp.uint32),
                   jax.ShapeDtypeStruct((n, LANES), F32),
                   jax.ShapeDtypeStruct((1, LANES), F32)],
        scratch_shapes=[pltpu.VMEM((1, LANES), F32)],
        compiler_params=_cparams(52),
        name=name,
    )(*args, w, h, mod, g2.reshape(1, D), wr_hi, wr_lo, br)


def _shift_rows(x, s, fill, reverse):
    t = x.shape[0]
    if s % SUBLANES == 0:
        pad = jnp.full((s, x.shape[1]), fill, x.dtype)
        return jnp.concatenate([x[s:], pad] if reverse else [pad, x[:t - s]], axis=0)
    row = lax.broadcasted_iota(jnp.int32, x.shape, 0)
    if reverse:
        return jnp.where(row >= t - s, fill, pltpu.roll(x, t - s, 0))
    return jnp.where(row < s, fill, pltpu.roll(x, s, 0))


def _chunk_scan(a, b, reverse):
    s = 1
    while s < a.shape[0]:
        a_s = _shift_rows(a, s, 1.0, reverse)
        b_s = _shift_rows(b, s, 0.0, reverse)
        b = a * b_s + b
        a = a * a_s
        s *= 2
    return a, b


def _group_scan(a, b, h_in, reverse, scr_ref):
    t, c = a.shape
    g = t // SUBLANES
    a3 = a.reshape(g, SUBLANES, c)
    b3 = b.reshape(g, SUBLANES, c)
    sub = lax.broadcasted_iota(jnp.int32, a3.shape, 1)
    s = 1
    while s < SUBLANES:
        edge = (sub >= SUBLANES - s) if reverse else (sub < s)
        shift = SUBLANES - s if reverse else s
        a_s = jnp.where(edge, 1.0, pltpu.roll(a3, shift, 1))
        b_s = jnp.where(edge, 0.0, pltpu.roll(b3, shift, 1))
        b3 = a3 * b_s + b3
        a3 = a3 * a_s
        s *= 2
    a_loc = a3.reshape(t, c)
    b_loc = b3.reshape(t, c)
    edge_row = 0 if reverse else SUBLANES - 1

    def boundary_rows(i, x):
        for j in range(c // LANES):
            scr_ref[i, j] = x[:, j * LANES:(j + 1) * LANES]
        return jnp.concatenate([scr_ref[i, j, pl.ds(edge_row, g, stride=SUBLANES), :] for j in range(c // LANES)],
                               axis=1)

    ga, gb = _chunk_scan(boundary_rows(0, a_loc), boundary_rows(1, b_loc), reverse)
    leaving = gb + ga * h_in
    entering = _shift_rows(leaving, 1, 0.0, reverse)
    grow = lax.broadcasted_iota(jnp.int32, leaving.shape, 0)
    entering = jnp.where(grow == (g - 1 if reverse else 0), h_in, entering)
    carry = jnp.broadcast_to(entering[:, None, :], (g, SUBLANES, c))
    h = (a3 * carry + b3).reshape(t, c)
    return h, (leaving[0:1, :] if reverse else leaving[g - 1:g, :])


def _lru_body(x_ref, cw_ref, cb_ref, wa_ref, wx_ref, ba_ref, bx_ref, lam_ref, o_ref, scr_ref, xc_ref, *, nchunks):
    n = nchunks * SCAN_TC
    halo = SUBLANES
    cw = cw_ref[...]
    cb = cb_ref[...]
    sp = jax.nn.softplus(-lam_ref[...])

    def conv_chunk(ci):
        r0 = pl.multiple_of(ci * SCAN_TC, SCAN_TC)
        lat_first = CTX // SCAN_TC
        prev_ok = jnp.logical_and(ci != 0, ci != lat_first).astype(F32)
        next_ok = jnp.logical_and(ci != lat_first - 1, ci != nchunks - 1).astype(F32)
        p0 = pl.multiple_of(jnp.maximum(r0 - halo, 0), halo)
        n0 = pl.multiple_of(jnp.minimum(r0 + SCAN_TC, n - halo), halo)
        xe = jnp.concatenate([x_ref[pl.ds(p0, halo), :] * prev_ok,
                              x_ref[pl.ds(r0, SCAN_TC), :],
                              x_ref[pl.ds(n0, halo), :] * next_ok], axis=0)
        y = cb
        for j in range(CONV_W):
            off = halo + j - CONV_LEFT
            y = y + xe[off:off + SCAN_TC, :] * cw[j:j + 1, :]
        return r0, y

    def direction(d, ci, h_in):
        if d == 0:
            r0, xc = conv_chunk(ci)
            xc_ref[pl.ds(r0, SCAN_TC), :] = xc
        else:
            r0 = pl.multiple_of(ci * SCAN_TC, SCAN_TC)
            xc = xc_ref[pl.ds(r0, SCAN_TC), :]
        xb = xc.astype(BF16)
        r = jax.nn.sigmoid(jnp.dot(xb, wa_ref[d, 0], preferred_element_type=F32) + ba_ref[d:d + 1, :])
        ig = jax.nn.sigmoid(jnp.dot(xb, wx_ref[d, 0], preferred_element_type=F32) + bx_ref[d:d + 1, :])
        log_a = -LRU_C * r * sp[d:d + 1, :]
        a = jnp.exp(log_a)
        bb = jnp.sqrt(1.0 - a * a) * (ig * xc)
        h, h_out = _group_scan(a, bb, h_in, d == 1, scr_ref)
        return r0, h, h_out

    def fwd(ci, h_in):
        r0, h, h_out = direction(0, ci, h_in)
        o_ref[pl.ds(r0, SCAN_TC), :] = h
        return h_out

    def bwd(ci, h_in):
        r0, h, h_out = direction(1, ci, h_in)
        o_ref[pl.ds(r0, SCAN_TC), :] += h
        return h_out

    h0 = jnp.zeros((1, RNN_BW), F32)
    lax.fori_loop(0, nchunks, fwd, h0)
    nctx = CTX // SCAN_TC
    hc = lax.fori_loop(0, nctx, lambda k, hh: bwd(nctx - 1 - k, hh), h0)
    lax.fori_loop(0, nchunks - nctx, lambda k, hh: bwd(nchunks - 1 - k, hh), hc)


def _lru_call(u, conv_w, conv_b, wa, wx, ba, bx, lam):
    n = u.shape[0]
    return pl.pallas_call(
        functools.partial(_lru_body, nchunks=n // SCAN_TC),
        grid=(RNN_BLOCKS,),
        in_specs=[_resident((n, RNN_BW), lambda m: (0, RNN_BLOCKS + m)),
                  pl.BlockSpec((CONV_W, RNN_BW), lambda m: (0, m)),
                  pl.BlockSpec((1, RNN_BW), lambda m: (0, m)),
                  pl.BlockSpec((2, 1, RNN_BW, RNN_BW), lambda m: (0, m, 0, 0)),
                  pl.BlockSpec((2, 1, RNN_BW, RNN_BW), lambda m: (0, m, 0, 0)),
                  pl.BlockSpec((2, RNN_BW), lambda m: (0, m)),
                  pl.BlockSpec((2, RNN_BW), lambda m: (0, m)),
                  pl.BlockSpec((2, RNN_BW), lambda m: (0, m))],
        out_specs=_resident((n, RNN_BW), lambda m: (0, m)),
        scratch_shapes=[pltpu.VMEM((2, RNN_BW // LANES, SCAN_TC, LANES), F32), pltpu.VMEM((n, RNN_BW), F32)],
        out_shape=jax.ShapeDtypeStruct((n, D_RNN), F32),
        compiler_params=_cparams(58),
        name="rglru_scan",
    )(u, conv_w, conv_b.reshape(1, D_RNN), wa, wx, ba, bx, lam)


def _pack_words(ref, lo, hi, first_tile):
    rows = lo.shape[0]
    lo = lax.bitcast_convert_type(lo.astype(BF16).astype(F32), jnp.uint32)
    hi = lax.bitcast_convert_type(hi.astype(BF16).astype(F32), jnp.uint32)
    w = hi | (lo >> 16)
    for s in range(lo.shape[1] // LANES):
        ref[pl.ds(first_tile + s, rows, stride=XROW_TILE), :] = w[:, s * LANES:(s + 1) * LANES]


def _pack_rows(ref, x):
    _pack_words(ref, x[:, :D // 2], x[:, D // 2:], 0)


def _unpack_rows(ref, rows, dtype):
    w = jnp.concatenate([ref[pl.ds(s, rows, stride=XROW_TILE), :] for s in range(XROW_TILE)], axis=1)
    lo = lax.bitcast_convert_type(w << 16, F32)
    hi = lax.bitcast_convert_type(w & jnp.uint32(0xFFFF0000), F32)
    return jnp.concatenate([lo, hi], axis=1).astype(dtype)


def _route_tile(h, g, shift, scale, wrh_ref, wrl_ref, br_ref, f_ref, r_ref, cnt_ref, base_ref):
    @pl.when(pl.program_id(0) == 0)
    def _():
        base_ref[...] = jnp.zeros(base_ref.shape, F32)

    f = _norm_mod(h, g, shift, scale)
    _pack_rows(f_ref, f)
    f_hi = f.astype(BF16)
    f_lo = (f - f_hi.astype(F32)).astype(BF16)
    logits = (jnp.dot(jnp.concatenate([f_hi, f_lo], axis=0), wrh_ref[...], preferred_element_type=F32).reshape(2, TM, LANES).sum(axis=0)
              + jnp.dot(f_hi, wrl_ref[...], preferred_element_type=F32) + br_ref[...])
    lane = lax.broadcasted_iota(jnp.int32, logits.shape, 1)
    ninf = -jnp.inf
    is_g = lane < N_GROUPS
    gl = jnp.where(is_g, logits, ninf)
    gmax = jnp.max(gl, axis=-1, keepdims=True)
    gsel = jnp.min(jnp.where(gl == gmax, lane, LANES), axis=-1, keepdims=True)
    gsum = jnp.sum(jnp.where(is_g, jnp.exp(gl - gmax), 0.0), axis=-1, keepdims=True)
    g_w = 1.0 / gsum
    e_lo = N_GROUPS + EXP_PER_GROUP * gsel
    in_grp = jnp.logical_and(lane >= e_lo, lane < e_lo + EXP_PER_GROUP)
    el = jnp.where(in_grp, logits, ninf)
    m1 = jnp.max(el, axis=-1, keepdims=True)
    i1 = jnp.min(jnp.where(el == m1, lane, LANES), axis=-1, keepdims=True)
    el2 = jnp.where(lane == i1, ninf, el)
    m2 = jnp.max(el2, axis=-1, keepdims=True)
    i2 = jnp.min(jnp.where(el2 == m2, lane, LANES), axis=-1, keepdims=True)
    t = jnp.exp(m2 - m1)
    w1 = g_w / (1.0 + t)
    w2 = g_w * t / (1.0 + t)
    oh1 = lane == i1
    oh2 = lane == i2
    rr = lax.broadcasted_iota(jnp.int32, (TM, TM), 0)
    cc = lax.broadcasted_iota(jnp.int32, (TM, TM), 1)
    tri = (cc < rr).astype(BF16)
    pre1 = jnp.dot(tri, oh1.astype(BF16), preferred_element_type=F32)
    pre2 = jnp.dot(tri, oh2.astype(BF16), preferred_element_type=F32)
    base = base_ref[...]
    cnt1 = jnp.sum(oh1.astype(F32), axis=0, keepdims=True)
    cnt2 = jnp.sum(oh2.astype(F32), axis=0, keepdims=True)
    rank1 = jnp.sum(jnp.where(oh1, pre1 + base, 0.0), axis=-1, keepdims=True)
    rank2 = jnp.sum(jnp.where(oh2, pre2 + (base + cnt1), 0.0), axis=-1, keepdims=True)
    total = base + cnt1 + cnt2
    base_ref[...] = total
    cnt_ref[...] = total
    cols = [(i1 - N_GROUPS).astype(F32), (i2 - N_GROUPS).astype(F32), w1, w2, rank1, rank2]
    out = jnp.zeros(logits.shape, F32)
    for k, v in enumerate(cols):
        out = jnp.where(lane == k, v, out)
    r_ref[...] = out


def _dispatch_body(dest_ref, pad_lo_ref, pad_n_ref, f_ref, xg_ref, z_ref, sem, blk_sem):
    i = pl.program_id(0)
    base = i * TM

    def issue(j, carry):
        for k in range(TOP_K):
            d = dest_ref[(base + j) * TOP_K + k]
            pltpu.make_async_copy(f_ref.at[pl.ds(j * XROW_TILE, XROW_TILE)],
                                  xg_ref.at[pl.ds(d * XROW_TILE, XROW_TILE)], sem).start(priority=k)
        return carry

    z_row = z_ref.at[pl.ds(0, XROW_TILE)]

    def wait_one(carry):
        pltpu.make_async_copy(z_row, xg_ref.at[pl.ds(0, XROW_TILE)], sem).wait()
        return carry

    lax.fori_loop(0, TM, issue, 0, unroll=DMA_UNROLL)

    @pl.when(i == 0)
    def _():
        z_ref[...] = jnp.zeros(z_ref.shape, jnp.uint32)

        def fill(e, carry):
            lo = pad_lo_ref[e]
            n_fill = pad_n_ref[e]

            def batch(bi, c):
                s0 = bi * MOE_MB
                cnt = jnp.minimum(n_fill - s0, MOE_MB)

                def one(s, c2):
                    pltpu.make_async_copy(z_row, xg_ref.at[pl.ds((lo + s0 + s) * XROW_TILE, XROW_TILE)], sem).start()
                    return c2

                lax.fori_loop(0, cnt, one, 0)
                lax.fori_loop(0, cnt, lambda s, c2: wait_one(c2), 0)
                return c

            lax.fori_loop(0, (n_fill + (MOE_MB - 1)) // MOE_MB, batch, 0)
            return carry

        lax.fori_loop(0, N_EXPERTS, fill, 0)

        def block_copy(bi):
            row0 = (pad_lo_ref[N_EXPERTS] + bi * MOE_MB) * XROW_TILE
            return pltpu.make_async_copy(z_ref, xg_ref.at[pl.ds(row0, MOE_MB * XROW_TILE)], blk_sem)

        n_tail = pad_n_ref[N_EXPERTS] // MOE_MB
        lax.fori_loop(0, n_tail, lambda bi, c: (block_copy(bi).start(), c)[1], 0)
        lax.fori_loop(0, n_tail, lambda bi, c: (block_copy(bi).wait(), c)[1], 0)

    lax.fori_loop(0, TM * TOP_K, lambda j, c: wait_one(c), 0, unroll=DMA_UNROLL)


def _dispatch_call(dest, pad_lo, pad_n, f2, p):
    n = f2.shape[0] // XROW_TILE
    return pl.pallas_call(
        _dispatch_body,
        grid_spec=pltpu.PrefetchScalarGridSpec(
            num_scalar_prefetch=3,
            grid=(n // TM,),
            in_specs=[pl.BlockSpec((TM * XROW_TILE, LANES), lambda i, d, lo, nn: (i, 0))],
            out_specs=pl.BlockSpec(memory_space=pl.ANY),
            scratch_shapes=[pltpu.VMEM((MOE_MB * XROW_TILE, LANES), jnp.uint32), pltpu.SemaphoreType.DMA(()),
                            pltpu.SemaphoreType.DMA(())]),
        out_shape=jax.ShapeDtypeStruct((p * XROW_TILE, LANES), jnp.uint32),
        compiler_params=_cparams(16),
        name="moe_dispatch",
    )(dest, pad_lo, pad_n, f2)


def _expert_weights(b, be_ref, ne_ref, nu_ref, w_hbm, wf_ref, wb_ref, sem, slot_ref, layer):
    e = be_ref[b]
    first = jnp.logical_or(b == 0, e != be_ref[jnp.maximum(b - 1, 0)])

    def fetch(expert, slot):
        return pltpu.make_async_copy(w_hbm.at[layer, expert], wf_ref.at[slot], sem.at[slot])

    @pl.when(b == 0)
    def _():
        slot_ref[0] = 0
        fetch(e, 0).start()

    @pl.when(jnp.logical_and(first, b < nu_ref[0]))
    def _():
        s = slot_ref[0]
        fetch(e, s).wait()
        wb_ref[...] = wf_ref[s].astype(BF16)
        ne = ne_ref[b]

        @pl.when(ne != e)
        def _():
            fetch(ne, 1 - s).start()

        slot_ref[0] = 1 - s


def _ffn_up_body(be_ref, ne_ref, nu_ref, x_ref, w_hbm, a_ref, wf_ref, wb_ref, sem, slot_ref, *, layer):
    b = pl.program_id(0)
    _expert_weights(b, be_ref, ne_ref, nu_ref, w_hbm, wf_ref, wb_ref, sem, slot_ref, layer)

    @pl.when(b < nu_ref[0])
    def _():
        x = _unpack_rows(x_ref, MOE_MB, BF16)
        cw = FFN_CHUNK
        n_chunks = D_EXPERT // cw

        def gate_up(c):
            return (jnp.dot(x, wb_ref[:, c * cw:(c + 1) * cw], preferred_element_type=F32),
                    jnp.dot(x, wb_ref[:, D_EXPERT + c * cw:D_EXPERT + (c + 1) * cw], preferred_element_type=F32))

        nxt = gate_up(0)
        for c in range(n_chunks):
            g, u = nxt
            if c + 1 < n_chunks:
                nxt = gate_up(c + 1)
            a_ref[:, c * cw:(c + 1) * cw] = (g * jax.nn.sigmoid(g) * u).astype(BF16)

    @pl.when(b >= nu_ref[0])
    def _():
        a_ref[...] = jnp.zeros(a_ref.shape, BF16)


def _expert_scratch(rows, cols):
    return [pltpu.VMEM((2, rows, cols), F32), pltpu.VMEM((rows, cols), BF16),
            pltpu.SemaphoreType.DMA((2,)), pltpu.SMEM((1,), jnp.int32)]


def _ffn_up_call(block_e, next_e, n_used, xg, w_gu, layer):
    p = xg.shape[0] // XROW_TILE
    return pl.pallas_call(
        functools.partial(_ffn_up_body, layer=layer),
        grid_spec=pltpu.PrefetchScalarGridSpec(
            num_scalar_prefetch=3,
            grid=(p // MOE_MB,),
            in_specs=[pl.BlockSpec((MOE_MB * XROW_TILE, LANES),
                                   lambda b, be, ne, nu: (jnp.minimum(b, nu[0] - 1), 0)),
                      pl.BlockSpec(memory_space=pl.ANY)],
            out_specs=pl.BlockSpec((MOE_MB, D_EXPERT), lambda b, be, ne, nu: (b, 0)),
            scratch_shapes=_expert_scratch(D, 2 * D_EXPERT)),
        out_shape=jax.ShapeDtypeStruct((p, D_EXPERT), BF16),
        compiler_params=_cparams(48),
        name="moe_ffn_up",
    )(block_e, next_e, n_used, xg, w_gu)


def _ffn_down_body(be_ref, ne_ref, nu_ref, a_ref, w_hbm, y_ref, wf_ref, wb_ref, sem, slot_ref, *, layer):
    b = pl.program_id(0)
    _expert_weights(b, be_ref, ne_ref, nu_ref, w_hbm, wf_ref, wb_ref, sem, slot_ref, layer)

    @pl.when(b < nu_ref[0])
    def _():
        a = a_ref[...]
        cw = FFN_CHUNK
        n_chunks = D // 2 // cw

        def halves(c):
            return (jnp.dot(a, wb_ref[:, c * cw:(c + 1) * cw], preferred_element_type=F32),
                    jnp.dot(a, wb_ref[:, D // 2 + c * cw:D // 2 + (c + 1) * cw], preferred_element_type=F32))

        nxt = halves(0)
        for c in range(n_chunks):
            lo, hi = nxt
            if c + 1 < n_chunks:
                nxt = halves(c + 1)
            _pack_words(y_ref, lo, hi, c * (cw // LANES))

    @pl.when(b >= nu_ref[0])
    def _():
        y_ref[...] = jnp.zeros(y_ref.shape, jnp.uint32)


def _ffn_down_call(block_e, next_e, n_used, act, w_down, layer):
    p = act.shape[0]
    return pl.pallas_call(
        functools.partial(_ffn_down_body, layer=layer),
        grid_spec=pltpu.PrefetchScalarGridSpec(
            num_scalar_prefetch=3,
            grid=(p // MOE_MB,),
            in_specs=[pl.BlockSpec((MOE_MB, D_EXPERT), lambda b, be, ne, nu: (b, 0)),
                      pl.BlockSpec(memory_space=pl.ANY)],
            out_specs=pl.BlockSpec((MOE_MB * XROW_TILE, LANES), lambda b, be, ne, nu: (b, 0)),
            scratch_shapes=_expert_scratch(D_EXPERT, D)),
        out_shape=jax.ShapeDtypeStruct((p * XROW_TILE, LANES), jnp.uint32),
        compiler_params=_cparams(32),
        name="moe_ffn_down",
    )(block_e, next_e, n_used, act, w_down)


def _combine_body(dest_ref, h_ref, r_ref, mod_ref, g_ref, yb_ref, o_ref, buf_ref, sem, *, final):
    i = pl.program_id(0)
    slot = i % 2

    def gather(tile, dst_slot):
        base = tile * TM

        def issue(j, carry):
            for k in range(TOP_K):
                d = dest_ref[(base + j) * TOP_K + k]
                pltpu.make_async_copy(yb_ref.at[pl.ds(d * XROW_TILE, XROW_TILE)],
                                      buf_ref.at[dst_slot, k, pl.ds(j * XROW_TILE, XROW_TILE)],
                                      sem.at[dst_slot]).start(priority=k)
            return carry

        lax.fori_loop(0, TM, issue, 0, unroll=DMA_UNROLL)

    @pl.when(i == 0)
    def _():
        gather(0, 0)

    @pl.when(i + 1 < pl.num_programs(0))
    def _():
        gather(i + 1, 1 - slot)

    def drain(j, carry):
        pltpu.make_async_copy(yb_ref.at[pl.ds(0, XROW_TILE)], buf_ref.at[slot, 0, pl.ds(0, XROW_TILE)],
                              sem.at[slot]).wait()
        return carry

    lax.fori_loop(0, TM * TOP_K, drain, 0, unroll=DMA_UNROLL)
    r = r_ref[...]
    y = sum(r[:, TOP_K + k:TOP_K + k + 1] * _unpack_rows(buf_ref.at[slot, k], TM, F32) for k in range(TOP_K))
    h_new = h_ref[...] + mod_ref[0, 2:3, :] * y
    if final:
        ms = jnp.mean(h_new * h_new, axis=-1, keepdims=True)
        h_new = h_new * lax.rsqrt(ms + EPS) * g_ref[...]
    o_ref[...] = h_new


def _combine_call(dest, h, route, mod, yb, final_g=None):
    n = h.shape[0]
    final = final_g is not None
    first = CTX // TM
    out_rows = n - CTX if final else n
    out_map = (lambda i, d: (jnp.maximum(i - first, 0), 0)) if final else (lambda i, d: (i, 0))
    g = final_g if final else jnp.ones((D,), F32)
    return pl.pallas_call(
        functools.partial(_combine_body, final=final),
        grid_spec=pltpu.PrefetchScalarGridSpec(
            num_scalar_prefetch=1,
            grid=(n // TM,),
            in_specs=[pl.BlockSpec((TM, D), lambda i, d: (i, 0)),
                      pl.BlockSpec((TM, LANES), lambda i, d: (i, 0)),
                      pl.BlockSpec((1, 3, D), lambda i, d: (jnp.minimum(i, 1), 0, 0)),
                      pl.BlockSpec((1, D), lambda i, d: (0, 0)),
                      pl.BlockSpec(memory_space=pl.ANY)],
            out_specs=pl.BlockSpec((TM, D), out_map),
            scratch_shapes=[pltpu.VMEM((2, TOP_K, TM * XROW_TILE, LANES), jnp.uint32),
                            pltpu.SemaphoreType.DMA((2,))]),
        out_shape=jax.ShapeDtypeStruct((out_rows, D), F32),
        compiler_params=_cparams(32),
        name="moe_combine",
    )(dest, h, route, mod[:, 3:6], g.reshape(1, D), yb)


def _dispatch_plan(route, cnt, n):
    experts = jnp.arange(N_EXPERTS, dtype=jnp.int32)
    counts = cnt[0, N_GROUPS:N_GROUPS + N_EXPERTS].astype(jnp.int32)
    padded = (counts + MOE_MB - 1) // MOE_MB * MOE_MB
    end_pad = jnp.cumsum(padded)
    start_pad = end_pad - padded
    n_blocks = -(-(n * TOP_K + N_EXPERTS * (MOE_MB - 1)) // MOE_MB)
    p = n_blocks * MOE_MB
    blk_start = jnp.arange(n_blocks, dtype=jnp.int32) * MOE_MB
    block_e = jnp.minimum(jnp.sum(end_pad[None, :] <= blk_start[:, None], axis=1), N_EXPERTS - 1).astype(jnp.int32)
    n_used = (end_pad[-1:] // MOE_MB).astype(jnp.int32)
    group_end = end_pad[block_e] // MOE_MB
    next_e = jnp.where(group_end < n_used[0], block_e[jnp.minimum(group_end, n_blocks - 1)], block_e)
    e_idx = route[:, 0:TOP_K].astype(jnp.int32)
    rank = route[:, 2 * TOP_K:3 * TOP_K].astype(jnp.int32)
    start = jnp.sum(jnp.where(e_idx[..., None] == experts, start_pad, 0), axis=-1)
    dest = (start + rank).reshape(n * TOP_K)
    pad_lo = jnp.concatenate([start_pad + counts, end_pad[-1:]]).astype(jnp.int32)
    pad_n = jnp.concatenate([padded - counts, p - end_pad[-1:]]).astype(jnp.int32)
    return p, block_e, next_e, n_used, dest, pad_lo, pad_n


def _moe_layer(h, f2, route, cnt, mod, w_gu, w_down, layer, final_g=None):
    n = h.shape[0]
    p, block_e, next_e, n_used, dest, pad_lo, pad_n = _dispatch_plan(route, cnt, n)
    xg = _dispatch_call(dest, pad_lo, pad_n, f2, p)
    act = _ffn_up_call(block_e, next_e, n_used, xg, w_gu, layer)
    yb = _ffn_down_call(block_e, next_e, n_used, act, w_down, layer)
    return _combine_call(dest, h, route, mod, yb, final_g)


def _rope_tables(l):
    quarter = HEAD_DIM // 4
    inv = ROPE_BASE ** (-jnp.arange(quarter, dtype=F32) / quarter)
    n_rows = l // GRID_W
    ang_r = jnp.arange(n_rows, dtype=F32)[:, None] * inv
    ang_c = jnp.arange(GRID_W, dtype=F32)[:, None] * inv
    by_row = lambda t: jnp.broadcast_to(t[:, None, :], (n_rows, GRID_W, quarter)).reshape(l, quarter)
    by_col = lambda t: jnp.broadcast_to(t[None, :, :], (n_rows, GRID_W, quarter)).reshape(l, quarter)
    cr, sr, cc, sc = by_row(jnp.cos(ang_r)), by_row(jnp.sin(ang_r)), by_col(jnp.cos(ang_c)), by_col(jnp.sin(ang_c))
    cos = jnp.concatenate([cr, cr, cc, cc], axis=1)
    sin = jnp.concatenate([-sr, sr, -sc, sc], axis=1)
    cos = jnp.concatenate([jnp.ones((CTX, HEAD_DIM), F32), cos], axis=0)
    sin = jnp.concatenate([jnp.zeros((CTX, HEAD_DIM), F32), sin], axis=0)
    return jnp.tile(cos, (1, 2)), jnp.tile(sin, (1, 2))


def _attn_a_weights(w_qkv):
    nq = HEADS * HEAD_DIM
    nkv = A_KV_HEADS * HEAD_DIM
    wq = w_qkv[:, :nq]
    wk = w_qkv[:, nq:nq + nkv].reshape(D, A_KV_HEADS, 1, HEAD_DIM)
    wv = w_qkv[:, nq + nkv:].reshape(D, A_KV_HEADS, 1, HEAD_DIM)
    dup = lambda w: jnp.broadcast_to(w, (D, A_KV_HEADS, 2, HEAD_DIM)).reshape(D, 2 * nkv)
    return jnp.concatenate([wq, dup(wk), dup(wv)], axis=1).astype(BF16)


def kernel(x, c, ctx, c_ctx, ada_w, ada_b, norm_mix_g, norm_ffn_g, router_group_w, router_group_b,
           router_expert_w, router_expert_b, moe_w_gu, moe_w_down, attn_w_qkv, attn_w_o, attn_sink,
           na_w_qkv, na_w_o, na_rpb, rnn_w_in, rnn_conv_w, rnn_conv_b, rnn_wa, rnn_ba, rnn_wx, rnn_bx,
           rnn_lam, rnn_w_out, final_norm_g):
    batch, l, _ = x.shape
    assert batch == 1 and ctx.shape[1] == CTX and l % (QB * 2) == 0 and l // QB >= NA_KBLK
    h = jnp.concatenate([ctx[0], x[0]], axis=0)
    c2 = jnp.stack([c_ctx, c[0]], axis=1)
    mods = _ada_call(c2, ada_w, ada_b).reshape(DEPTH, 2, 6, D)
    rope = _rope_tables(l)
    pad_r = LANES - N_GROUPS - N_EXPERTS
    for i in range(DEPTH):
        kind, j = i % 3, i // 3
        mod = mods[i]
        wr = jnp.concatenate([router_group_w[i], router_expert_w[i], jnp.zeros((D, pad_r), F32)], axis=1)
        br = jnp.concatenate([router_group_b[i], router_expert_b[i], jnp.zeros((pad_r,), F32)]).reshape(1, LANES)
        if kind == 0:
            w = _attn_a_weights(attn_w_qkv[j])
            nq = HEADS * HEAD_DIM
            qkv3 = _proj_call(h, norm_mix_g[i], mod, w, mode="rope", rope=rope,
                              n_rope=nq + 2 * A_KV_HEADS * HEAD_DIM, n_q=nq, name="proj_window")
            mix, w_out, mode, name = _attn_a_call(qkv3, attn_sink[j]), attn_w_o[j], "attn", "oproj_window"
        elif kind == 1:
            qkv3 = _proj_call(h, norm_mix_g[i], mod, na_w_qkv[j].astype(BF16), mode="cols",
                              n_q=HEADS * HEAD_DIM, name="proj_neighbourhood")
            mix, w_out, mode, name = (_attn_b_call(qkv3, _na_table_call(na_rpb[j])), na_w_o[j], "attn",
                                      "oproj_neighbourhood")
        else:
            u = _proj_call(h, norm_mix_g[i], mod, rnn_w_in[j].astype(BF16), mode="plain", name="proj_rglru")
            hs = _lru_call(u, rnn_conv_w[j], rnn_conv_b[j], rnn_wa[j].astype(BF16), rnn_wx[j].astype(BF16),
                           rnn_ba[j], rnn_bx[j], rnn_lam[j])
            mix, w_out, mode, name = (u, hs), rnn_w_out[j], "rnn", "oproj_rglru"
        h, f2, route, cnt = _oproj_router_call(mix, w_out.astype(BF16), h, mod, norm_ffn_g[i], wr, br,
                                               mode=mode, name=name)
        h = _moe_layer(h, f2, route, cnt, mod, moe_w_gu, moe_w_down, i,
                       final_norm_g if i == DEPTH - 1 else None)
    return h[None]
```

```python
import functools

import jax
import jax.numpy as jnp
from jax import lax
from jax.experimental import pallas as pl
from jax.experimental.pallas import tpu as pltpu

F32 = jnp.float32
BF16 = jnp.bfloat16

D = 2048
DEPTH = 4
GRID_W = 64
CTX = 256
HEADS = 32
HEAD_DIM = 64
A_KV_HEADS = 4
A_GROUP = HEADS // A_KV_HEADS
WINDOW = 128
NB_KH = 8
NB_KW = 16
D_RNN = 2560
RNN_BLOCKS = 10
RNN_BW = D_RNN // RNN_BLOCKS
CONV_W = 4
CONV_LEFT = 2
LRU_C = 8.0
N_GROUPS = 4
EXP_PER_GROUP = 8
N_EXPERTS = N_GROUPS * EXP_PER_GROUP
TOP_K = 2
D_EXPERT = 768
ROPE_BASE = 10000.0
EPS = 1e-6
NEG = -1e30

LANES = 128
SUBLANES = 8
MIB = 1024 * 1024

TM = 256
QB = 128
HP = HEADS // 2
MOE_MB = 512
ROW_TILE = D // LANES
XROW_TILE = ROW_TILE // 2
DMA_UNROLL = 8
FFN_CHUNK = 256
SCAN_TC = 256
NA_KBLK = 5
SQRT_SCALE = HEAD_DIM ** -0.5
LOG2E = 1.4426950408889634


def _cparams(vmem_mib, sem=("arbitrary",)):
    return pltpu.CompilerParams(dimension_semantics=sem, vmem_limit_bytes=int(vmem_mib * MIB))


def _resident(block_shape, index_map):
    return pl.BlockSpec(block_shape, index_map, pipeline_mode=pl.Buffered(1))


def _mod_spec():
    return pl.BlockSpec((1, 3, D), lambda i: (jnp.minimum(i, 1), 0, 0))


def _norm_mod(x, g, shift, scale):
    ms = jnp.mean(x * x, axis=-1, keepdims=True)
    y = x * lax.rsqrt(ms + EPS) * g
    return y * (1.0 + scale) + shift


ADA_TN = 1024


def _ada_body(c_ref, w_ref, b_ref, o_ref):
    c = c_ref[...]
    cs = c * jax.nn.sigmoid(c)
    for r in range(2):
        cb = jnp.broadcast_to(cs[:, r:r + 1], (D, LANES))
        outs = []
        for j in range(ADA_TN // LANES):
            w = w_ref[0, :, j * LANES:(j + 1) * LANES]
            p = (w * cb).reshape(D // SUBLANES, SUBLANES, LANES).sum(axis=0)
            outs.append(p.sum(axis=0, keepdims=True))
        o_ref[0, r:r + 1, :] = jnp.concatenate(outs, axis=1) + b_ref[0]


def _ada_call(c2, ada_w, ada_b):
    return pl.pallas_call(
        _ada_body,
        grid=(DEPTH, 6 * D // ADA_TN),
        in_specs=[pl.BlockSpec((D, 2), lambda l, j: (0, 0)),
                  pl.BlockSpec((1, D, ADA_TN), lambda l, j: (l, 0, j)),
                  pl.BlockSpec((1, 1, ADA_TN), lambda l, j: (l, 0, j))],
        out_specs=pl.BlockSpec((1, 2, ADA_TN), lambda l, j: (l, 0, j)),
        out_shape=jax.ShapeDtypeStruct((DEPTH, 2, 6 * D), F32),
        compiler_params=_cparams(32, ("arbitrary", "arbitrary")),
        name="ada_mod",
    )(c2, ada_w, ada_b.reshape(DEPTH, 1, 6 * D))


PROJ_CH = 512


def _rope_piece(piece, cos, sin):
    lane = lax.broadcasted_iota(jnp.int32, piece.shape, 1)
    first = (lane & 16) == 0
    partner = jnp.where(first, pltpu.roll(piece, LANES - 16, 1), pltpu.roll(piece, 16, 1))
    return piece * cos + partner * sin


def _proj_body(*refs, mode, n_rope, n_q):
    if mode == "rope":
        x_ref, g_ref, mod_ref, w_ref, cos_ref, sin_ref, o_ref = refs
    else:
        x_ref, g_ref, mod_ref, w_ref, o_ref = refs
    a = _norm_mod(x_ref[...], g_ref[...], mod_ref[0, 0:1, :], mod_ref[0, 1:2, :]).astype(BF16)
    nout = w_ref.shape[1]
    for c in range(nout // PROJ_CH):
        acc = jnp.dot(a, w_ref[:, c * PROJ_CH:(c + 1) * PROJ_CH], preferred_element_type=F32)
        if mode == "plain":
            o_ref[:, c * PROJ_CH:(c + 1) * PROJ_CH] = acc
            continue
        for k in range(PROJ_CH // LANES):
            col0 = c * PROJ_CH + k * LANES
            piece = acc[:, k * LANES:(k + 1) * LANES]
            if col0 < n_rope:
                piece = _rope_piece(piece, cos_ref[...], sin_ref[...])
            if col0 < n_q:
                piece = piece * (SQRT_SCALE * LOG2E)
            o_ref[col0 // LANES] = piece.astype(BF16)


def _proj_call(h, g, mod, w, *, mode, rope=None, n_rope=0, n_q=0, name):
    n = h.shape[0]
    nout = w.shape[1]
    in_specs = [pl.BlockSpec((TM, D), lambda i: (i, 0)),
                pl.BlockSpec((1, D), lambda i: (0, 0)),
                _mod_spec(),
                _resident((D, nout), lambda i: (0, 0))]
    args = [h, g.reshape(1, D), mod[:, 0:3], w]
    if mode == "rope":
        in_specs += [pl.BlockSpec((TM, LANES), lambda i: (i, 0))] * 2
        args += list(rope)
    if mode == "plain":
        out_spec = pl.BlockSpec((TM, nout), lambda i: (i, 0))
        out_shape = jax.ShapeDtypeStruct((n, nout), F32)
        out_bytes = TM * nout * 4
    else:
        out_spec = pl.BlockSpec((nout // LANES, TM, LANES), lambda i: (0, i, 0))
        out_shape = jax.ShapeDtypeStruct((nout // LANES, n, LANES), BF16)
        out_bytes = TM * nout * 2
    vmem = (D * nout * 2 + 2 * TM * D * 4 + 2 * out_bytes) / MIB + 12
    return pl.pallas_call(
        functools.partial(_proj_body, mode=mode, n_rope=n_rope, n_q=n_q),
        grid=(n // TM,),
        in_specs=in_specs,
        out_specs=out_spec,
        out_shape=out_shape,
        compiler_params=_cparams(vmem),
        name=name,
    )(*args)


def _lane_lo(shape):
    return lax.broadcasted_iota(jnp.int32, shape, len(shape) - 1) < HEAD_DIM


def _pipeline_pairs(scores, probs, output):
    s_next = scores(0)
    pending = None
    for hp in range(HP):
        s_cur = s_next
        if hp + 1 < HP:
            s_next = scores(hp + 1)
        cur = probs(hp, s_cur)
        if pending is not None:
            output(hp - 1, *pending)
        pending = cur
    output(HP - 1, *pending)


def _attn_a_body(sink_ref, q_ref, kp_ref, kc_ref, kn_ref, kx_ref, vp_ref, vc_ref, vn_ref, vx_ref, o_ref, *, nb):
    b = pl.program_id(0)
    first_lat = CTX // QB
    is_lat = b >= first_lat
    prev_ok = b >= first_lat + 1
    next_ok = jnp.logical_and(is_lat, b <= nb - 2)
    nloc = 3 * QB
    nkeys = nloc + CTX
    qi = lax.broadcasted_iota(jnp.int32, (QB, nkeys), 0)
    kj = lax.broadcasted_iota(jnp.int32, (QB, nkeys), 1)
    rel = qi + WINDOW - kj
    band = jnp.abs(rel) <= WINDOW
    seg_ok = jnp.where(kj < QB, prev_ok.astype(jnp.int32),
                       jnp.where(kj < 2 * QB, is_lat.astype(jnp.int32), next_ok.astype(jnp.int32)))
    ok = jnp.logical_or(kj >= nloc, jnp.logical_and(band, seg_ok > 0))
    mask_add = jnp.where(ok, 0.0, NEG)
    lo = _lane_lo((QB, LANES))
    zero = jnp.zeros((QB, LANES), BF16)
    def scores(hp):
        kvh = hp // (A_GROUP // 2)
        keys = jnp.concatenate([kp_ref[kvh], kc_ref[kvh], kn_ref[kvh], kx_ref[kvh]], axis=0)
        q2 = q_ref[hp]
        qs = jnp.concatenate([jnp.where(lo, q2, zero), jnp.where(lo, zero, q2)], axis=0)
        return lax.dot_general(qs, keys, (((1,), (1,)), ((), ())), preferred_element_type=F32)

    def probs(hp, s):
        sink = jnp.concatenate([jnp.full((1, QB, 1), sink_ref[2 * hp + hh] * LOG2E, F32) for hh in range(2)], axis=0)
        s3 = s.reshape(2, QB, nkeys) + mask_add[None]
        m = jnp.maximum(jnp.max(s3, axis=-1, keepdims=True), sink)
        e = jnp.exp2(s3 - m)
        denom = jnp.sum(e, axis=-1, keepdims=True) + jnp.exp2(sink - m)
        return e.astype(BF16).reshape(2 * QB, nkeys), denom

    def output(hp, e, denom):
        kvh = hp // (A_GROUP // 2)
        vals = jnp.concatenate([vp_ref[kvh], vc_ref[kvh], vn_ref[kvh], vx_ref[kvh]], axis=0)
        o = jnp.dot(e, vals, preferred_element_type=F32).reshape(2, QB, LANES) / denom
        o_ref[hp] = jnp.where(lo, o[0], o[1]).astype(BF16)

    _pipeline_pairs(scores, probs, output)


def _attn_a_call(qkv3, sink):
    n = qkv3.shape[1]
    nb = n // QB
    first_lat = CTX // QB
    kblk, vblk = HP // A_KV_HEADS, HP // A_KV_HEADS + 1

    def kv_specs(blk):
        return [pl.BlockSpec((A_KV_HEADS, QB, LANES), lambda b: (blk, jnp.maximum(b - 1, first_lat), 0)),
                pl.BlockSpec((A_KV_HEADS, QB, LANES), lambda b: (blk, b, 0)),
                pl.BlockSpec((A_KV_HEADS, QB, LANES), lambda b: (blk, jnp.minimum(b + 1, nb - 1), 0)),
                pl.BlockSpec((A_KV_HEADS, CTX, LANES), lambda b: (blk, 0, 0))]

    return pl.pallas_call(
        functools.partial(_attn_a_body, nb=nb),
        grid=(nb,),
        in_specs=[pl.BlockSpec(memory_space=pltpu.SMEM),
                  pl.BlockSpec((HP, QB, LANES), lambda b: (0, b, 0))] + kv_specs(kblk) + kv_specs(vblk),
        out_specs=pl.BlockSpec((HP, QB, LANES), lambda b: (0, b, 0)),
        out_shape=jax.ShapeDtypeStruct((HP, n, LANES), BF16),
        compiler_params=_cparams(40),
        name="attn_window",
    )(sink, qkv3, *([qkv3] * 8))


NA_TAB = 2 * NB_KH
RPB_W = 2 * NB_KW - 1
RPB_H = 2 * NB_KH - 1


def _na_table_body(rpb_ref, o_ref):
    h = pl.program_id(0)
    shape = (GRID_W, LANES)
    c = lax.broadcasted_iota(jnp.int32, shape, 0)
    lane = lax.broadcasted_iota(jnp.int32, shape, 1)
    kc = lane & (GRID_W - 1)
    hi = lane >= GRID_W
    cs = jnp.clip(c - NB_KW // 2, 0, GRID_W - NB_KW)
    colok = jnp.logical_and(kc >= cs, kc < cs + NB_KW)
    diff = kc - c + (NB_KW - 1)
    neg = jnp.full(shape, NEG, F32)
    rows = [neg]
    for d in range(RPB_H):
        acc = neg
        for j in range(RPB_W):
            acc = jnp.where(diff == j, rpb_ref[h * (RPB_H * RPB_W) + d * RPB_W + j] * LOG2E, acc)
        rows.append(jnp.where(colok, acc, NEG))
    rows.append(neg)
    for t in range(NA_TAB):
        o_ref[0, t] = jnp.where(hi, rows[t + 1], rows[t])


def _na_table_call(rpb):
    return pl.pallas_call(
        _na_table_body,
        grid=(HEADS,),
        in_specs=[pl.BlockSpec(memory_space=pltpu.SMEM)],
        out_specs=pl.BlockSpec((1, NA_TAB, GRID_W, LANES), lambda h: (h, 0, 0, 0)),
        out_shape=jax.ShapeDtypeStruct((HEADS, NA_TAB, GRID_W, LANES), F32),
        compiler_params=_cparams(16),
        name="na_bias_table",
    )(rpb.reshape(-1))


def _attn_b_body(tab_ref, q_ref, k0, k1, k2, k3, k4, kx_ref, v0, v1, v2, v3, v4, vx_ref, o_ref, *, nbl):
    b = pl.program_id(0)
    first_lat = CTX // QB
    is_lat = b >= first_lat
    bl = jnp.maximum(b - first_lat, 0)
    ws = jnp.clip(bl - 2, 0, nbl - NA_KBLK)
    n_rows = 2 * nbl
    nloc = NA_KBLK * QB
    rs = [jnp.clip(2 * bl + qr - NB_KH // 2, 0, n_rows - NB_KH) for qr in range(2)]
    tidx = [[jnp.clip(2 * (ws + j) - (2 * bl + qr) + (NB_KH - 1), -1, NA_TAB - 2) + 1 for j in range(NA_KBLK)]
            for qr in range(2)]
    qrow = lax.broadcasted_iota(jnp.int32, (QB, nloc), 0)
    kcol = lax.broadcasted_iota(jnp.int32, (QB, nloc), 1)
    krow = 2 * ws + jnp.right_shift(kcol, 6)
    rs_q = jnp.where(qrow < GRID_W, rs[0], rs[1])
    row_ok = jnp.logical_and(jnp.logical_and(krow >= rs_q, krow < rs_q + NB_KH), is_lat)
    row_ok2 = jnp.concatenate([row_ok, row_ok], axis=0)
    lo = _lane_lo((QB, LANES))
    zero = jnp.zeros((QB, LANES), BF16)
    kl = (k0, k1, k2, k3, k4)
    vl = (v0, v1, v2, v3, v4)

    def scores(hp):
        q2 = q_ref[hp]
        qs = jnp.concatenate([jnp.where(lo, q2, zero), jnp.where(lo, zero, q2)], axis=0)
        keys = jnp.concatenate([r[hp] for r in kl] + [kx_ref[hp]], axis=0)
        return lax.dot_general(qs, keys, (((1,), (1,)), ((), ())), preferred_element_type=F32)

    def probs(hp, s):
        bias = jnp.concatenate(
            [jnp.concatenate([tab_ref[2 * hp + hh, tidx[qr][j]] for j in range(NA_KBLK)], axis=1)
             for hh in range(2) for qr in range(2)], axis=0)
        s_loc = jnp.where(row_ok2, s[:, :nloc] + bias, NEG)
        s_ctx = s[:, nloc:]
        m = jnp.maximum(jnp.max(s_loc, axis=-1, keepdims=True), jnp.max(s_ctx, axis=-1, keepdims=True))
        e_loc = jnp.exp2(s_loc - m)
        e_ctx = jnp.exp2(s_ctx - m)
        denom = jnp.sum(e_loc, axis=-1, keepdims=True) + jnp.sum(e_ctx, axis=-1, keepdims=True)
        return jnp.concatenate([e_loc, e_ctx], axis=1).astype(BF16), denom

    def output(hp, p, denom):
        vals = jnp.concatenate([r[hp] for r in vl] + [vx_ref[hp]], axis=0)
        o = jnp.dot(p, vals, preferred_element_type=F32) / denom
        o_ref[hp] = jnp.where(lo, o[:QB], o[QB:]).astype(BF16)

    _pipeline_pairs(scores, probs, output)


def _attn_b_call(qkv3, table):
    n = qkv3.shape[1]
    nb = n // QB
    first_lat = CTX // QB
    nbl = nb - first_lat

    def win(blk, i):
        return pl.BlockSpec(
            (HP, QB, LANES),
            lambda b: (blk, first_lat + jnp.clip(jnp.maximum(b - first_lat, 0) - 2, 0, nbl - NA_KBLK) + i, 0))

    def kv_specs(blk):
        return [win(blk, i) for i in range(NA_KBLK)] + [pl.BlockSpec((HP, CTX, LANES), lambda b: (blk, 0, 0))]

    return pl.pallas_call(
        functools.partial(_attn_b_body, nbl=nbl),
        grid=(nb,),
        in_specs=[_resident((HEADS, NA_TAB, GRID_W, LANES), lambda b: (0, 0, 0, 0)),
                  pl.BlockSpec((HP, QB, LANES), lambda b: (0, b, 0))] + kv_specs(1) + kv_specs(2),
        out_specs=pl.BlockSpec((HP, QB, LANES), lambda b: (0, b, 0)),
        out_shape=jax.ShapeDtypeStruct((HP, n, LANES), BF16),
        compiler_params=_cparams(52),
        name="attn_neighbourhood",
    )(table, qkv3, *([qkv3] * 12))


def _oproj_router_body(*refs, mode):
    n_mix = 1 if mode == "attn" else 2
    w_ref, h_ref, mod_ref, g2_ref, wrh_ref, wrl_ref, br_ref, out_ref, f_ref, r_ref, cnt_ref, base_ref = refs[n_mix:]
    if mode == "attn":
        z = jnp.concatenate([refs[0][c] for c in range(HP)], axis=1)
    else:
        z = (jax.nn.gelu(refs[0][...]) * refs[1][...]).astype(BF16)
    y = jnp.dot(z, w_ref[...], preferred_element_type=F32)
    h_new = h_ref[...] + mod_ref[0, 2:3, :] * y
    out_ref[...] = h_new
    _route_tile(h_new, g2_ref[...], mod_ref[0, 3:4, :], mod_ref[0, 4:5, :], wrh_ref, wrl_ref, br_ref,
                f_ref, r_ref, cnt_ref, base_ref)


def _oproj_router_call(mix_in, w, h, mod, g2, wr, br, *, mode, name):
    n = h.shape[0]
    kdim = w.shape[0]
    if mode == "attn":
        in_specs = [pl.BlockSpec((HP, TM, LANES), lambda i: (0, i, 0))]
        args = [mix_in]
    else:
        u, hs = mix_in
        in_specs = [pl.BlockSpec((TM, D_RNN), lambda i: (i, 0)), pl.BlockSpec((TM, D_RNN), lambda i: (i, 0))]
        args = [u, hs]
    wr_hi = wr.astype(BF16)
    wr_lo = (wr - wr_hi.astype(F32)).astype(BF16)
    in_specs += [_resident((kdim, D), lambda i: (0, 0)),
                 pl.BlockSpec((TM, D), lambda i: (i, 0)),
                 pl.BlockSpec((1, 6, D), lambda i: (jnp.minimum(i, 1), 0, 0)),
                 pl.BlockSpec((1, D), lambda i: (0, 0)),
                 pl.BlockSpec((D, LANES), lambda i: (0, 0)),
                 pl.BlockSpec((D, LANES), lambda i: (0, 0)),
                 pl.BlockSpec((1, LANES), lambda i: (0, 0))]
    return pl.pallas_call(
        functools.partial(_oproj_router_body, mode=mode),
        grid=(n // TM,),
        in_specs=in_specs,
        out_specs=[pl.BlockSpec((TM, D), lambda i: (i, 0)),
                   pl.BlockSpec((TM * XROW_TILE, LANES), lambda i: (i, 0)),
                   pl.BlockSpec((TM, LANES), lambda i: (i, 0)),
                   pl.BlockSpec((1, LANES), lambda i: (0, 0))],
        out_shape=[jax.ShapeDtypeStruct((n, D), F32),
                   jax.ShapeDtypeStruct((n * XROW_TILE, LANES), jnp.uint32),
                   jax.ShapeDtypeStruct((n, LANES), F32),
                   jax.ShapeDtypeStruct((1, LANES), F32)],
        scratch_shapes=[pltpu.VMEM((1, LANES), F32)],
        compiler_params=_cparams(52),
        name=name,
    )(*args, w, h, mod, g2.reshape(1, D), wr_hi, wr_lo, br)


def _shift_rows(x, s, fill, reverse):
    t = x.shape[0]
    if s % SUBLANES == 0:
        pad = jnp.full((s, x.shape[1]), fill, x.dtype)
        return jnp.concatenate([x[s:], pad] if reverse else [pad, x[:t - s]], axis=0)
    row = lax.broadcasted_iota(jnp.int32, x.shape, 0)
    if reverse:
        return jnp.where(row >= t - s, fill, pltpu.roll(x, t - s, 0))
    return jnp.where(row < s, fill, pltpu.roll(x, s, 0))


def _chunk_scan(a, b, reverse):
    s = 1
    while s < a.shape[0]:
        a_s = _shift_rows(a, s, 1.0, reverse)
        b_s = _shift_rows(b, s, 0.0, reverse)
        b = a * b_s + b
        a = a * a_s
        s *= 2
    return a, b


def _group_scan(a, b, h_in, reverse, scr_ref):
    t, c = a.shape
    g = t // SUBLANES
    a3 = a.reshape(g, SUBLANES, c)
    b3 = b.reshape(g, SUBLANES, c)
    sub = lax.broadcasted_iota(jnp.int32, a3.shape, 1)
    s = 1
    while s < SUBLANES:
        edge = (sub >= SUBLANES - s) if reverse else (sub < s)
        shift = SUBLANES - s if reverse else s
        a_s = jnp.where(edge, 1.0, pltpu.roll(a3, shift, 1))
        b_s = jnp.where(edge, 0.0, pltpu.roll(b3, shift, 1))
        b3 = a3 * b_s + b3
        a3 = a3 * a_s
        s *= 2
    a_loc = a3.reshape(t, c)
    b_loc = b3.reshape(t, c)
    edge_row = 0 if reverse else SUBLANES - 1

    def boundary_rows(i, x):
        for j in range(c // LANES):
            scr_ref[i, j] = x[:, j * LANES:(j + 1) * LANES]
        return jnp.concatenate([scr_ref[i, j, pl.ds(edge_row, g, stride=SUBLANES), :] for j in range(c // LANES)],
                               axis=1)

    ga, gb = _chunk_scan(boundary_rows(0, a_loc), boundary_rows(1, b_loc), reverse)
    leaving = gb + ga * h_in
    entering = _shift_rows(leaving, 1, 0.0, reverse)
    grow = lax.broadcasted_iota(jnp.int32, leaving.shape, 0)
    entering = jnp.where(grow == (g - 1 if reverse else 0), h_in, entering)
    carry = jnp.broadcast_to(entering[:, None, :], (g, SUBLANES, c))
    h = (a3 * carry + b3).reshape(t, c)
    return h, (leaving[0:1, :] if reverse else leaving[g - 1:g, :])


def _lru_body(x_ref, cw_ref, cb_ref, wa_ref, wx_ref, ba_ref, bx_ref, lam_ref, o_ref, scr_ref, xc_ref, *, nchunks):
    n = nchunks * SCAN_TC
    halo = SUBLANES
    cw = cw_ref[...]
    cb = cb_ref[...]
    sp = jax.nn.softplus(-lam_ref[...])

    def conv_chunk(ci):
        r0 = pl.multiple_of(ci * SCAN_TC, SCAN_TC)
        lat_first = CTX // SCAN_TC
        prev_ok = jnp.logical_and(ci != 0, ci != lat_first).astype(F32)
        next_ok = jnp.logical_and(ci != lat_first - 1, ci != nchunks - 1).astype(F32)
        p0 = pl.multiple_of(jnp.maximum(r0 - halo, 0), halo)
        n0 = pl.multiple_of(jnp.minimum(r0 + SCAN_TC, n - halo), halo)
        xe = jnp.concatenate([x_ref[pl.ds(p0, halo), :] * prev_ok,
                              x_ref[pl.ds(r0, SCAN_TC), :],
                              x_ref[pl.ds(n0, halo), :] * next_ok], axis=0)
        y = cb
        for j in range(CONV_W):
            off = halo + j - CONV_LEFT
            y = y + xe[off:off + SCAN_TC, :] * cw[j:j + 1, :]
        return r0, y

    def direction(d, ci, h_in):
        if d == 0:
            r0, xc = conv_chunk(ci)
            xc_ref[pl.ds(r0, SCAN_TC), :] = xc
        else:
            r0 = pl.multiple_of(ci * SCAN_TC, SCAN_TC)
            xc = xc_ref[pl.ds(r0, SCAN_TC), :]
        xb = xc.astype(BF16)
        r = jax.nn.sigmoid(jnp.dot(xb, wa_ref[d, 0], preferred_element_type=F32) + ba_ref[d:d + 1, :])
        ig = jax.nn.sigmoid(jnp.dot(xb, wx_ref[d, 0], preferred_element_type=F32) + bx_ref[d:d + 1, :])
        log_a = -LRU_C * r * sp[d:d + 1, :]
        a = jnp.exp(log_a)
        bb = jnp.sqrt(1.0 - a * a) * (ig * xc)
        h, h_out = _group_scan(a, bb, h_in, d == 1, scr_ref)
        return r0, h, h_out

    def fwd(ci, h_in):
        r0, h, h_out = direction(0, ci, h_in)
        o_ref[pl.ds(r0, SCAN_TC), :] = h
        return h_out

    def bwd(ci, h_in):
        r0, h, h_out = direction(1, ci, h_in)
        o_ref[pl.ds(r0, SCAN_TC), :] += h
        return h_out

    h0 = jnp.zeros((1, RNN_BW), F32)
    lax.fori_loop(0, nchunks, fwd, h0)
    nctx = CTX // SCAN_TC
    hc = lax.fori_loop(0, nctx, lambda k, hh: bwd(nctx - 1 - k, hh), h0)
    lax.fori_loop(0, nchunks - nctx, lambda k, hh: bwd(nchunks - 1 - k, hh), hc)


def _lru_call(u, conv_w, conv_b, wa, wx, ba, bx, lam):
    n = u.shape[0]
    return pl.pallas_call(
        functools.partial(_lru_body, nchunks=n // SCAN_TC),
        grid=(RNN_BLOCKS,),
        in_specs=[_resident((n, RNN_BW), lambda m: (0, RNN_BLOCKS + m)),
                  pl.BlockSpec((CONV_W, RNN_BW), lambda m: (0, m)),
                  pl.BlockSpec((1, RNN_BW), lambda m: (0, m)),
                  pl.BlockSpec((2, 1, RNN_BW, RNN_BW), lambda m: (0, m, 0, 0)),
                  pl.BlockSpec((2, 1, RNN_BW, RNN_BW), lambda m: (0, m, 0, 0)),
                  pl.BlockSpec((2, RNN_BW), lambda m: (0, m)),
                  pl.BlockSpec((2, RNN_BW), lambda m: (0, m)),
                  pl.BlockSpec((2, RNN_BW), lambda m: (0, m))],
        out_specs=_resident((n, RNN_BW), lambda m: (0, m)),
        scratch_shapes=[pltpu.VMEM((2, RNN_BW // LANES, SCAN_TC, LANES), F32), pltpu.VMEM((n, RNN_BW), F32)],
        out_shape=jax.ShapeDtypeStruct((n, D_RNN), F32),
        compiler_params=_cparams(58),
        name="rglru_scan",
    )(u, conv_w, conv_b.reshape(1, D_RNN), wa, wx, ba, bx, lam)


def _pack_words(ref, lo, hi, first_tile):
    rows = lo.shape[0]
    lo = lax.bitcast_convert_type(lo.astype(BF16).astype(F32), jnp.uint32)
    hi = lax.bitcast_convert_type(hi.astype(BF16).astype(F32), jnp.uint32)
    w = hi | (lo >> 16)
    for s in range(lo.shape[1] // LANES):
        ref[pl.ds(first_tile + s, rows, stride=XROW_TILE), :] = w[:, s * LANES:(s + 1) * LANES]


def _pack_rows(ref, x):
    _pack_words(ref, x[:, :D // 2], x[:, D // 2:], 0)


def _unpack_rows(ref, rows, dtype):
    w = jnp.concatenate([ref[pl.ds(s, rows, stride=XROW_TILE), :] for s in range(XROW_TILE)], axis=1)
    lo = lax.bitcast_convert_type(w << 16, F32)
    hi = lax.bitcast_convert_type(w & jnp.uint32(0xFFFF0000), F32)
    return jnp.concatenate([lo, hi], axis=1).astype(dtype)


def _route_tile(h, g, shift, scale, wrh_ref, wrl_ref, br_ref, f_ref, r_ref, cnt_ref, base_ref):
    @pl.when(pl.program_id(0) == 0)
    def _():
        base_ref[...] = jnp.zeros(base_ref.shape, F32)

    f = _norm_mod(h, g, shift, scale)
    _pack_rows(f_ref, f)
    f_hi = f.astype(BF16)
    f_lo = (f - f_hi.astype(F32)).astype(BF16)
    logits = (jnp.dot(jnp.concatenate([f_hi, f_lo], axis=0), wrh_ref[...], preferred_element_type=F32).reshape(2, TM, LANES).sum(axis=0)
              + jnp.dot(f_hi, wrl_ref[...], preferred_element_type=F32) + br_ref[...])
    lane = lax.broadcasted_iota(jnp.int32, logits.shape, 1)
    ninf = -jnp.inf
    is_g = lane < N_GROUPS
    gl = jnp.where(is_g, logits, ninf)
    gmax = jnp.max(gl, axis=-1, keepdims=True)
    gsel = jnp.min(jnp.where(gl == gmax, lane, LANES), axis=-1, keepdims=True)
    gsum = jnp.sum(jnp.where(is_g, jnp.exp(gl - gmax), 0.0), axis=-1, keepdims=True)
    g_w = 1.0 / gsum
    e_lo = N_GROUPS + EXP_PER_GROUP * gsel
    in_grp = jnp.logical_and(lane >= e_lo, lane < e_lo + EXP_PER_GROUP)
    el = jnp.where(in_grp, logits, ninf)
    m1 = jnp.max(el, axis=-1, keepdims=True)
    i1 = jnp.min(jnp.where(el == m1, lane, LANES), axis=-1, keepdims=True)
    el2 = jnp.where(lane == i1, ninf, el)
    m2 = jnp.max(el2, axis=-1, keepdims=True)
    i2 = jnp.min(jnp.where(el2 == m2, lane, LANES), axis=-1, keepdims=True)
    t = jnp.exp(m2 - m1)
    w1 = g_w / (1.0 + t)
    w2 = g_w * t / (1.0 + t)
    oh1 = lane == i1
    oh2 = lane == i2
    rr = lax.broadcasted_iota(jnp.int32, (TM, TM), 0)
    cc = lax.broadcasted_iota(jnp.int32, (TM, TM), 1)
    tri = (cc < rr).astype(BF16)
    pre1 = jnp.dot(tri, oh1.astype(BF16), preferred_element_type=F32)
    pre2 = jnp.dot(tri, oh2.astype(BF16), preferred_element_type=F32)
    base = base_ref[...]
    cnt1 = jnp.sum(oh1.astype(F32), axis=0, keepdims=True)
    cnt2 = jnp.sum(oh2.astype(F32), axis=0, keepdims=True)
    rank1 = jnp.sum(jnp.where(oh1, pre1 + base, 0.0), axis=-1, keepdims=True)
    rank2 = jnp.sum(jnp.where(oh2, pre2 + (base + cnt1), 0.0), axis=-1, keepdims=True)
    total = base + cnt1 + cnt2
    base_ref[...] = total
    cnt_ref[...] = total
    cols = [(i1 - N_GROUPS).astype(F32), (i2 - N_GROUPS).astype(F32), w1, w2, rank1, rank2]
    out = jnp.zeros(logits.shape, F32)
    for k, v in enumerate(cols):
        out = jnp.where(lane == k, v, out)
    r_ref[...] = out


def _dispatch_body(dest_ref, pad_lo_ref, pad_n_ref, f_ref, xg_ref, z_ref, sem, blk_sem):
    i = pl.program_id(0)
    base = i * TM

    def issue(j, carry):
        for k in range(TOP_K):
            d = dest_ref[(base + j) * TOP_K + k]
            pltpu.make_async_copy(f_ref.at[pl.ds(j * XROW_TILE, XROW_TILE)],
                                  xg_ref.at[pl.ds(d * XROW_TILE, XROW_TILE)], sem).start(priority=k)
        return carry

    z_row = z_ref.at[pl.ds(0, XROW_TILE)]

    def wait_one(carry):
        pltpu.make_async_copy(z_row, xg_ref.at[pl.ds(0, XROW_TILE)], sem).wait()
        return carry

    lax.fori_loop(0, TM, issue, 0, unroll=DMA_UNROLL)

    @pl.when(i == 0)
    def _():
        z_ref[...] = jnp.zeros(z_ref.shape, jnp.uint32)

        def fill(e, carry):
            lo = pad_lo_ref[e]
            n_fill = pad_n_ref[e]

            def batch(bi, c):
                s0 = bi * MOE_MB
                cnt = jnp.minimum(n_fill - s0, MOE_MB)

                def one(s, c2):
                    pltpu.make_async_copy(z_row, xg_ref.at[pl.ds((lo + s0 + s) * XROW_TILE, XROW_TILE)], sem).start()
                    return c2

                lax.fori_loop(0, cnt, one, 0)
                lax.fori_loop(0, cnt, lambda s, c2: wait_one(c2), 0)
                return c

            lax.fori_loop(0, (n_fill + (MOE_MB - 1)) // MOE_MB, batch, 0)
            return carry

        lax.fori_loop(0, N_EXPERTS, fill, 0)

        def block_copy(bi):
            row0 = (pad_lo_ref[N_EXPERTS] + bi * MOE_MB) * XROW_TILE
            return pltpu.make_async_copy(z_ref, xg_ref.at[pl.ds(row0, MOE_MB * XROW_TILE)], blk_sem)

        n_tail = pad_n_ref[N_EXPERTS] // MOE_MB
        lax.fori_loop(0, n_tail, lambda bi, c: (block_copy(bi).start(), c)[1], 0)
        lax.fori_loop(0, n_tail, lambda bi, c: (block_copy(bi).wait(), c)[1], 0)

    lax.fori_loop(0, TM * TOP_K, lambda j, c: wait_one(c), 0, unroll=DMA_UNROLL)


def _dispatch_call(dest, pad_lo, pad_n, f2, p):
    n = f2.shape[0] // XROW_TILE
    return pl.pallas_call(
        _dispatch_body,
        grid_spec=pltpu.PrefetchScalarGridSpec(
            num_scalar_prefetch=3,
            grid=(n // TM,),
            in_specs=[pl.BlockSpec((TM * XROW_TILE, LANES), lambda i, d, lo, nn: (i, 0))],
            out_specs=pl.BlockSpec(memory_space=pl.ANY),
            scratch_shapes=[pltpu.VMEM((MOE_MB * XROW_TILE, LANES), jnp.uint32), pltpu.SemaphoreType.DMA(()),
                            pltpu.SemaphoreType.DMA(())]),
        out_shape=jax.ShapeDtypeStruct((p * XROW_TILE, LANES), jnp.uint32),
        compiler_params=_cparams(16),
        name="moe_dispatch",
    )(dest, pad_lo, pad_n, f2)


def _expert_weights(b, be_ref, ne_ref, nu_ref, w_hbm, wf_ref, wb_ref, sem, slot_ref, layer):
    e = be_ref[b]
    first = jnp.logical_or(b == 0, e != be_ref[jnp.maximum(b - 1, 0)])

    def fetch(expert, slot):
        return pltpu.make_async_copy(w_hbm.at[layer, expert], wf_ref.at[slot], sem.at[slot])

    @pl.when(b == 0)
    def _():
        slot_ref[0] = 0
        fetch(e, 0).start()

    @pl.when(jnp.logical_and(first, b < nu_ref[0]))
    def _():
        s = slot_ref[0]
        fetch(e, s).wait()
        wb_ref[...] = wf_ref[s].astype(BF16)
        ne = ne_ref[b]

        @pl.when(ne != e)
        def _():
            fetch(ne, 1 - s).start()

        slot_ref[0] = 1 - s


def _ffn_up_body(be_ref, ne_ref, nu_ref, x_ref, w_hbm, a_ref, wf_ref, wb_ref, sem, slot_ref, *, layer):
    b = pl.program_id(0)
    _expert_weights(b, be_ref, ne_ref, nu_ref, w_hbm, wf_ref, wb_ref, sem, slot_ref, layer)

    @pl.when(b < nu_ref[0])
    def _():
        x = _unpack_rows(x_ref, MOE_MB, BF16)
        cw = FFN_CHUNK
        n_chunks = D_EXPERT // cw

        def gate_up(c):
            return (jnp.dot(x, wb_ref[:, c * cw:(c + 1) * cw], preferred_element_type=F32),
                    jnp.dot(x, wb_ref[:, D_EXPERT + c * cw:D_EXPERT + (c + 1) * cw], preferred_element_type=F32))

        nxt = gate_up(0)
        for c in range(n_chunks):
            g, u = nxt
            if c + 1 < n_chunks:
                nxt = gate_up(c + 1)
            a_ref[:, c * cw:(c + 1) * cw] = (g * jax.nn.sigmoid(g) * u).astype(BF16)

    @pl.when(b >= nu_ref[0])
    def _():
        a_ref[...] = jnp.zeros(a_ref.shape, BF16)


def _expert_scratch(rows, cols):
    return [pltpu.VMEM((2, rows, cols), F32), pltpu.VMEM((rows, cols), BF16),
            pltpu.SemaphoreType.DMA((2,)), pltpu.SMEM((1,), jnp.int32)]


def _ffn_up_call(block_e, next_e, n_used, xg, w_gu, layer):
    p = xg.shape[0] // XROW_TILE
    return pl.pallas_call(
        functools.partial(_ffn_up_body, layer=layer),
        grid_spec=pltpu.PrefetchScalarGridSpec(
            num_scalar_prefetch=3,
            grid=(p // MOE_MB,),
            in_specs=[pl.BlockSpec((MOE_MB * XROW_TILE, LANES),
                                   lambda b, be, ne, nu: (jnp.minimum(b, nu[0] - 1), 0)),
                      pl.BlockSpec(memory_space=pl.ANY)],
            out_specs=pl.BlockSpec((MOE_MB, D_EXPERT), lambda b, be, ne, nu: (b, 0)),
            scratch_shapes=_expert_scratch(D, 2 * D_EXPERT)),
        out_shape=jax.ShapeDtypeStruct((p, D_EXPERT), BF16),
        compiler_params=_cparams(48),
        name="moe_ffn_up",
    )(block_e, next_e, n_used, xg, w_gu)


def _ffn_down_body(be_ref, ne_ref, nu_ref, a_ref, w_hbm, y_ref, wf_ref, wb_ref, sem, slot_ref, *, layer):
    b = pl.program_id(0)
    _expert_weights(b, be_ref, ne_ref, nu_ref, w_hbm, wf_ref, wb_ref, sem, slot_ref, layer)

    @pl.when(b < nu_ref[0])
    def _():
        a = a_ref[...]
        cw = FFN_CHUNK
        n_chunks = D // 2 // cw

        def halves(c):
            return (jnp.dot(a, wb_ref[:, c * cw:(c + 1) * cw], preferred_element_type=F32),
                    jnp.dot(a, wb_ref[:, D // 2 + c * cw:D // 2 + (c + 1) * cw], preferred_element_type=F32))

        nxt = halves(0)
        for c in range(n_chunks):
            lo, hi = nxt
            if c + 1 < n_chunks:
                nxt = halves(c + 1)
            _pack_words(y_ref, lo, hi, c * (cw // LANES))

    @pl.when(b >= nu_ref[0])
    def _():
        y_ref[...] = jnp.zeros(y_ref.shape, jnp.uint32)


def _ffn_down_call(block_e, next_e, n_used, act, w_down, layer):
    p = act.shape[0]
    return pl.pallas_call(
        functools.partial(_ffn_down_body, layer=layer),
        grid_spec=pltpu.PrefetchScalarGridSpec(
            num_scalar_prefetch=3,
            grid=(p // MOE_MB,),
            in_specs=[pl.BlockSpec((MOE_MB, D_EXPERT), lambda b, be, ne, nu: (b, 0)),
                      pl.BlockSpec(memory_space=pl.ANY)],
            out_specs=pl.BlockSpec((MOE_MB * XROW_TILE, LANES), lambda b, be, ne, nu: (b, 0)),
            scratch_shapes=_expert_scratch(D_EXPERT, D)),
        out_shape=jax.ShapeDtypeStruct((p * XROW_TILE, LANES), jnp.uint32),
        compiler_params=_cparams(32),
        name="moe_ffn_down",
    )(block_e, next_e, n_used, act, w_down)


def _combine_body(dest_ref, h_ref, r_ref, mod_ref, g_ref, yb_ref, o_ref, buf_ref, sem, *, final):
    i = pl.program_id(0)
    slot = i % 2

    def gather(tile, dst_slot):
        base = tile * TM

        def issue(j, carry):
            for k in range(TOP_K):
                d = dest_ref[(base + j) * TOP_K + k]
                pltpu.make_async_copy(yb_ref.at[pl.ds(d * XROW_TILE, XROW_TILE)],
                                      buf_ref.at[dst_slot, k, pl.ds(j * XROW_TILE, XROW_TILE)],
                                      sem.at[dst_slot]).start(priority=k)
            return carry

        lax.fori_loop(0, TM, issue, 0, unroll=DMA_UNROLL)

    @pl.when(i == 0)
    def _():
        gather(0, 0)

    @pl.when(i + 1 < pl.num_programs(0))
    def _():
        gather(i + 1, 1 - slot)

    def drain(j, carry):
        pltpu.make_async_copy(yb_ref.at[pl.ds(0, XROW_TILE)], buf_ref.at[slot, 0, pl.ds(0, XROW_TILE)],
                              sem.at[slot]).wait()
        return carry

    lax.fori_loop(0, TM * TOP_K, drain, 0, unroll=DMA_UNROLL)
    r = r_ref[...]
    y = sum(r[:, TOP_K + k:TOP_K + k + 1] * _unpack_rows(buf_ref.at[slot, k], TM, F32) for k in range(TOP_K))
    h_new = h_ref[...] + mod_ref[0, 2:3, :] * y
    if final:
        ms = jnp.mean(h_new * h_new, axis=-1, keepdims=True)
        h_new = h_new * lax.rsqrt(ms + EPS) * g_ref[...]
    o_ref[...] = h_new


def _combine_call(dest, h, route, mod, yb, final_g=None):
    n = h.shape[0]
    final = final_g is not None
    first = CTX // TM
    out_rows = n - CTX if final else n
    out_map = (lambda i, d: (jnp.maximum(i - first, 0), 0)) if final else (lambda i, d: (i, 0))
    g = final_g if final else jnp.ones((D,), F32)
    return pl.pallas_call(
        functools.partial(_combine_body, final=final),
        grid_spec=pltpu.PrefetchScalarGridSpec(
            num_scalar_prefetch=1,
            grid=(n // TM,),
            in_specs=[pl.BlockSpec((TM, D), lambda i, d: (i, 0)),
                      pl.BlockSpec((TM, LANES), lambda i, d: (i, 0)),
                      pl.BlockSpec((1, 3, D), lambda i, d: (jnp.minimum(i, 1), 0, 0)),
                      pl.BlockSpec((1, D), lambda i, d: (0, 0)),
                      pl.BlockSpec(memory_space=pl.ANY)],
            out_specs=pl.BlockSpec((TM, D), out_map),
            scratch_shapes=[pltpu.VMEM((2, TOP_K, TM * XROW_TILE, LANES), jnp.uint32),
                            pltpu.SemaphoreType.DMA((2,))]),
        out_shape=jax.ShapeDtypeStruct((out_rows, D), F32),
        compiler_params=_cparams(32),
        name="moe_combine",
    )(dest, h, route, mod[:, 3:6], g.reshape(1, D), yb)


def _dispatch_plan(route, cnt, n):
    experts = jnp.arange(N_EXPERTS, dtype=jnp.int32)
    counts = cnt[0, N_GROUPS:N_GROUPS + N_EXPERTS].astype(jnp.int32)
    padded = (counts + MOE_MB - 1) // MOE_MB * MOE_MB
    end_pad = jnp.cumsum(padded)
    start_pad = end_pad - padded
    n_blocks = -(-(n * TOP_K + N_EXPERTS * (MOE_MB - 1)) // MOE_MB)
    p = n_blocks * MOE_MB
    blk_start = jnp.arange(n_blocks, dtype=jnp.int32) * MOE_MB
    block_e = jnp.minimum(jnp.sum(end_pad[None, :] <= blk_start[:, None], axis=1), N_EXPERTS - 1).astype(jnp.int32)
    n_used = (end_pad[-1:] // MOE_MB).astype(jnp.int32)
    group_end = end_pad[block_e] // MOE_MB
    next_e = jnp.where(group_end < n_used[0], block_e[jnp.minimum(group_end, n_blocks - 1)], block_e)
    e_idx = route[:, 0:TOP_K].astype(jnp.int32)
    rank = route[:, 2 * TOP_K:3 * TOP_K].astype(jnp.int32)
    start = jnp.sum(jnp.where(e_idx[..., None] == experts, start_pad, 0), axis=-1)
    dest = (start + rank).reshape(n * TOP_K)
    pad_lo = jnp.concatenate([start_pad + counts, end_pad[-1:]]).astype(jnp.int32)
    pad_n = jnp.concatenate([padded - counts, p - end_pad[-1:]]).astype(jnp.int32)
    return p, block_e, next_e, n_used, dest, pad_lo, pad_n


def _moe_layer(h, f2, route, cnt, mod, w_gu, w_down, layer, final_g=None):
    n = h.shape[0]
    p, block_e, next_e, n_used, dest, pad_lo, pad_n = _dispatch_plan(route, cnt, n)
    xg = _dispatch_call(dest, pad_lo, pad_n, f2, p)
    act = _ffn_up_call(block_e, next_e, n_used, xg, w_gu, layer)
    yb = _ffn_down_call(block_e, next_e, n_used, act, w_down, layer)
    return _combine_call(dest, h, route, mod, yb, final_g)


def _rope_tables(l):
    quarter = HEAD_DIM // 4
    inv = ROPE_BASE ** (-jnp.arange(quarter, dtype=F32) / quarter)
    n_rows = l // GRID_W
    ang_r = jnp.arange(n_rows, dtype=F32)[:, None] * inv
    ang_c = jnp.arange(GRID_W, dtype=F32)[:, None] * inv
    by_row = lambda t: jnp.broadcast_to(t[:, None, :], (n_rows, GRID_W, quarter)).reshape(l, quarter)
    by_col = lambda t: jnp.broadcast_to(t[None, :, :], (n_rows, GRID_W, quarter)).reshape(l, quarter)
    cr, sr, cc, sc = by_row(jnp.cos(ang_r)), by_row(jnp.sin(ang_r)), by_col(jnp.cos(ang_c)), by_col(jnp.sin(ang_c))
    cos = jnp.concatenate([cr, cr, cc, cc], axis=1)
    sin = jnp.concatenate([-sr, sr, -sc, sc], axis=1)
    cos = jnp.concatenate([jnp.ones((CTX, HEAD_DIM), F32), cos], axis=0)
    sin = jnp.concatenate([jnp.zeros((CTX, HEAD_DIM), F32), sin], axis=0)
    return jnp.tile(cos, (1, 2)), jnp.tile(sin, (1, 2))


def _attn_a_weights(w_qkv):
    nq = HEADS * HEAD_DIM
    nkv = A_KV_HEADS * HEAD_DIM
    wq = w_qkv[:, :nq]
    wk = w_qkv[:, nq:nq + nkv].reshape(D, A_KV_HEADS, 1, HEAD_DIM)
    wv = w_qkv[:, nq + nkv:].reshape(D, A_KV_HEADS, 1, HEAD_DIM)
    dup = lambda w: jnp.broadcast_to(w, (D, A_KV_HEADS, 2, HEAD_DIM)).reshape(D, 2 * nkv)
    return jnp.concatenate([wq, dup(wk), dup(wv)], axis=1).astype(BF16)


def kernel(x, c, ctx, c_ctx, ada_w, ada_b, norm_mix_g, norm_ffn_g, router_group_w, router_group_b,
           router_expert_w, router_expert_b, moe_w_gu, moe_w_down, attn_w_qkv, attn_w_o, attn_sink,
           na_w_qkv, na_w_o, na_rpb, rnn_w_in, rnn_conv_w, rnn_conv_b, rnn_wa, rnn_ba, rnn_wx, rnn_bx,
           rnn_lam, rnn_w_out, final_norm_g):
    batch, l, _ = x.shape
    assert batch == 1 and ctx.shape[1] == CTX and l % (QB * 2) == 0 and l // QB >= NA_KBLK
    h = jnp.concatenate([ctx[0], x[0]], axis=0)
    c2 = jnp.stack([c_ctx, c[0]], axis=1)
    mods = _ada_call(c2, ada_w, ada_b).reshape(DEPTH, 2, 6, D)
    rope = _rope_tables(l)
    pad_r = LANES - N_GROUPS - N_EXPERTS
    for i in range(DEPTH):
        kind, j = i % 3, i // 3
        mod = mods[i]
        wr = jnp.concatenate([router_group_w[i], router_expert_w[i], jnp.zeros((D, pad_r), F32)], axis=1)
        br = jnp.concatenate([router_group_b[i], router_expert_b[i], jnp.zeros((pad_r,), F32)]).reshape(1, LANES)
        if kind == 0:
            w = _attn_a_weights(attn_w_qkv[j])
            nq = HEADS * HEAD_DIM
            qkv3 = _proj_call(h, norm_mix_g[i], mod, w, mode="rope", rope=rope,
                              n_rope=nq + 2 * A_KV_HEADS * HEAD_DIM, n_q=nq, name="proj_window")
            mix, w_out, mode, name = _attn_a_call(qkv3, attn_sink[j]), attn_w_o[j], "attn", "oproj_window"
        elif kind == 1:
            qkv3 = _proj_call(h, norm_mix_g[i], mod, na_w_qkv[j].astype(BF16), mode="cols",
                              n_q=HEADS * HEAD_DIM, name="proj_neighbourhood")
            mix, w_out, mode, name = (_attn_b_call(qkv3, _na_table_call(na_rpb[j])), na_w_o[j], "attn",
                                      "oproj_neighbourhood")
        else:
            u = _proj_call(h, norm_mix_g[i], mod, rnn_w_in[j].astype(BF16), mode="plain", name="proj_rglru")
            hs = _lru_call(u, rnn_conv_w[j], rnn_conv_b[j], rnn_wa[j].astype(BF16), rnn_wx[j].astype(BF16),
                           rnn_ba[j], rnn_bx[j], rnn_lam[j])
            mix, w_out, mode, name = (u, hs), rnn_w_out[j], "rnn", "oproj_rglru"
        h, f2, route, cnt = _oproj_router_call(mix, w_out.astype(BF16), h, mod, norm_ffn_g[i], wr, br,
                                               mode=mode, name=name)
        h = _moe_layer(h, f2, route, cnt, mod, moe_w_gu, moe_w_down, i,
                       final_norm_g if i == DEPTH - 1 else None)
    return h[None]
```

```python
import functools

import jax
import jax.numpy as jnp
from jax import lax
from jax.experimental import pallas as pl
from jax.experimental.pallas import tpu as pltpu

F32 = jnp.float32
BF16 = jnp.bfloat16

D = 2048
DEPTH = 4
GRID_W = 64
CTX = 256
HEADS = 32
HEAD_DIM = 64
A_KV_HEADS = 4
A_GROUP = HEADS // A_KV_HEADS
WINDOW = 128
NB_KH = 8
NB_KW = 16
D_RNN = 2560
RNN_BLOCKS = 10
RNN_BW = D_RNN // RNN_BLOCKS
CONV_W = 4
CONV_LEFT = 2
LRU_C = 8.0
N_GROUPS = 4
EXP_PER_GROUP = 8
N_EXPERTS = N_GROUPS * EXP_PER_GROUP
TOP_K = 2
D_EXPERT = 768
ROPE_BASE = 10000.0
EPS = 1e-6
NEG = -1e30

LANES = 128
SUBLANES = 8
MIB = 1024 * 1024

TM = 256
QB = 128
HP = HEADS // 2
MOE_MB = 512
ROW_TILE = D // LANES
XROW_TILE = ROW_TILE // 2
DMA_UNROLL = 8
FFN_CHUNK = 256
FILL_RUN = 64
SCAN_TC = 256
NA_KBLK = 5
SQRT_SCALE = HEAD_DIM ** -0.5
LOG2E = 1.4426950408889634


def _cparams(vmem_mib, sem=("arbitrary",)):
    return pltpu.CompilerParams(dimension_semantics=sem, vmem_limit_bytes=int(vmem_mib * MIB))


def _resident(block_shape, index_map):
    return pl.BlockSpec(block_shape, index_map, pipeline_mode=pl.Buffered(1))


def _mod_spec():
    return pl.BlockSpec((1, 3, D), lambda i: (jnp.minimum(i, 1), 0, 0))


def _stream_specs(h):
    if isinstance(h, tuple):
        first = CTX // TM
        return [pl.BlockSpec((TM, D), lambda i: (0, 0)),
                pl.BlockSpec((TM, D), lambda i: (jnp.maximum(i - first, 0), 0))], list(h)
    return [pl.BlockSpec((TM, D), lambda i: (i, 0))], [h]


def _stream_tile(refs):
    if len(refs) == 2:
        return jnp.where(pl.program_id(0) < CTX // TM, refs[0][...], refs[1][...])
    return refs[0][...]


def _stream_rows(h):
    return sum(a.shape[0] for a in h) if isinstance(h, tuple) else h.shape[0]


def _norm_mod(x, g, shift, scale):
    ms = jnp.mean(x * x, axis=-1, keepdims=True)
    y = x * lax.rsqrt(ms + EPS) * g
    return y * (1.0 + scale) + shift


ADA_TN = 1024


def _ada_body(c_ref, w_ref, b_ref, o_ref):
    c = c_ref[...]
    cs = c * jax.nn.sigmoid(c)
    for r in range(2):
        cb = jnp.broadcast_to(cs[:, r:r + 1], (D, LANES))
        outs = []
        for j in range(ADA_TN // LANES):
            w = w_ref[0, :, j * LANES:(j + 1) * LANES]
            p = (w * cb).reshape(D // SUBLANES, SUBLANES, LANES).sum(axis=0)
            outs.append(p.sum(axis=0, keepdims=True))
        o_ref[0, r:r + 1, :] = jnp.concatenate(outs, axis=1) + b_ref[0]


def _ada_call(c2, ada_w, ada_b):
    return pl.pallas_call(
        _ada_body,
        grid=(DEPTH, 6 * D // ADA_TN),
        in_specs=[pl.BlockSpec((D, 2), lambda l, j: (0, 0)),
                  pl.BlockSpec((1, D, ADA_TN), lambda l, j: (l, 0, j)),
                  pl.BlockSpec((1, 1, ADA_TN), lambda l, j: (l, 0, j))],
        out_specs=pl.BlockSpec((1, 2, ADA_TN), lambda l, j: (l, 0, j)),
        out_shape=jax.ShapeDtypeStruct((DEPTH, 2, 6 * D), F32),
        compiler_params=_cparams(32, ("arbitrary", "arbitrary")),
        name="ada_mod",
    )(c2, ada_w, ada_b.reshape(DEPTH, 1, 6 * D))


PROJ_CH = 512


def _rope_piece(piece, cos, sin):
    lane = lax.broadcasted_iota(jnp.int32, piece.shape, 1)
    first = (lane & 16) == 0
    partner = jnp.where(first, pltpu.roll(piece, LANES - 16, 1), pltpu.roll(piece, 16, 1))
    return piece * cos + partner * sin


def _proj_body(*refs, mode, n_rope, n_q, n_stream):
    x_refs, refs = refs[:n_stream], refs[n_stream:]
    if mode == "rope":
        g_ref, mod_ref, w_ref, cos_ref, sin_ref, o_ref = refs
    else:
        g_ref, mod_ref, w_ref, o_ref = refs
    a = _norm_mod(_stream_tile(x_refs), g_ref[...], mod_ref[0, 0:1, :], mod_ref[0, 1:2, :]).astype(BF16)
    nout = w_ref.shape[1]
    for c in range(nout // PROJ_CH):
        acc = jnp.dot(a, w_ref[:, c * PROJ_CH:(c + 1) * PROJ_CH], preferred_element_type=F32)
        if mode == "plain":
            o_ref[:, c * PROJ_CH:(c + 1) * PROJ_CH] = acc
            continue
        for k in range(PROJ_CH // LANES):
            col0 = c * PROJ_CH + k * LANES
            piece = acc[:, k * LANES:(k + 1) * LANES]
            if col0 < n_rope:
                piece = _rope_piece(piece, cos_ref[...], sin_ref[...])
            if col0 < n_q:
                piece = piece * (SQRT_SCALE * LOG2E)
            o_ref[col0 // LANES] = piece.astype(BF16)


def _proj_call(h, g, mod, w, *, mode, rope=None, n_rope=0, n_q=0, name):
    n = _stream_rows(h)
    nout = w.shape[1]
    in_specs, args = _stream_specs(h)
    n_stream = len(args)
    in_specs += [pl.BlockSpec((1, D), lambda i: (0, 0)),
                 _mod_spec(),
                 _resident((D, nout), lambda i: (0, 0))]
    args += [g.reshape(1, D), mod[:, 0:3], w]
    if mode == "rope":
        in_specs += [pl.BlockSpec((TM, LANES), lambda i: (i, 0))] * 2
        args += list(rope)
    if mode == "plain":
        out_spec = pl.BlockSpec((TM, nout), lambda i: (i, 0))
        out_shape = jax.ShapeDtypeStruct((n, nout), F32)
        out_bytes = TM * nout * 4
    else:
        out_spec = pl.BlockSpec((nout // LANES, TM, LANES), lambda i: (0, i, 0))
        out_shape = jax.ShapeDtypeStruct((nout // LANES, n, LANES), BF16)
        out_bytes = TM * nout * 2
    vmem = (D * nout * 2 + 2 * TM * D * 4 + 2 * out_bytes) / MIB + 12
    return pl.pallas_call(
        functools.partial(_proj_body, mode=mode, n_rope=n_rope, n_q=n_q, n_stream=n_stream),
        grid=(n // TM,),
        in_specs=in_specs,
        out_specs=out_spec,
        out_shape=out_shape,
        compiler_params=_cparams(vmem),
        name=name,
    )(*args)


def _lane_lo(shape):
    return lax.broadcasted_iota(jnp.int32, shape, len(shape) - 1) < HEAD_DIM


def _pipeline_pairs(scores, probs, output):
    s_next = scores(0)
    pending = None
    for hp in range(HP):
        s_cur = s_next
        if hp + 1 < HP:
            s_next = scores(hp + 1)
        cur = probs(hp, s_cur)
        if pending is not None:
            output(hp - 1, *pending)
        pending = cur
    output(HP - 1, *pending)


def _attn_a_body(sink_ref, q_ref, kp_ref, kc_ref, kn_ref, kx_ref, vp_ref, vc_ref, vn_ref, vx_ref, o_ref, *, nb):
    b = pl.program_id(0)
    first_lat = CTX // QB
    is_lat = b >= first_lat
    prev_ok = b >= first_lat + 1
    next_ok = jnp.logical_and(is_lat, b <= nb - 2)
    nloc = 3 * QB
    nkeys = nloc + CTX
    qi = lax.broadcasted_iota(jnp.int32, (QB, nkeys), 0)
    kj = lax.broadcasted_iota(jnp.int32, (QB, nkeys), 1)
    rel = qi + WINDOW - kj
    band = jnp.abs(rel) <= WINDOW
    seg_ok = jnp.where(kj < QB, prev_ok.astype(jnp.int32),
                       jnp.where(kj < 2 * QB, is_lat.astype(jnp.int32), next_ok.astype(jnp.int32)))
    ok = jnp.logical_or(kj >= nloc, jnp.logical_and(band, seg_ok > 0))
    mask_add = jnp.where(ok, 0.0, NEG)
    lo = _lane_lo((QB, LANES))
    zero = jnp.zeros((QB, LANES), BF16)
    def scores(hp):
        kvh = hp // (A_GROUP // 2)
        keys = jnp.concatenate([kp_ref[kvh], kc_ref[kvh], kn_ref[kvh], kx_ref[kvh]], axis=0)
        q2 = q_ref[hp]
        qs = jnp.concatenate([jnp.where(lo, q2, zero), jnp.where(lo, zero, q2)], axis=0)
        return lax.dot_general(qs, keys, (((1,), (1,)), ((), ())), preferred_element_type=F32)

    def probs(hp, s):
        sink = jnp.concatenate([jnp.full((1, QB, 1), sink_ref[2 * hp + hh] * LOG2E, F32) for hh in range(2)], axis=0)
        s3 = s.reshape(2, QB, nkeys) + mask_add[None]
        m = jnp.maximum(jnp.max(s3, axis=-1, keepdims=True), sink)
        e = jnp.exp2(s3 - m)
        denom = jnp.sum(e, axis=-1, keepdims=True) + jnp.exp2(sink - m)
        return e.astype(BF16).reshape(2 * QB, nkeys), denom

    def output(hp, e, denom):
        kvh = hp // (A_GROUP // 2)
        vals = jnp.concatenate([vp_ref[kvh], vc_ref[kvh], vn_ref[kvh], vx_ref[kvh]], axis=0)
        o = jnp.dot(e, vals, preferred_element_type=F32).reshape(2, QB, LANES) / denom
        o_ref[hp] = jnp.where(lo, o[0], o[1]).astype(BF16)

    _pipeline_pairs(scores, probs, output)


def _attn_a_call(qkv3, sink):
    n = qkv3.shape[1]
    nb = n // QB
    first_lat = CTX // QB
    kblk, vblk = HP // A_KV_HEADS, HP // A_KV_HEADS + 1

    def kv_specs(blk):
        return [pl.BlockSpec((A_KV_HEADS, QB, LANES), lambda b: (blk, jnp.maximum(b - 1, first_lat), 0)),
                pl.BlockSpec((A_KV_HEADS, QB, LANES), lambda b: (blk, b, 0)),
                pl.BlockSpec((A_KV_HEADS, QB, LANES), lambda b: (blk, jnp.minimum(b + 1, nb - 1), 0)),
                pl.BlockSpec((A_KV_HEADS, CTX, LANES), lambda b: (blk, 0, 0))]

    return pl.pallas_call(
        functools.partial(_attn_a_body, nb=nb),
        grid=(nb,),
        in_specs=[pl.BlockSpec(memory_space=pltpu.SMEM),
                  pl.BlockSpec((HP, QB, LANES), lambda b: (0, b, 0))] + kv_specs(kblk) + kv_specs(vblk),
        out_specs=pl.BlockSpec((HP, QB, LANES), lambda b: (0, b, 0)),
        out_shape=jax.ShapeDtypeStruct((HP, n, LANES), BF16),
        compiler_params=_cparams(40),
        name="attn_window",
    )(sink, qkv3, *([qkv3] * 8))


NA_TAB = 2 * NB_KH
RPB_W = 2 * NB_KW - 1
RPB_H = 2 * NB_KH - 1


def _na_table_body(rpb_ref, o_ref):
    h = pl.program_id(0)
    shape = (GRID_W, LANES)
    c = lax.broadcasted_iota(jnp.int32, shape, 0)
    lane = lax.broadcasted_iota(jnp.int32, shape, 1)
    kc = lane & (GRID_W - 1)
    hi = lane >= GRID_W
    cs = jnp.clip(c - NB_KW // 2, 0, GRID_W - NB_KW)
    colok = jnp.logical_and(kc >= cs, kc < cs + NB_KW)
    diff = kc - c + (NB_KW - 1)
    neg = jnp.full(shape, NEG, F32)
    rows = [neg]
    for d in range(RPB_H):
        acc = neg
        for j in range(RPB_W):
            acc = jnp.where(diff == j, rpb_ref[h * (RPB_H * RPB_W) + d * RPB_W + j] * LOG2E, acc)
        rows.append(jnp.where(colok, acc, NEG))
    rows.append(neg)
    for t in range(NA_TAB):
        o_ref[0, t] = jnp.where(hi, rows[t + 1], rows[t])


def _na_table_call(rpb):
    return pl.pallas_call(
        _na_table_body,
        grid=(HEADS,),
        in_specs=[pl.BlockSpec(memory_space=pltpu.SMEM)],
        out_specs=pl.BlockSpec((1, NA_TAB, GRID_W, LANES), lambda h: (h, 0, 0, 0)),
        out_shape=jax.ShapeDtypeStruct((HEADS, NA_TAB, GRID_W, LANES), F32),
        compiler_params=_cparams(16),
        name="na_bias_table",
    )(rpb.reshape(-1))


def _attn_b_body(tab_ref, q_ref, k0, k1, k2, k3, k4, kx_ref, v0, v1, v2, v3, v4, vx_ref, o_ref, *, nbl):
    b = pl.program_id(0)
    first_lat = CTX // QB
    is_lat = b >= first_lat
    bl = jnp.maximum(b - first_lat, 0)
    ws = jnp.clip(bl - 2, 0, nbl - NA_KBLK)
    n_rows = 2 * nbl
    nloc = NA_KBLK * QB
    rs = [jnp.clip(2 * bl + qr - NB_KH // 2, 0, n_rows - NB_KH) for qr in range(2)]
    tidx = [[jnp.clip(2 * (ws + j) - (2 * bl + qr) + (NB_KH - 1), -1, NA_TAB - 2) + 1 for j in range(NA_KBLK)]
            for qr in range(2)]
    qrow = lax.broadcasted_iota(jnp.int32, (QB, nloc), 0)
    kcol = lax.broadcasted_iota(jnp.int32, (QB, nloc), 1)
    krow = 2 * ws + jnp.right_shift(kcol, 6)
    rs_q = jnp.where(qrow < GRID_W, rs[0], rs[1])
    row_ok = jnp.logical_and(jnp.logical_and(krow >= rs_q, krow < rs_q + NB_KH), is_lat)
    row_ok2 = jnp.concatenate([row_ok, row_ok], axis=0)
    lo = _lane_lo((QB, LANES))
    zero = jnp.zeros((QB, LANES), BF16)
    kl = (k0, k1, k2, k3, k4)
    vl = (v0, v1, v2, v3, v4)

    def scores(hp):
        q2 = q_ref[hp]
        qs = jnp.concatenate([jnp.where(lo, q2, zero), jnp.where(lo, zero, q2)], axis=0)
        keys = jnp.concatenate([r[hp] for r in kl] + [kx_ref[hp]], axis=0)
        return lax.dot_general(qs, keys, (((1,), (1,)), ((), ())), preferred_element_type=F32)

    def probs(hp, s):
        bias = jnp.concatenate(
            [jnp.concatenate([tab_ref[2 * hp + hh, tidx[qr][j]] for j in range(NA_KBLK)], axis=1)
             for hh in range(2) for qr in range(2)], axis=0)
        s_loc = jnp.where(row_ok2, s[:, :nloc] + bias, NEG)
        s_ctx = s[:, nloc:]
        m = jnp.maximum(jnp.max(s_loc, axis=-1, keepdims=True), jnp.max(s_ctx, axis=-1, keepdims=True))
        e_loc = jnp.exp2(s_loc - m)
        e_ctx = jnp.exp2(s_ctx - m)
        denom = jnp.sum(e_loc, axis=-1, keepdims=True) + jnp.sum(e_ctx, axis=-1, keepdims=True)
        return jnp.concatenate([e_loc, e_ctx], axis=1).astype(BF16), denom

    def output(hp, p, denom):
        vals = jnp.concatenate([r[hp] for r in vl] + [vx_ref[hp]], axis=0)
        o = jnp.dot(p, vals, preferred_element_type=F32) / denom
        o_ref[hp] = jnp.where(lo, o[:QB], o[QB:]).astype(BF16)

    _pipeline_pairs(scores, probs, output)


def _attn_b_call(qkv3, table):
    n = qkv3.shape[1]
    nb = n // QB
    first_lat = CTX // QB
    nbl = nb - first_lat

    def win(blk, i):
        return pl.BlockSpec(
            (HP, QB, LANES),
            lambda b: (blk, first_lat + jnp.clip(jnp.maximum(b - first_lat, 0) - 2, 0, nbl - NA_KBLK) + i, 0))

    def kv_specs(blk):
        return [win(blk, i) for i in range(NA_KBLK)] + [pl.BlockSpec((HP, CTX, LANES), lambda b: (blk, 0, 0))]

    return pl.pallas_call(
        functools.partial(_attn_b_body, nbl=nbl),
        grid=(nb,),
        in_specs=[_resident((HEADS, NA_TAB, GRID_W, LANES), lambda b: (0, 0, 0, 0)),
                  pl.BlockSpec((HP, QB, LANES), lambda b: (0, b, 0))] + kv_specs(1) + kv_specs(2),
        out_specs=pl.BlockSpec((HP, QB, LANES), lambda b: (0, b, 0)),
        out_shape=jax.ShapeDtypeStruct((HP, n, LANES), BF16),
        compiler_params=_cparams(52),
        name="attn_neighbourhood",
    )(table, qkv3, *([qkv3] * 12))


def _oproj_router_body(*refs, mode, n_stream):
    n_mix = 1 if mode == "attn" else 2
    w_ref = refs[n_mix]
    h_refs = refs[n_mix + 1:n_mix + 1 + n_stream]
    mod_ref, g2_ref, wrh_ref, wrl_ref, br_ref, out_ref, f_ref, r_ref, cnt_ref, base_ref = refs[n_mix + 1 + n_stream:]
    if mode == "attn":
        z = jnp.concatenate([refs[0][c] for c in range(HP)], axis=1)
    else:
        z = (jax.nn.gelu(refs[0][...]) * refs[1][...]).astype(BF16)
    y = jnp.dot(z, w_ref[...], preferred_element_type=F32)
    h_new = _stream_tile(h_refs) + mod_ref[0, 2:3, :] * y
    out_ref[...] = h_new
    _route_tile(h_new, g2_ref[...], mod_ref[0, 3:4, :], mod_ref[0, 4:5, :], wrh_ref, wrl_ref, br_ref,
                f_ref, r_ref, cnt_ref, base_ref)


def _oproj_router_call(mix_in, w, h, mod, g2, wr, br, *, mode, name):
    n = _stream_rows(h)
    kdim = w.shape[0]
    if mode == "attn":
        in_specs = [pl.BlockSpec((HP, TM, LANES), lambda i: (0, i, 0))]
        args = [mix_in]
    else:
        u, hs = mix_in
        in_specs = [pl.BlockSpec((TM, D_RNN), lambda i: (i, 0)), pl.BlockSpec((TM, D_RNN), lambda i: (i, 0))]
        args = [u, hs]
    wr_hi = wr.astype(BF16)
    wr_lo = (wr - wr_hi.astype(F32)).astype(BF16)
    h_specs, h_args = _stream_specs(h)
    in_specs += [_resident((kdim, D), lambda i: (0, 0))] + h_specs + [
                 pl.BlockSpec((1, 6, D), lambda i: (jnp.minimum(i, 1), 0, 0)),
                 pl.BlockSpec((1, D), lambda i: (0, 0)),
                 pl.BlockSpec((D, LANES), lambda i: (0, 0)),
                 pl.BlockSpec((D, LANES), lambda i: (0, 0)),
                 pl.BlockSpec((1, LANES), lambda i: (0, 0))]
    return pl.pallas_call(
        functools.partial(_oproj_router_body, mode=mode, n_stream=len(h_args)),
        grid=(n // TM,),
        in_specs=in_specs,
        out_specs=[pl.BlockSpec((TM, D), lambda i: (i, 0)),
                   pl.BlockSpec((TM * XROW_TILE, LANES), lambda i: (i, 0)),
                   pl.BlockSpec((TM, LANES), lambda i: (i, 0)),
                   pl.BlockSpec((1, LANES), lambda i: (0, 0))],
        out_shape=[jax.ShapeDtypeStruct((n, D), F32),
                   jax.ShapeDtypeStruct((n * XROW_TILE, LANES), jnp.uint32),
                   jax.ShapeDtypeStruct((n, LANES), F32),
                   jax.ShapeDtypeStruct((1, LANES), F32)],
        scratch_shapes=[pltpu.VMEM((1, LANES), F32)],
        compiler_params=_cparams(52),
        name=name,
    )(*args, w, *h_args, mod, g2.reshape(1, D), wr_hi, wr_lo, br)


def _shift_rows(x, s, fill, reverse):
    t = x.shape[0]
    if s % SUBLANES == 0:
        pad = jnp.full((s, x.shape[1]), fill, x.dtype)
        return jnp.concatenate([x[s:], pad] if reverse else [pad, x[:t - s]], axis=0)
    row = lax.broadcasted_iota(jnp.int32, x.shape, 0)
    if reverse:
        return jnp.where(row >= t - s, fill, pltpu.roll(x, t - s, 0))
    return jnp.where(row < s, fill, pltpu.roll(x, s, 0))


def _chunk_scan(a, b, reverse):
    s = 1
    while s < a.shape[0]:
        a_s = _shift_rows(a, s, 1.0, reverse)
        b_s = _shift_rows(b, s, 0.0, reverse)
        b = a * b_s + b
        a = a * a_s
        s *= 2
    return a, b


def _group_scan(a, b, h_in, reverse, scr_ref):
    t, c = a.shape
    g = t // SUBLANES
    a3 = a.reshape(g, SUBLANES, c)
    b3 = b.reshape(g, SUBLANES, c)
    sub = lax.broadcasted_iota(jnp.int32, a3.shape, 1)
    s = 1
    while s < SUBLANES:
        edge = (sub >= SUBLANES - s) if reverse else (sub < s)
        shift = SUBLANES - s if reverse else s
        a_s = jnp.where(edge, 1.0, pltpu.roll(a3, shift, 1))
        b_s = jnp.where(edge, 0.0, pltpu.roll(b3, shift, 1))
        b3 = a3 * b_s + b3
        a3 = a3 * a_s
        s *= 2
    a_loc = a3.reshape(t, c)
    b_loc = b3.reshape(t, c)
    edge_row = 0 if reverse else SUBLANES - 1

    def boundary_rows(i, x):
        for j in range(c // LANES):
            scr_ref[i, j] = x[:, j * LANES:(j + 1) * LANES]
        return jnp.concatenate([scr_ref[i, j, pl.ds(edge_row, g, stride=SUBLANES), :] for j in range(c // LANES)],
                               axis=1)

    ga, gb = _chunk_scan(boundary_rows(0, a_loc), boundary_rows(1, b_loc), reverse)
    leaving = gb + ga * h_in
    entering = _shift_rows(leaving, 1, 0.0, reverse)
    grow = lax.broadcasted_iota(jnp.int32, leaving.shape, 0)
    entering = jnp.where(grow == (g - 1 if reverse else 0), h_in, entering)
    carry = jnp.broadcast_to(entering[:, None, :], (g, SUBLANES, c))
    h = (a3 * carry + b3).reshape(t, c)
    return h, (leaving[0:1, :] if reverse else leaving[g - 1:g, :])


def _lru_body(x_ref, cw_ref, cb_ref, wa_ref, wx_ref, ba_ref, bx_ref, lam_ref, o_ref, scr_ref, xc_ref, *, nchunks):
    n = nchunks * SCAN_TC
    halo = SUBLANES
    cw = cw_ref[...]
    cb = cb_ref[...]
    sp = jax.nn.softplus(-lam_ref[...])

    def conv_chunk(ci):
        r0 = pl.multiple_of(ci * SCAN_TC, SCAN_TC)
        lat_first = CTX // SCAN_TC
        prev_ok = jnp.logical_and(ci != 0, ci != lat_first).astype(F32)
        next_ok = jnp.logical_and(ci != lat_first - 1, ci != nchunks - 1).astype(F32)
        p0 = pl.multiple_of(jnp.maximum(r0 - halo, 0), halo)
        n0 = pl.multiple_of(jnp.minimum(r0 + SCAN_TC, n - halo), halo)
        xe = jnp.concatenate([x_ref[pl.ds(p0, halo), :] * prev_ok,
                              x_ref[pl.ds(r0, SCAN_TC), :],
                              x_ref[pl.ds(n0, halo), :] * next_ok], axis=0)
        y = cb
        for j in range(CONV_W):
            off = halo + j - CONV_LEFT
            y = y + xe[off:off + SCAN_TC, :] * cw[j:j + 1, :]
        return r0, y

    def direction(d, ci, h_in):
        if d == 0:
            r0, xc = conv_chunk(ci)
            xc_ref[pl.ds(r0, SCAN_TC), :] = xc
        else:
            r0 = pl.multiple_of(ci * SCAN_TC, SCAN_TC)
            xc = xc_ref[pl.ds(r0, SCAN_TC), :]
        xb = xc.astype(BF16)
        r = jax.nn.sigmoid(jnp.dot(xb, wa_ref[d, 0], preferred_element_type=F32) + ba_ref[d:d + 1, :])
        ig = jax.nn.sigmoid(jnp.dot(xb, wx_ref[d, 0], preferred_element_type=F32) + bx_ref[d:d + 1, :])
        log_a = -LRU_C * r * sp[d:d + 1, :]
        a = jnp.exp(log_a)
        bb = jnp.sqrt(1.0 - a * a) * (ig * xc)
        h, h_out = _group_scan(a, bb, h_in, d == 1, scr_ref)
        return r0, h, h_out

    def fwd(ci, h_in):
        r0, h, h_out = direction(0, ci, h_in)
        o_ref[pl.ds(r0, SCAN_TC), :] = h
        return h_out

    def bwd(ci, h_in):
        r0, h, h_out = direction(1, ci, h_in)
        o_ref[pl.ds(r0, SCAN_TC), :] += h
        return h_out

    h0 = jnp.zeros((1, RNN_BW), F32)
    lax.fori_loop(0, nchunks, fwd, h0)
    nctx = CTX // SCAN_TC
    hc = lax.fori_loop(0, nctx, lambda k, hh: bwd(nctx - 1 - k, hh), h0)
    lax.fori_loop(0, nchunks - nctx, lambda k, hh: bwd(nchunks - 1 - k, hh), hc)


def _lru_call(u, conv_w, conv_b, wa, wx, ba, bx, lam):
    n = u.shape[0]
    return pl.pallas_call(
        functools.partial(_lru_body, nchunks=n // SCAN_TC),
        grid=(RNN_BLOCKS,),
        in_specs=[_resident((n, RNN_BW), lambda m: (0, RNN_BLOCKS + m)),
                  pl.BlockSpec((CONV_W, RNN_BW), lambda m: (0, m)),
                  pl.BlockSpec((1, RNN_BW), lambda m: (0, m)),
                  pl.BlockSpec((2, 1, RNN_BW, RNN_BW), lambda m: (0, m, 0, 0)),
                  pl.BlockSpec((2, 1, RNN_BW, RNN_BW), lambda m: (0, m, 0, 0)),
                  pl.BlockSpec((2, RNN_BW), lambda m: (0, m)),
                  pl.BlockSpec((2, RNN_BW), lambda m: (0, m)),
                  pl.BlockSpec((2, RNN_BW), lambda m: (0, m))],
        out_specs=_resident((n, RNN_BW), lambda m: (0, m)),
        scratch_shapes=[pltpu.VMEM((2, RNN_BW // LANES, SCAN_TC, LANES), F32), pltpu.VMEM((n, RNN_BW), F32)],
        out_shape=jax.ShapeDtypeStruct((n, D_RNN), F32),
        compiler_params=_cparams(58),
        name="rglru_scan",
    )(u, conv_w, conv_b.reshape(1, D_RNN), wa, wx, ba, bx, lam)


def _pack_words(ref, lo, hi, first_tile):
    rows = lo.shape[0]
    lo = lax.bitcast_convert_type(lo.astype(BF16).astype(F32), jnp.uint32)
    hi = lax.bitcast_convert_type(hi.astype(BF16).astype(F32), jnp.uint32)
    w = hi | (lo >> 16)
    for s in range(lo.shape[1] // LANES):
        ref[pl.ds(first_tile + s, rows, stride=XROW_TILE), :] = w[:, s * LANES:(s + 1) * LANES]


def _pack_rows(ref, x):
    _pack_words(ref, x[:, :D // 2], x[:, D // 2:], 0)


def _unpack_rows(ref, rows, dtype):
    w = jnp.concatenate([ref[pl.ds(s, rows, stride=XROW_TILE), :] for s in range(XROW_TILE)], axis=1)
    lo = lax.bitcast_convert_type(w << 16, F32)
    hi = lax.bitcast_convert_type(w & jnp.uint32(0xFFFF0000), F32)
    return jnp.concatenate([lo, hi], axis=1).astype(dtype)


def _route_tile(h, g, shift, scale, wrh_ref, wrl_ref, br_ref, f_ref, r_ref, cnt_ref, base_ref):
    @pl.when(pl.program_id(0) == 0)
    def _():
        base_ref[...] = jnp.zeros(base_ref.shape, F32)

    f = _norm_mod(h, g, shift, scale)
    _pack_rows(f_ref, f)
    f_hi = f.astype(BF16)
    f_lo = (f - f_hi.astype(F32)).astype(BF16)
    logits = (jnp.dot(jnp.concatenate([f_hi, f_lo], axis=0), wrh_ref[...], preferred_element_type=F32).reshape(2, TM, LANES).sum(axis=0)
              + jnp.dot(f_hi, wrl_ref[...], preferred_element_type=F32) + br_ref[...])
    lane = lax.broadcasted_iota(jnp.int32, logits.shape, 1)
    ninf = -jnp.inf
    is_g = lane < N_GROUPS
    gl = jnp.where(is_g, logits, ninf)
    gmax = jnp.max(gl, axis=-1, keepdims=True)
    gsel = jnp.min(jnp.where(gl == gmax, lane, LANES), axis=-1, keepdims=True)
    gsum = jnp.sum(jnp.where(is_g, jnp.exp(gl - gmax), 0.0), axis=-1, keepdims=True)
    g_w = 1.0 / gsum
    e_lo = N_GROUPS + EXP_PER_GROUP * gsel
    in_grp = jnp.logical_and(lane >= e_lo, lane < e_lo + EXP_PER_GROUP)
    el = jnp.where(in_grp, logits, ninf)
    m1 = jnp.max(el, axis=-1, keepdims=True)
    i1 = jnp.min(jnp.where(el == m1, lane, LANES), axis=-1, keepdims=True)
    el2 = jnp.where(lane == i1, ninf, el)
    m2 = jnp.max(el2, axis=-1, keepdims=True)
    i2 = jnp.min(jnp.where(el2 == m2, lane, LANES), axis=-1, keepdims=True)
    t = jnp.exp(m2 - m1)
    w1 = g_w / (1.0 + t)
    w2 = g_w * t / (1.0 + t)
    oh1 = lane == i1
    oh2 = lane == i2
    rr = lax.broadcasted_iota(jnp.int32, (TM, TM), 0)
    cc = lax.broadcasted_iota(jnp.int32, (TM, TM), 1)
    tri = (cc < rr).astype(BF16)
    pre1 = jnp.dot(tri, oh1.astype(BF16), preferred_element_type=F32)
    pre2 = jnp.dot(tri, oh2.astype(BF16), preferred_element_type=F32)
    base = base_ref[...]
    cnt1 = jnp.sum(oh1.astype(F32), axis=0, keepdims=True)
    cnt2 = jnp.sum(oh2.astype(F32), axis=0, keepdims=True)
    rank1 = jnp.sum(jnp.where(oh1, pre1 + base, 0.0), axis=-1, keepdims=True)
    rank2 = jnp.sum(jnp.where(oh2, pre2 + (base + cnt1), 0.0), axis=-1, keepdims=True)
    total = base + cnt1 + cnt2
    base_ref[...] = total
    cnt_ref[...] = total
    cols = [(i1 - N_GROUPS).astype(F32), (i2 - N_GROUPS).astype(F32), w1, w2, rank1, rank2]
    out = jnp.zeros(logits.shape, F32)
    for k, v in enumerate(cols):
        out = jnp.where(lane == k, v, out)
    r_ref[...] = out


def _dispatch_body(dest_ref, pad_lo_ref, pad_n_ref, f_ref, xg_ref, z_ref, sem, run_sem, blk_sem):
    i = pl.program_id(0)
    base = i * TM

    def issue(j, carry):
        for k in range(TOP_K):
            d = dest_ref[(base + j) * TOP_K + k]
            pltpu.make_async_copy(f_ref.at[pl.ds(j * XROW_TILE, XROW_TILE)],
                                  xg_ref.at[pl.ds(d * XROW_TILE, XROW_TILE)], sem).start(priority=k)
        return carry

    z_row = z_ref.at[pl.ds(0, XROW_TILE)]

    def wait_one(carry):
        pltpu.make_async_copy(z_row, xg_ref.at[pl.ds(0, XROW_TILE)], sem).wait()
        return carry

    lax.fori_loop(0, TM, issue, 0, unroll=DMA_UNROLL)

    @pl.when(i == 0)
    def _():
        z_ref[...] = jnp.zeros(z_ref.shape, jnp.uint32)

        z_run = z_ref.at[pl.ds(0, FILL_RUN * XROW_TILE)]

        def fill(e, carry):
            lo = pad_lo_ref[e]
            n_fill = pad_n_ref[e]
            n_runs = n_fill // FILL_RUN
            lo_rows = lo + n_runs * FILL_RUN
            n_rows = n_fill - n_runs * FILL_RUN

            def run_copy(bi):
                return pltpu.make_async_copy(
                    z_run, xg_ref.at[pl.ds((lo + bi * FILL_RUN) * XROW_TILE, FILL_RUN * XROW_TILE)], run_sem)

            def one(s, c):
                pltpu.make_async_copy(z_row, xg_ref.at[pl.ds((lo_rows + s) * XROW_TILE, XROW_TILE)], sem).start()
                return c

            lax.fori_loop(0, n_runs, lambda bi, c: (run_copy(bi).start(), c)[1], 0)
            lax.fori_loop(0, n_rows, one, 0)
            lax.fori_loop(0, n_runs, lambda bi, c: (run_copy(bi).wait(), c)[1], 0)
            lax.fori_loop(0, n_rows, lambda s, c: wait_one(c), 0)
            return carry

        lax.fori_loop(0, N_EXPERTS, fill, 0)

        def block_copy(bi):
            row0 = (pad_lo_ref[N_EXPERTS] + bi * MOE_MB) * XROW_TILE
            return pltpu.make_async_copy(z_ref, xg_ref.at[pl.ds(row0, MOE_MB * XROW_TILE)], blk_sem)

        n_tail = pad_n_ref[N_EXPERTS] // MOE_MB
        lax.fori_loop(0, n_tail, lambda bi, c: (block_copy(bi).start(), c)[1], 0)
        lax.fori_loop(0, n_tail, lambda bi, c: (block_copy(bi).wait(), c)[1], 0)

    lax.fori_loop(0, TM * TOP_K, lambda j, c: wait_one(c), 0, unroll=DMA_UNROLL)


def _dispatch_call(dest, pad_lo, pad_n, f2, p):
    n = f2.shape[0] // XROW_TILE
    return pl.pallas_call(
        _dispatch_body,
        grid_spec=pltpu.PrefetchScalarGridSpec(
            num_scalar_prefetch=3,
            grid=(n // TM,),
            in_specs=[pl.BlockSpec((TM * XROW_TILE, LANES), lambda i, d, lo, nn: (i, 0))],
            out_specs=pl.BlockSpec(memory_space=pl.ANY),
            scratch_shapes=[pltpu.VMEM((MOE_MB * XROW_TILE, LANES), jnp.uint32), pltpu.SemaphoreType.DMA(()),
                            pltpu.SemaphoreType.DMA(()), pltpu.SemaphoreType.DMA(())]),
        out_shape=jax.ShapeDtypeStruct((p * XROW_TILE, LANES), jnp.uint32),
        compiler_params=_cparams(16),
        name="moe_dispatch",
    )(dest, pad_lo, pad_n, f2)


def _expert_weights(b, be_ref, ne_ref, nu_ref, w_hbm, wf_ref, wb_ref, sem, slot_ref, layer):
    e = be_ref[b]
    first = jnp.logical_or(b == 0, e != be_ref[jnp.maximum(b - 1, 0)])

    def fetch(expert, slot):
        return pltpu.make_async_copy(w_hbm.at[layer, expert], wf_ref.at[slot], sem.at[slot])

    @pl.when(b == 0)
    def _():
        slot_ref[0] = 0
        fetch(e, 0).start()

    @pl.when(jnp.logical_and(first, b < nu_ref[0]))
    def _():
        s = slot_ref[0]
        fetch(e, s).wait()
        wb_ref[...] = wf_ref[s].astype(BF16)
        ne = ne_ref[b]

        @pl.when(ne != e)
        def _():
            fetch(ne, 1 - s).start()

        slot_ref[0] = 1 - s


def _ffn_up_body(be_ref, ne_ref, nu_ref, x_ref, w_hbm, a_ref, wf_ref, wb_ref, sem, slot_ref, *, layer):
    b = pl.program_id(0)
    _expert_weights(b, be_ref, ne_ref, nu_ref, w_hbm, wf_ref, wb_ref, sem, slot_ref, layer)

    @pl.when(b < nu_ref[0])
    def _():
        x = _unpack_rows(x_ref, MOE_MB, BF16)
        cw = FFN_CHUNK
        n_chunks = D_EXPERT // cw

        def gate_up(c):
            return (jnp.dot(x, wb_ref[:, c * cw:(c + 1) * cw], preferred_element_type=F32),
                    jnp.dot(x, wb_ref[:, D_EXPERT + c * cw:D_EXPERT + (c + 1) * cw], preferred_element_type=F32))

        nxt = gate_up(0)
        for c in range(n_chunks):
            g, u = nxt
            if c + 1 < n_chunks:
                nxt = gate_up(c + 1)
            a_ref[:, c * cw:(c + 1) * cw] = (g * jax.nn.sigmoid(g) * u).astype(BF16)

    @pl.when(b >= nu_ref[0])
    def _():
        a_ref[...] = jnp.zeros(a_ref.shape, BF16)


def _expert_scratch(rows, cols):
    return [pltpu.VMEM((2, rows, cols), F32), pltpu.VMEM((rows, cols), BF16),
            pltpu.SemaphoreType.DMA((2,)), pltpu.SMEM((1,), jnp.int32)]


def _ffn_up_call(block_e, next_e, n_used, xg, w_gu, layer):
    p = xg.shape[0] // XROW_TILE
    return pl.pallas_call(
        functools.partial(_ffn_up_body, layer=layer),
        grid_spec=pltpu.PrefetchScalarGridSpec(
            num_scalar_prefetch=3,
            grid=(p // MOE_MB,),
            in_specs=[pl.BlockSpec((MOE_MB * XROW_TILE, LANES),
                                   lambda b, be, ne, nu: (jnp.minimum(b, nu[0] - 1), 0)),
                      pl.BlockSpec(memory_space=pl.ANY)],
            out_specs=pl.BlockSpec((MOE_MB, D_EXPERT), lambda b, be, ne, nu: (b, 0)),
            scratch_shapes=_expert_scratch(D, 2 * D_EXPERT)),
        out_shape=jax.ShapeDtypeStruct((p, D_EXPERT), BF16),
        compiler_params=_cparams(48),
        name="moe_ffn_up",
    )(block_e, next_e, n_used, xg, w_gu)


def _ffn_down_body(be_ref, ne_ref, nu_ref, a_ref, w_hbm, y_ref, wf_ref, wb_ref, sem, slot_ref, *, layer):
    b = pl.program_id(0)
    _expert_weights(b, be_ref, ne_ref, nu_ref, w_hbm, wf_ref, wb_ref, sem, slot_ref, layer)

    @pl.when(b < nu_ref[0])
    def _():
        a = a_ref[...]
        cw = FFN_CHUNK
        n_chunks = D // 2 // cw

        def halves(c):
            return (jnp.dot(a, wb_ref[:, c * cw:(c + 1) * cw], preferred_element_type=F32),
                    jnp.dot(a, wb_ref[:, D // 2 + c * cw:D // 2 + (c + 1) * cw], preferred_element_type=F32))

        nxt = halves(0)
        for c in range(n_chunks):
            lo, hi = nxt
            if c + 1 < n_chunks:
                nxt = halves(c + 1)
            _pack_words(y_ref, lo, hi, c * (cw // LANES))

    @pl.when(b >= nu_ref[0])
    def _():
        y_ref[...] = jnp.zeros(y_ref.shape, jnp.uint32)


def _ffn_down_call(block_e, next_e, n_used, act, w_down, layer):
    p = act.shape[0]
    return pl.pallas_call(
        functools.partial(_ffn_down_body, layer=layer),
        grid_spec=pltpu.PrefetchScalarGridSpec(
            num_scalar_prefetch=3,
            grid=(p // MOE_MB,),
            in_specs=[pl.BlockSpec((MOE_MB, D_EXPERT), lambda b, be, ne, nu: (b, 0)),
                      pl.BlockSpec(memory_space=pl.ANY)],
            out_specs=pl.BlockSpec((MOE_MB * XROW_TILE, LANES), lambda b, be, ne, nu: (b, 0)),
            scratch_shapes=_expert_scratch(D_EXPERT, D)),
        out_shape=jax.ShapeDtypeStruct((p * XROW_TILE, LANES), jnp.uint32),
        compiler_params=_cparams(32),
        name="moe_ffn_down",
    )(block_e, next_e, n_used, act, w_down)


def _combine_body(dest_ref, h_ref, r_ref, mod_ref, g_ref, yb_ref, o_ref, buf_ref, sem, *, final):
    i = pl.program_id(0)
    slot = i % 2

    def gather(tile, dst_slot):
        base = tile * TM

        def issue(j, carry):
            for k in range(TOP_K):
                d = dest_ref[(base + j) * TOP_K + k]
                pltpu.make_async_copy(yb_ref.at[pl.ds(d * XROW_TILE, XROW_TILE)],
                                      buf_ref.at[dst_slot, k, pl.ds(j * XROW_TILE, XROW_TILE)],
                                      sem.at[dst_slot]).start(priority=k)
            return carry

        lax.fori_loop(0, TM, issue, 0, unroll=DMA_UNROLL)

    @pl.when(i == 0)
    def _():
        gather(0, 0)

    @pl.when(i + 1 < pl.num_programs(0))
    def _():
        gather(i + 1, 1 - slot)

    def drain(j, carry):
        pltpu.make_async_copy(yb_ref.at[pl.ds(0, XROW_TILE)], buf_ref.at[slot, 0, pl.ds(0, XROW_TILE)],
                              sem.at[slot]).wait()
        return carry

    lax.fori_loop(0, TM * TOP_K, drain, 0, unroll=DMA_UNROLL)
    r = r_ref[...]
    y = sum(r[:, TOP_K + k:TOP_K + k + 1] * _unpack_rows(buf_ref.at[slot, k], TM, F32) for k in range(TOP_K))
    h_new = h_ref[...] + mod_ref[0, 2:3, :] * y
    if final:
        ms = jnp.mean(h_new * h_new, axis=-1, keepdims=True)
        h_new = h_new * lax.rsqrt(ms + EPS) * g_ref[...]
    o_ref[...] = h_new


def _combine_call(dest, h, route, mod, yb, final_g=None):
    n = h.shape[0]
    final = final_g is not None
    first = CTX // TM
    out_rows = n - CTX if final else n
    out_map = (lambda i, d: (jnp.maximum(i - first, 0), 0)) if final else (lambda i, d: (i, 0))
    g = final_g if final else jnp.ones((D,), F32)
    return pl.pallas_call(
        functools.partial(_combine_body, final=final),
        grid_spec=pltpu.PrefetchScalarGridSpec(
            num_scalar_prefetch=1,
            grid=(n // TM,),
            in_specs=[pl.BlockSpec((TM, D), lambda i, d: (i, 0)),
                      pl.BlockSpec((TM, LANES), lambda i, d: (i, 0)),
                      pl.BlockSpec((1, 3, D), lambda i, d: (jnp.minimum(i, 1), 0, 0)),
                      pl.BlockSpec((1, D), lambda i, d: (0, 0)),
                      pl.BlockSpec(memory_space=pl.ANY)],
            out_specs=pl.BlockSpec((TM, D), out_map),
            scratch_shapes=[pltpu.VMEM((2, TOP_K, TM * XROW_TILE, LANES), jnp.uint32),
                            pltpu.SemaphoreType.DMA((2,))]),
        out_shape=jax.ShapeDtypeStruct((out_rows, D), F32),
        compiler_params=_cparams(32),
        name="moe_combine",
    )(dest, h, route, mod[:, 3:6], g.reshape(1, D), yb)


def _dispatch_plan(route, cnt, n):
    experts = jnp.arange(N_EXPERTS, dtype=jnp.int32)
    counts = cnt[0, N_GROUPS:N_GROUPS + N_EXPERTS].astype(jnp.int32)
    padded = (counts + MOE_MB - 1) // MOE_MB * MOE_MB
    end_pad = jnp.cumsum(padded)
    start_pad = end_pad - padded
    n_blocks = -(-(n * TOP_K + N_EXPERTS * (MOE_MB - 1)) // MOE_MB)
    p = n_blocks * MOE_MB
    blk_start = jnp.arange(n_blocks, dtype=jnp.int32) * MOE_MB
    block_e = jnp.minimum(jnp.sum(end_pad[None, :] <= blk_start[:, None], axis=1), N_EXPERTS - 1).astype(jnp.int32)
    n_used = (end_pad[-1:] // MOE_MB).astype(jnp.int32)
    group_end = end_pad[block_e] // MOE_MB
    next_e = jnp.where(group_end < n_used[0], block_e[jnp.minimum(group_end, n_blocks - 1)], block_e)
    e_idx = route[:, 0:TOP_K].astype(jnp.int32)
    rank = route[:, 2 * TOP_K:3 * TOP_K].astype(jnp.int32)
    start = jnp.sum(jnp.where(e_idx[..., None] == experts, start_pad, 0), axis=-1)
    dest = (start + rank).reshape(n * TOP_K)
    pad_lo = jnp.concatenate([start_pad + counts, end_pad[-1:]]).astype(jnp.int32)
    pad_n = jnp.concatenate([padded - counts, p - end_pad[-1:]]).astype(jnp.int32)
    return p, block_e, next_e, n_used, dest, pad_lo, pad_n


def _moe_layer(h, f2, route, cnt, mod, w_gu, w_down, layer, final_g=None):
    n = h.shape[0]
    p, block_e, next_e, n_used, dest, pad_lo, pad_n = _dispatch_plan(route, cnt, n)
    xg = _dispatch_call(dest, pad_lo, pad_n, f2, p)
    act = _ffn_up_call(block_e, next_e, n_used, xg, w_gu, layer)
    yb = _ffn_down_call(block_e, next_e, n_used, act, w_down, layer)
    return _combine_call(dest, h, route, mod, yb, final_g)


def _rope_tables(l):
    quarter = HEAD_DIM // 4
    inv = ROPE_BASE ** (-jnp.arange(quarter, dtype=F32) / quarter)
    n_rows = l // GRID_W
    ang_r = jnp.arange(n_rows, dtype=F32)[:, None] * inv
    ang_c = jnp.arange(GRID_W, dtype=F32)[:, None] * inv
    by_row = lambda t: jnp.broadcast_to(t[:, None, :], (n_rows, GRID_W, quarter)).reshape(l, quarter)
    by_col = lambda t: jnp.broadcast_to(t[None, :, :], (n_rows, GRID_W, quarter)).reshape(l, quarter)
    cr, sr, cc, sc = by_row(jnp.cos(ang_r)), by_row(jnp.sin(ang_r)), by_col(jnp.cos(ang_c)), by_col(jnp.sin(ang_c))
    cos = jnp.concatenate([cr, cr, cc, cc], axis=1)
    sin = jnp.concatenate([-sr, sr, -sc, sc], axis=1)
    cos = jnp.concatenate([jnp.ones((CTX, HEAD_DIM), F32), cos], axis=0)
    sin = jnp.concatenate([jnp.zeros((CTX, HEAD_DIM), F32), sin], axis=0)
    return jnp.tile(cos, (1, 2)), jnp.tile(sin, (1, 2))


def _attn_a_weights(w_qkv):
    nq = HEADS * HEAD_DIM
    nkv = A_KV_HEADS * HEAD_DIM
    wq = w_qkv[:, :nq]
    wk = w_qkv[:, nq:nq + nkv].reshape(D, A_KV_HEADS, 1, HEAD_DIM)
    wv = w_qkv[:, nq + nkv:].reshape(D, A_KV_HEADS, 1, HEAD_DIM)
    dup = lambda w: jnp.broadcast_to(w, (D, A_KV_HEADS, 2, HEAD_DIM)).reshape(D, 2 * nkv)
    return jnp.concatenate([wq, dup(wk), dup(wv)], axis=1).astype(BF16)


def kernel(x, c, ctx, c_ctx, ada_w, ada_b, norm_mix_g, norm_ffn_g, router_group_w, router_group_b,
           router_expert_w, router_expert_b, moe_w_gu, moe_w_down, attn_w_qkv, attn_w_o, attn_sink,
           na_w_qkv, na_w_o, na_rpb, rnn_w_in, rnn_conv_w, rnn_conv_b, rnn_wa, rnn_ba, rnn_wx, rnn_bx,
           rnn_lam, rnn_w_out, final_norm_g):
    batch, l, _ = x.shape
    assert batch == 1 and ctx.shape[1] == CTX and l % (QB * 2) == 0 and l // QB >= NA_KBLK
    h = (ctx[0], x[0])
    c2 = jnp.stack([c_ctx, c[0]], axis=1)
    mods = _ada_call(c2, ada_w, ada_b).reshape(DEPTH, 2, 6, D)
    rope = _rope_tables(l)
    pad_r = LANES - N_GROUPS - N_EXPERTS
    for i in range(DEPTH):
        kind, j = i % 3, i // 3
        mod = mods[i]
        wr = jnp.concatenate([router_group_w[i], router_expert_w[i], jnp.zeros((D, pad_r), F32)], axis=1)
        br = jnp.concatenate([router_group_b[i], router_expert_b[i], jnp.zeros((pad_r,), F32)]).reshape(1, LANES)
        if kind == 0:
            w = _attn_a_weights(attn_w_qkv[j])
            nq = HEADS * HEAD_DIM
            qkv3 = _proj_call(h, norm_mix_g[i], mod, w, mode="rope", rope=rope,
                              n_rope=nq + 2 * A_KV_HEADS * HEAD_DIM, n_q=nq, name="proj_window")
            mix, w_out, mode, name = _attn_a_call(qkv3, attn_sink[j]), attn_w_o[j], "attn", "oproj_window"
        elif kind == 1:
            qkv3 = _proj_call(h, norm_mix_g[i], mod, na_w_qkv[j].astype(BF16), mode="cols",
                              n_q=HEADS * HEAD_DIM, name="proj_neighbourhood")
            mix, w_out, mode, name = (_attn_b_call(qkv3, _na_table_call(na_rpb[j])), na_w_o[j], "attn",
                                      "oproj_neighbourhood")
        else:
            u = _proj_call(h, norm_mix_g[i], mod, rnn_w_in[j].astype(BF16), mode="plain", name="proj_rglru")
            hs = _lru_call(u, rnn_conv_w[j], rnn_conv_b[j], rnn_wa[j].astype(BF16), rnn_wx[j].astype(BF16),
                           rnn_ba[j], rnn_bx[j], rnn_lam[j])
            mix, w_out, mode, name = (u, hs), rnn_w_out[j], "rnn", "oproj_rglru"
        h, f2, route, cnt = _oproj_router_call(mix, w_out.astype(BF16), h, mod, norm_ffn_g[i], wr, br,
                                               mode=mode, name=name)
        h = _moe_layer(h, f2, route, cnt, mod, moe_w_gu, moe_w_down, i,
                       final_norm_g if i == DEPTH - 1 else None)
    return h[None]
```

```python
import functools

import jax
import jax.numpy as jnp
from jax import lax
from jax.experimental import pallas as pl
from jax.experimental.pallas import tpu as pltpu

F32 = jnp.float32
BF16 = jnp.bfloat16

D = 2048
DEPTH = 4
GRID_W = 64
CTX = 256
HEADS = 32
HEAD_DIM = 64
A_KV_HEADS = 4
A_GROUP = HEADS // A_KV_HEADS
WINDOW = 128
NB_KH = 8
NB_KW = 16
D_RNN = 2560
RNN_BLOCKS = 10
RNN_BW = D_RNN // RNN_BLOCKS
CONV_W = 4
CONV_LEFT = 2
LRU_C = 8.0
N_GROUPS = 4
EXP_PER_GROUP = 8
N_EXPERTS = N_GROUPS * EXP_PER_GROUP
TOP_K = 2
D_EXPERT = 768
ROPE_BASE = 10000.0
EPS = 1e-6
NEG = -1e30

LANES = 128
SUBLANES = 8
MIB = 1024 * 1024

TM = 256
QB = 128
HP = HEADS // 2
MOE_MB = 512
ROW_TILE = D // LANES
XROW_TILE = ROW_TILE // 2
DMA_UNROLL = 8
FFN_CHUNK = 256
FILL_RUN = 64
SCAN_TC = 256
NA_KBLK = 5
SQRT_SCALE = HEAD_DIM ** -0.5
LOG2E = 1.4426950408889634


def _cparams(vmem_mib, sem=("arbitrary",)):
    return pltpu.CompilerParams(dimension_semantics=sem, vmem_limit_bytes=int(vmem_mib * MIB))


def _resident(block_shape, index_map):
    return pl.BlockSpec(block_shape, index_map, pipeline_mode=pl.Buffered(1))


def _mod_spec():
    return pl.BlockSpec((1, 3, D), lambda i: (jnp.minimum(i, 1), 0, 0))


def _stream_specs(h):
    if isinstance(h, tuple):
        first = CTX // TM
        return [pl.BlockSpec((TM, D), lambda i: (0, 0)),
                pl.BlockSpec((TM, D), lambda i: (jnp.maximum(i - first, 0), 0))], list(h)
    return [pl.BlockSpec((TM, D), lambda i: (i, 0))], [h]


def _stream_tile(refs):
    if len(refs) == 2:
        return jnp.where(pl.program_id(0) < CTX // TM, refs[0][...], refs[1][...])
    return refs[0][...]


def _stream_rows(h):
    return sum(a.shape[0] for a in h) if isinstance(h, tuple) else h.shape[0]


def _norm_mod(x, g, shift, scale):
    ms = jnp.mean(x * x, axis=-1, keepdims=True)
    y = x * lax.rsqrt(ms + EPS) * g
    return y * (1.0 + scale) + shift


ADA_TN = 1024


def _ada_body(c_ref, w_ref, b_ref, o_ref):
    c = c_ref[...]
    cs = c * jax.nn.sigmoid(c)
    for r in range(2):
        cb = jnp.broadcast_to(cs[:, r:r + 1], (D, LANES))
        outs = []
        for j in range(ADA_TN // LANES):
            w = w_ref[0, :, j * LANES:(j + 1) * LANES]
            p = (w * cb).reshape(D // SUBLANES, SUBLANES, LANES).sum(axis=0)
            outs.append(p.sum(axis=0, keepdims=True))
        o_ref[0, r:r + 1, :] = jnp.concatenate(outs, axis=1) + b_ref[0]


def _ada_call(c2, ada_w, ada_b):
    return pl.pallas_call(
        _ada_body,
        grid=(DEPTH, 6 * D // ADA_TN),
        in_specs=[pl.BlockSpec((D, 2), lambda l, j: (0, 0)),
                  pl.BlockSpec((1, D, ADA_TN), lambda l, j: (l, 0, j)),
                  pl.BlockSpec((1, 1, ADA_TN), lambda l, j: (l, 0, j))],
        out_specs=pl.BlockSpec((1, 2, ADA_TN), lambda l, j: (l, 0, j)),
        out_shape=jax.ShapeDtypeStruct((DEPTH, 2, 6 * D), F32),
        compiler_params=_cparams(32, ("arbitrary", "arbitrary")),
        name="ada_mod",
    )(c2, ada_w, ada_b.reshape(DEPTH, 1, 6 * D))


PROJ_CH = 512


def _rope_piece(piece, cos, sin):
    lane = lax.broadcasted_iota(jnp.int32, piece.shape, 1)
    first = (lane & 16) == 0
    partner = jnp.where(first, pltpu.roll(piece, LANES - 16, 1), pltpu.roll(piece, 16, 1))
    return piece * cos + partner * sin


def _proj_body(*refs, mode, n_rope, n_q, n_stream):
    x_refs, refs = refs[:n_stream], refs[n_stream:]
    if mode == "rope":
        g_ref, mod_ref, w_ref, cos_ref, sin_ref, o_ref = refs
    else:
        g_ref, mod_ref, w_ref, o_ref = refs
    a = _norm_mod(_stream_tile(x_refs), g_ref[...], mod_ref[0, 0:1, :], mod_ref[0, 1:2, :]).astype(BF16)
    nout = w_ref.shape[1]
    for c in range(nout // PROJ_CH):
        acc = jnp.dot(a, w_ref[:, c * PROJ_CH:(c + 1) * PROJ_CH], preferred_element_type=F32)
        if mode == "plain":
            o_ref[:, c * PROJ_CH:(c + 1) * PROJ_CH] = acc
            continue
        for k in range(PROJ_CH // LANES):
            col0 = c * PROJ_CH + k * LANES
            piece = acc[:, k * LANES:(k + 1) * LANES]
            if col0 < n_rope:
                piece = _rope_piece(piece, cos_ref[...], sin_ref[...])
            if col0 < n_q:
                piece = piece * (SQRT_SCALE * LOG2E)
            o_ref[col0 // LANES] = piece.astype(BF16)


def _proj_call(h, g, mod, w, *, mode, rope=None, n_rope=0, n_q=0, name):
    n = _stream_rows(h)
    nout = w.shape[1]
    in_specs, args = _stream_specs(h)
    n_stream = len(args)
    in_specs += [pl.BlockSpec((1, D), lambda i: (0, 0)),
                 _mod_spec(),
                 _resident((D, nout), lambda i: (0, 0))]
    args += [g.reshape(1, D), mod[:, 0:3], w]
    if mode == "rope":
        in_specs += [pl.BlockSpec((TM, LANES), lambda i: (i, 0))] * 2
        args += list(rope)
    if mode == "plain":
        out_spec = pl.BlockSpec((TM, nout), lambda i: (i, 0))
        out_shape = jax.ShapeDtypeStruct((n, nout), F32)
        out_bytes = TM * nout * 4
    else:
        out_spec = pl.BlockSpec((nout // LANES, TM, LANES), lambda i: (0, i, 0))
        out_shape = jax.ShapeDtypeStruct((nout // LANES, n, LANES), BF16)
        out_bytes = TM * nout * 2
    vmem = (D * nout * 2 + 2 * TM * D * 4 + 2 * out_bytes) / MIB + 12
    return pl.pallas_call(
        functools.partial(_proj_body, mode=mode, n_rope=n_rope, n_q=n_q, n_stream=n_stream),
        grid=(n // TM,),
        in_specs=in_specs,
        out_specs=out_spec,
        out_shape=out_shape,
        compiler_params=_cparams(vmem),
        name=name,
    )(*args)


def _lane_lo(shape):
    return lax.broadcasted_iota(jnp.int32, shape, len(shape) - 1) < HEAD_DIM


def _pipeline_pairs(scores, probs, output):
    s_next = scores(0)
    pending = None
    for hp in range(HP):
        s_cur = s_next
        if hp + 1 < HP:
            s_next = scores(hp + 1)
        cur = probs(hp, s_cur)
        if pending is not None:
            output(hp - 1, *pending)
        pending = cur
    output(HP - 1, *pending)


def _attn_a_body(sink_ref, q_ref, kp_ref, kc_ref, kn_ref, kx_ref, vp_ref, vc_ref, vn_ref, vx_ref, o_ref, *, nb):
    b = pl.program_id(0)
    first_lat = CTX // QB
    is_lat = b >= first_lat
    prev_ok = b >= first_lat + 1
    next_ok = jnp.logical_and(is_lat, b <= nb - 2)
    nloc = 3 * QB
    nkeys = nloc + CTX
    qi = lax.broadcasted_iota(jnp.int32, (QB, nkeys), 0)
    kj = lax.broadcasted_iota(jnp.int32, (QB, nkeys), 1)
    rel = qi + WINDOW - kj
    band = jnp.abs(rel) <= WINDOW
    seg_ok = jnp.where(kj < QB, prev_ok.astype(jnp.int32),
                       jnp.where(kj < 2 * QB, is_lat.astype(jnp.int32), next_ok.astype(jnp.int32)))
    ok = jnp.logical_or(kj >= nloc, jnp.logical_and(band, seg_ok > 0))
    mask_add = jnp.where(ok, 0.0, NEG)
    lo = _lane_lo((QB, LANES))
    zero = jnp.zeros((QB, LANES), BF16)
    def scores(hp):
        kvh = hp // (A_GROUP // 2)
        keys = jnp.concatenate([kp_ref[kvh], kc_ref[kvh], kn_ref[kvh], kx_ref[kvh]], axis=0)
        q2 = q_ref[hp]
        qs = jnp.concatenate([jnp.where(lo, q2, zero), jnp.where(lo, zero, q2)], axis=0)
        return lax.dot_general(qs, keys, (((1,), (1,)), ((), ())), preferred_element_type=F32)

    def probs(hp, s):
        sink = jnp.concatenate([jnp.full((1, QB, 1), sink_ref[2 * hp + hh] * LOG2E, F32) for hh in range(2)], axis=0)
        s3 = s.reshape(2, QB, nkeys) + mask_add[None]
        m = jnp.maximum(jnp.max(s3, axis=-1, keepdims=True), sink)
        e = jnp.exp2(s3 - m)
        denom = jnp.sum(e, axis=-1, keepdims=True) + jnp.exp2(sink - m)
        return e.astype(BF16).reshape(2 * QB, nkeys), denom

    def output(hp, e, denom):
        kvh = hp // (A_GROUP // 2)
        vals = jnp.concatenate([vp_ref[kvh], vc_ref[kvh], vn_ref[kvh], vx_ref[kvh]], axis=0)
        o = jnp.dot(e, vals, preferred_element_type=F32).reshape(2, QB, LANES) / denom
        o_ref[hp] = jnp.where(lo, o[0], o[1]).astype(BF16)

    _pipeline_pairs(scores, probs, output)


def _attn_a_call(qkv3, sink):
    n = qkv3.shape[1]
    nb = n // QB
    first_lat = CTX // QB
    kblk, vblk = HP // A_KV_HEADS, HP // A_KV_HEADS + 1

    def kv_specs(blk):
        return [pl.BlockSpec((A_KV_HEADS, QB, LANES), lambda b: (blk, jnp.maximum(b - 1, first_lat), 0)),
                pl.BlockSpec((A_KV_HEADS, QB, LANES), lambda b: (blk, b, 0)),
                pl.BlockSpec((A_KV_HEADS, QB, LANES), lambda b: (blk, jnp.minimum(b + 1, nb - 1), 0)),
                pl.BlockSpec((A_KV_HEADS, CTX, LANES), lambda b: (blk, 0, 0))]

    return pl.pallas_call(
        functools.partial(_attn_a_body, nb=nb),
        grid=(nb,),
        in_specs=[pl.BlockSpec(memory_space=pltpu.SMEM),
                  pl.BlockSpec((HP, QB, LANES), lambda b: (0, b, 0))] + kv_specs(kblk) + kv_specs(vblk),
        out_specs=pl.BlockSpec((HP, QB, LANES), lambda b: (0, b, 0)),
        out_shape=jax.ShapeDtypeStruct((HP, n, LANES), BF16),
        compiler_params=_cparams(40),
        name="attn_window",
    )(sink, qkv3, *([qkv3] * 8))


NA_TAB = 2 * NB_KH
RPB_W = 2 * NB_KW - 1
RPB_H = 2 * NB_KH - 1


def _na_table_body(rpb_ref, o_ref):
    h = pl.program_id(0)
    shape = (GRID_W, LANES)
    c = lax.broadcasted_iota(jnp.int32, shape, 0)
    lane = lax.broadcasted_iota(jnp.int32, shape, 1)
    kc = lane & (GRID_W - 1)
    hi = lane >= GRID_W
    cs = jnp.clip(c - NB_KW // 2, 0, GRID_W - NB_KW)
    colok = jnp.logical_and(kc >= cs, kc < cs + NB_KW)
    diff = kc - c + (NB_KW - 1)
    neg = jnp.full(shape, NEG, F32)
    rows = [neg]
    for d in range(RPB_H):
        acc = neg
        for j in range(RPB_W):
            acc = jnp.where(diff == j, rpb_ref[h * (RPB_H * RPB_W) + d * RPB_W + j] * LOG2E, acc)
        rows.append(jnp.where(colok, acc, NEG))
    rows.append(neg)
    for t in range(NA_TAB):
        o_ref[0, t] = jnp.where(hi, rows[t + 1], rows[t])


def _na_table_call(rpb):
    return pl.pallas_call(
        _na_table_body,
        grid=(HEADS,),
        in_specs=[pl.BlockSpec(memory_space=pltpu.SMEM)],
        out_specs=pl.BlockSpec((1, NA_TAB, GRID_W, LANES), lambda h: (h, 0, 0, 0)),
        out_shape=jax.ShapeDtypeStruct((HEADS, NA_TAB, GRID_W, LANES), F32),
        compiler_params=_cparams(16),
        name="na_bias_table",
    )(rpb.reshape(-1))


def _attn_b_body(tab_ref, q_ref, k0, k1, k2, k3, k4, kx_ref, v0, v1, v2, v3, v4, vx_ref, o_ref, *, nbl):
    b = pl.program_id(0)
    first_lat = CTX // QB
    is_lat = b >= first_lat
    bl = jnp.maximum(b - first_lat, 0)
    ws = jnp.clip(bl - 2, 0, nbl - NA_KBLK)
    n_rows = 2 * nbl
    nloc = NA_KBLK * QB
    rs = [jnp.clip(2 * bl + qr - NB_KH // 2, 0, n_rows - NB_KH) for qr in range(2)]
    tidx = [[jnp.clip(2 * (ws + j) - (2 * bl + qr) + (NB_KH - 1), -1, NA_TAB - 2) + 1 for j in range(NA_KBLK)]
            for qr in range(2)]
    qrow = lax.broadcasted_iota(jnp.int32, (QB, nloc), 0)
    kcol = lax.broadcasted_iota(jnp.int32, (QB, nloc), 1)
    krow = 2 * ws + jnp.right_shift(kcol, 6)
    rs_q = jnp.where(qrow < GRID_W, rs[0], rs[1])
    row_ok = jnp.logical_and(jnp.logical_and(krow >= rs_q, krow < rs_q + NB_KH), is_lat)
    row_ok2 = jnp.concatenate([row_ok, row_ok], axis=0)
    lo = _lane_lo((QB, LANES))
    zero = jnp.zeros((QB, LANES), BF16)
    kl = (k0, k1, k2, k3, k4)
    vl = (v0, v1, v2, v3, v4)

    def scores(hp):
        q2 = q_ref[hp]
        qs = jnp.concatenate([jnp.where(lo, q2, zero), jnp.where(lo, zero, q2)], axis=0)
        keys = jnp.concatenate([r[hp] for r in kl] + [kx_ref[hp]], axis=0)
        return lax.dot_general(qs, keys, (((1,), (1,)), ((), ())), preferred_element_type=F32)

    def probs(hp, s):
        bias = jnp.concatenate(
            [jnp.concatenate([tab_ref[2 * hp + hh, tidx[qr][j]] for j in range(NA_KBLK)], axis=1)
             for hh in range(2) for qr in range(2)], axis=0)
        s_loc = jnp.where(row_ok2, s[:, :nloc] + bias, NEG)
        s_ctx = s[:, nloc:]
        m = jnp.maximum(jnp.max(s_loc, axis=-1, keepdims=True), jnp.max(s_ctx, axis=-1, keepdims=True))
        e_loc = jnp.exp2(s_loc - m)
        e_ctx = jnp.exp2(s_ctx - m)
        denom = jnp.sum(e_loc, axis=-1, keepdims=True) + jnp.sum(e_ctx, axis=-1, keepdims=True)
        return jnp.concatenate([e_loc, e_ctx], axis=1).astype(BF16), denom

    def output(hp, p, denom):
        vals = jnp.concatenate([r[hp] for r in vl] + [vx_ref[hp]], axis=0)
        o = jnp.dot(p, vals, preferred_element_type=F32) / denom
        o_ref[hp] = jnp.where(lo, o[:QB], o[QB:]).astype(BF16)

    _pipeline_pairs(scores, probs, output)


def _attn_b_call(qkv3, table):
    n = qkv3.shape[1]
    nb = n // QB
    first_lat = CTX // QB
    nbl = nb - first_lat

    def win(blk, i):
        return pl.BlockSpec(
            (HP, QB, LANES),
            lambda b: (blk, first_lat + jnp.clip(jnp.maximum(b - first_lat, 0) - 2, 0, nbl - NA_KBLK) + i, 0))

    def kv_specs(blk):
        return [win(blk, i) for i in range(NA_KBLK)] + [pl.BlockSpec((HP, CTX, LANES), lambda b: (blk, 0, 0))]

    return pl.pallas_call(
        functools.partial(_attn_b_body, nbl=nbl),
        grid=(nb,),
        in_specs=[_resident((HEADS, NA_TAB, GRID_W, LANES), lambda b: (0, 0, 0, 0)),
                  pl.BlockSpec((HP, QB, LANES), lambda b: (0, b, 0))] + kv_specs(1) + kv_specs(2),
        out_specs=pl.BlockSpec((HP, QB, LANES), lambda b: (0, b, 0)),
        out_shape=jax.ShapeDtypeStruct((HP, n, LANES), BF16),
        compiler_params=_cparams(52),
        name="attn_neighbourhood",
    )(table, qkv3, *([qkv3] * 12))


def _oproj_router_body(*refs, mode, n_stream):
    n_mix = 1 if mode == "attn" else 2
    w_ref = refs[n_mix]
    h_refs = refs[n_mix + 1:n_mix + 1 + n_stream]
    mod_ref, g2_ref, wrh_ref, wrl_ref, br_ref, out_ref, f_ref, r_ref, cnt_ref, base_ref = refs[n_mix + 1 + n_stream:]
    if mode == "attn":
        z = jnp.concatenate([refs[0][c] for c in range(HP)], axis=1)
    else:
        z = (jax.nn.gelu(refs[0][...]) * refs[1][...]).astype(BF16)
    y = jnp.dot(z, w_ref[...], preferred_element_type=F32)
    h_new = _stream_tile(h_refs) + mod_ref[0, 2:3, :] * y
    out_ref[...] = h_new
    _route_tile(h_new, g2_ref[...], mod_ref[0, 3:4, :], mod_ref[0, 4:5, :], wrh_ref, wrl_ref, br_ref,
                f_ref, r_ref, cnt_ref, base_ref)


def _oproj_router_call(mix_in, w, h, mod, g2, wr, br, *, mode, name):
    n = _stream_rows(h)
    kdim = w.shape[0]
    if mode == "attn":
        in_specs = [pl.BlockSpec((HP, TM, LANES), lambda i: (0, i, 0))]
        args = [mix_in]
    else:
        u, hs = mix_in
        in_specs = [pl.BlockSpec((TM, D_RNN), lambda i: (i, 0)), pl.BlockSpec((TM, D_RNN), lambda i: (i, 0))]
        args = [u, hs]
    wr_hi = wr.astype(BF16)
    wr_lo = (wr - wr_hi.astype(F32)).astype(BF16)
    h_specs, h_args = _stream_specs(h)
    in_specs += [_resident((kdim, D), lambda i: (0, 0))] + h_specs + [
                 pl.BlockSpec((1, 6, D), lambda i: (jnp.minimum(i, 1), 0, 0)),
                 pl.BlockSpec((1, D), lambda i: (0, 0)),
                 pl.BlockSpec((D, LANES), lambda i: (0, 0)),
                 pl.BlockSpec((D, LANES), lambda i: (0, 0)),
                 pl.BlockSpec((1, LANES), lambda i: (0, 0))]
    return pl.pallas_call(
        functools.partial(_oproj_router_body, mode=mode, n_stream=len(h_args)),
        grid=(n // TM,),
        in_specs=in_specs,
        out_specs=[pl.BlockSpec((TM, D), lambda i: (i, 0)),
                   pl.BlockSpec((TM * XROW_TILE, LANES), lambda i: (i, 0)),
                   pl.BlockSpec((TM, LANES), lambda i: (i, 0)),
                   pl.BlockSpec((1, LANES), lambda i: (0, 0))],
        out_shape=[jax.ShapeDtypeStruct((n, D), F32),
                   jax.ShapeDtypeStruct((n * XROW_TILE, LANES), jnp.uint32),
                   jax.ShapeDtypeStruct((n, LANES), F32),
                   jax.ShapeDtypeStruct((1, LANES), F32)],
        scratch_shapes=[pltpu.VMEM((1, LANES), F32)],
        compiler_params=_cparams(52),
        name=name,
    )(*args, w, *h_args, mod, g2.reshape(1, D), wr_hi, wr_lo, br)


def _shift_rows(x, s, fill, reverse):
    t = x.shape[0]
    if s % SUBLANES == 0:
        pad = jnp.full((s, x.shape[1]), fill, x.dtype)
        return jnp.concatenate([x[s:], pad] if reverse else [pad, x[:t - s]], axis=0)
    row = lax.broadcasted_iota(jnp.int32, x.shape, 0)
    if reverse:
        return jnp.where(row >= t - s, fill, pltpu.roll(x, t - s, 0))
    return jnp.where(row < s, fill, pltpu.roll(x, s, 0))


def _chunk_scan(a, b, reverse):
    s = 1
    while s < a.shape[0]:
        a_s = _shift_rows(a, s, 1.0, reverse)
        b_s = _shift_rows(b, s, 0.0, reverse)
        b = a * b_s + b
        a = a * a_s
        s *= 2
    return a, b


def _group_scan(a, b, h_in, reverse, scr_ref):
    t, c = a.shape
    g = t // SUBLANES
    a3 = a.reshape(g, SUBLANES, c)
    b3 = b.reshape(g, SUBLANES, c)
    sub = lax.broadcasted_iota(jnp.int32, a3.shape, 1)
    s = 1
    while s < SUBLANES:
        edge = (sub >= SUBLANES - s) if reverse else (sub < s)
        shift = SUBLANES - s if reverse else s
        a_s = jnp.where(edge, 1.0, pltpu.roll(a3, shift, 1))
        b_s = jnp.where(edge, 0.0, pltpu.roll(b3, shift, 1))
        b3 = a3 * b_s + b3
        a3 = a3 * a_s
        s *= 2
    a_loc = a3.reshape(t, c)
    b_loc = b3.reshape(t, c)
    edge_row = 0 if reverse else SUBLANES - 1

    def boundary_rows(i, x):
        for j in range(c // LANES):
            scr_ref[i, j] = x[:, j * LANES:(j + 1) * LANES]
        return jnp.concatenate([scr_ref[i, j, pl.ds(edge_row, g, stride=SUBLANES), :] for j in range(c // LANES)],
                               axis=1)

    ga, gb = _chunk_scan(boundary_rows(0, a_loc), boundary_rows(1, b_loc), reverse)
    leaving = gb + ga * h_in
    entering = _shift_rows(leaving, 1, 0.0, reverse)
    grow = lax.broadcasted_iota(jnp.int32, leaving.shape, 0)
    entering = jnp.where(grow == (g - 1 if reverse else 0), h_in, entering)
    carry = jnp.broadcast_to(entering[:, None, :], (g, SUBLANES, c))
    h = (a3 * carry + b3).reshape(t, c)
    return h, (leaving[0:1, :] if reverse else leaving[g - 1:g, :])


def _lru_body(x_ref, cw_ref, cb_ref, wa_ref, wx_ref, ba_ref, bx_ref, lam_ref, o_ref, scr_ref, xc_ref, *, nchunks):
    n = nchunks * SCAN_TC
    halo = SUBLANES
    cw = cw_ref[...]
    cb = cb_ref[...]
    sp = jax.nn.softplus(-lam_ref[...])

    def conv_chunk(ci):
        r0 = pl.multiple_of(ci * SCAN_TC, SCAN_TC)
        lat_first = CTX // SCAN_TC
        prev_ok = jnp.logical_and(ci != 0, ci != lat_first).astype(F32)
        next_ok = jnp.logical_and(ci != lat_first - 1, ci != nchunks - 1).astype(F32)
        p0 = pl.multiple_of(jnp.maximum(r0 - halo, 0), halo)
        n0 = pl.multiple_of(jnp.minimum(r0 + SCAN_TC, n - halo), halo)
        xe = jnp.concatenate([x_ref[pl.ds(p0, halo), :] * prev_ok,
                              x_ref[pl.ds(r0, SCAN_TC), :],
                              x_ref[pl.ds(n0, halo), :] * next_ok], axis=0)
        y = cb
        for j in range(CONV_W):
            off = halo + j - CONV_LEFT
            y = y + xe[off:off + SCAN_TC, :] * cw[j:j + 1, :]
        return r0, y

    def direction(d, ci, h_in):
        if d == 0:
            r0, xc = conv_chunk(ci)
            xc_ref[pl.ds(r0, SCAN_TC), :] = xc
        else:
            r0 = pl.multiple_of(ci * SCAN_TC, SCAN_TC)
            xc = xc_ref[pl.ds(r0, SCAN_TC), :]
        xb = xc.astype(BF16)
        r = jax.nn.sigmoid(jnp.dot(xb, wa_ref[d, 0], preferred_element_type=F32) + ba_ref[d:d + 1, :])
        ig = jax.nn.sigmoid(jnp.dot(xb, wx_ref[d, 0], preferred_element_type=F32) + bx_ref[d:d + 1, :])
        log_a = -LRU_C * r * sp[d:d + 1, :]
        a = jnp.exp(log_a)
        bb = jnp.sqrt(1.0 - a * a) * (ig * xc)
        h, h_out = _group_scan(a, bb, h_in, d == 1, scr_ref)
        return r0, h, h_out

    def fwd(ci, h_in):
        r0, h, h_out = direction(0, ci, h_in)
        o_ref[pl.ds(r0, SCAN_TC), :] = h
        return h_out

    def bwd(ci, h_in):
        r0, h, h_out = direction(1, ci, h_in)
        o_ref[pl.ds(r0, SCAN_TC), :] += h
        return h_out

    h0 = jnp.zeros((1, RNN_BW), F32)
    lax.fori_loop(0, nchunks, fwd, h0)
    nctx = CTX // SCAN_TC
    hc = lax.fori_loop(0, nctx, lambda k, hh: bwd(nctx - 1 - k, hh), h0)
    lax.fori_loop(0, nchunks - nctx, lambda k, hh: bwd(nchunks - 1 - k, hh), hc)


def _lru_call(u, conv_w, conv_b, wa, wx, ba, bx, lam):
    n = u.shape[0]
    return pl.pallas_call(
        functools.partial(_lru_body, nchunks=n // SCAN_TC),
        grid=(RNN_BLOCKS,),
        in_specs=[_resident((n, RNN_BW), lambda m: (0, RNN_BLOCKS + m)),
                  pl.BlockSpec((CONV_W, RNN_BW), lambda m: (0, m)),
                  pl.BlockSpec((1, RNN_BW), lambda m: (0, m)),
                  pl.BlockSpec((2, 1, RNN_BW, RNN_BW), lambda m: (0, m, 0, 0)),
                  pl.BlockSpec((2, 1, RNN_BW, RNN_BW), lambda m: (0, m, 0, 0)),
                  pl.BlockSpec((2, RNN_BW), lambda m: (0, m)),
                  pl.BlockSpec((2, RNN_BW), lambda m: (0, m)),
                  pl.BlockSpec((2, RNN_BW), lambda m: (0, m))],
        out_specs=_resident((n, RNN_BW), lambda m: (0, m)),
        scratch_shapes=[pltpu.VMEM((2, RNN_BW // LANES, SCAN_TC, LANES), F32), pltpu.VMEM((n, RNN_BW), F32)],
        out_shape=jax.ShapeDtypeStruct((n, D_RNN), F32),
        compiler_params=_cparams(58),
        name="rglru_scan",
    )(u, conv_w, conv_b.reshape(1, D_RNN), wa, wx, ba, bx, lam)


def _pack_words(ref, lo, hi, first_tile, rounded=False):
    rows = lo.shape[0]
    if not rounded:
        lo, hi = lo.astype(BF16).astype(F32), hi.astype(BF16).astype(F32)
    lo = lax.bitcast_convert_type(lo, jnp.uint32)
    hi = lax.bitcast_convert_type(hi, jnp.uint32)
    w = hi | (lo >> 16)
    for s in range(lo.shape[1] // LANES):
        ref[pl.ds(first_tile + s, rows, stride=XROW_TILE), :] = w[:, s * LANES:(s + 1) * LANES]


def _unpack_rows(ref, rows, dtype):
    w = jnp.concatenate([ref[pl.ds(s, rows, stride=XROW_TILE), :] for s in range(XROW_TILE)], axis=1)
    lo = lax.bitcast_convert_type(w << 16, F32)
    hi = lax.bitcast_convert_type(w & jnp.uint32(0xFFFF0000), F32)
    return jnp.concatenate([lo, hi], axis=1).astype(dtype)


def _route_tile(h, g, shift, scale, wrh_ref, wrl_ref, br_ref, f_ref, r_ref, cnt_ref, base_ref):
    @pl.when(pl.program_id(0) == 0)
    def _():
        base_ref[...] = jnp.zeros(base_ref.shape, F32)

    f = _norm_mod(h, g, shift, scale)
    f_hi = f.astype(BF16)
    f_hi32 = f_hi.astype(F32)
    _pack_words(f_ref, f_hi32[:, :D // 2], f_hi32[:, D // 2:], 0, rounded=True)
    f_lo = (f - f_hi32).astype(BF16)
    logits = (jnp.dot(jnp.concatenate([f_hi, f_lo], axis=0), wrh_ref[...], preferred_element_type=F32).reshape(2, TM, LANES).sum(axis=0)
              + jnp.dot(f_hi, wrl_ref[...], preferred_element_type=F32) + br_ref[...])
    lane = lax.broadcasted_iota(jnp.int32, logits.shape, 1)
    ninf = -jnp.inf
    is_g = lane < N_GROUPS
    gl = jnp.where(is_g, logits, ninf)
    gmax = jnp.max(gl, axis=-1, keepdims=True)
    gsel = jnp.min(jnp.where(gl == gmax, lane, LANES), axis=-1, keepdims=True)
    gsum = jnp.sum(jnp.where(is_g, jnp.exp(gl - gmax), 0.0), axis=-1, keepdims=True)
    g_w = 1.0 / gsum
    e_lo = N_GROUPS + EXP_PER_GROUP * gsel
    in_grp = jnp.logical_and(lane >= e_lo, lane < e_lo + EXP_PER_GROUP)
    el = jnp.where(in_grp, logits, ninf)
    m1 = jnp.max(el, axis=-1, keepdims=True)
    i1 = jnp.min(jnp.where(el == m1, lane, LANES), axis=-1, keepdims=True)
    el2 = jnp.where(lane == i1, ninf, el)
    m2 = jnp.max(el2, axis=-1, keepdims=True)
    i2 = jnp.min(jnp.where(el2 == m2, lane, LANES), axis=-1, keepdims=True)
    t = jnp.exp(m2 - m1)
    w1 = g_w / (1.0 + t)
    w2 = g_w * t / (1.0 + t)
    oh1 = lane == i1
    oh2 = lane == i2
    rr = lax.broadcasted_iota(jnp.int32, (TM, TM), 0)
    cc = lax.broadcasted_iota(jnp.int32, (TM, TM), 1)
    tri = (cc < rr).astype(BF16)
    pre1 = jnp.dot(tri, oh1.astype(BF16), preferred_element_type=F32)
    pre2 = jnp.dot(tri, oh2.astype(BF16), preferred_element_type=F32)
    base = base_ref[...]
    cnt1 = jnp.sum(oh1.astype(F32), axis=0, keepdims=True)
    cnt2 = jnp.sum(oh2.astype(F32), axis=0, keepdims=True)
    rank1 = jnp.sum(jnp.where(oh1, pre1 + base, 0.0), axis=-1, keepdims=True)
    rank2 = jnp.sum(jnp.where(oh2, pre2 + (base + cnt1), 0.0), axis=-1, keepdims=True)
    total = base + cnt1 + cnt2
    base_ref[...] = total
    cnt_ref[...] = total
    cols = [(i1 - N_GROUPS).astype(F32), (i2 - N_GROUPS).astype(F32), w1, w2, rank1, rank2]
    out = jnp.zeros(logits.shape, F32)
    for k, v in enumerate(cols):
        out = jnp.where(lane == k, v, out)
    r_ref[...] = out


def _dispatch_body(dest_ref, pad_lo_ref, pad_n_ref, f_ref, xg_ref, z_ref, sem, run_sem, blk_sem):
    i = pl.program_id(0)
    base = i * TM

    def issue(j, carry):
        for k in range(TOP_K):
            d = dest_ref[(base + j) * TOP_K + k]
            pltpu.make_async_copy(f_ref.at[pl.ds(j * XROW_TILE, XROW_TILE)],
                                  xg_ref.at[pl.ds(d * XROW_TILE, XROW_TILE)], sem).start(priority=k)
        return carry

    z_row = z_ref.at[pl.ds(0, XROW_TILE)]

    def wait_one(carry):
        pltpu.make_async_copy(z_row, xg_ref.at[pl.ds(0, XROW_TILE)], sem).wait()
        return carry

    lax.fori_loop(0, TM, issue, 0, unroll=DMA_UNROLL)

    @pl.when(i == 0)
    def _():
        z_ref[...] = jnp.zeros(z_ref.shape, jnp.uint32)

        z_run = z_ref.at[pl.ds(0, FILL_RUN * XROW_TILE)]

        def fill(e, carry):
            lo = pad_lo_ref[e]
            n_fill = pad_n_ref[e]
            n_runs = n_fill // FILL_RUN
            lo_rows = lo + n_runs * FILL_RUN
            n_rows = n_fill - n_runs * FILL_RUN

            def run_copy(bi):
                return pltpu.make_async_copy(
                    z_run, xg_ref.at[pl.ds((lo + bi * FILL_RUN) * XROW_TILE, FILL_RUN * XROW_TILE)], run_sem)

            def one(s, c):
                pltpu.make_async_copy(z_row, xg_ref.at[pl.ds((lo_rows + s) * XROW_TILE, XROW_TILE)], sem).start()
                return c

            lax.fori_loop(0, n_runs, lambda bi, c: (run_copy(bi).start(), c)[1], 0)
            lax.fori_loop(0, n_rows, one, 0)
            lax.fori_loop(0, n_runs, lambda bi, c: (run_copy(bi).wait(), c)[1], 0)
            lax.fori_loop(0, n_rows, lambda s, c: wait_one(c), 0)
            return carry

        lax.fori_loop(0, N_EXPERTS, fill, 0)

        def block_copy(bi):
            row0 = (pad_lo_ref[N_EXPERTS] + bi * MOE_MB) * XROW_TILE
            return pltpu.make_async_copy(z_ref, xg_ref.at[pl.ds(row0, MOE_MB * XROW_TILE)], blk_sem)

        n_tail = pad_n_ref[N_EXPERTS] // MOE_MB
        lax.fori_loop(0, n_tail, lambda bi, c: (block_copy(bi).start(), c)[1], 0)
        lax.fori_loop(0, n_tail, lambda bi, c: (block_copy(bi).wait(), c)[1], 0)

    lax.fori_loop(0, TM * TOP_K, lambda j, c: wait_one(c), 0, unroll=DMA_UNROLL)


def _dispatch_call(dest, pad_lo, pad_n, f2, p):
    n = f2.shape[0] // XROW_TILE
    return pl.pallas_call(
        _dispatch_body,
        grid_spec=pltpu.PrefetchScalarGridSpec(
            num_scalar_prefetch=3,
            grid=(n // TM,),
            in_specs=[pl.BlockSpec((TM * XROW_TILE, LANES), lambda i, d, lo, nn: (i, 0))],
            out_specs=pl.BlockSpec(memory_space=pl.ANY),
            scratch_shapes=[pltpu.VMEM((MOE_MB * XROW_TILE, LANES), jnp.uint32), pltpu.SemaphoreType.DMA(()),
                            pltpu.SemaphoreType.DMA(()), pltpu.SemaphoreType.DMA(())]),
        out_shape=jax.ShapeDtypeStruct((p * XROW_TILE, LANES), jnp.uint32),
        compiler_params=_cparams(16),
        name="moe_dispatch",
    )(dest, pad_lo, pad_n, f2)


def _expert_weights(b, be_ref, ne_ref, nu_ref, w_hbm, wf_ref, wb_ref, sem, slot_ref, layer):
    e = be_ref[b]
    first = jnp.logical_or(b == 0, e != be_ref[jnp.maximum(b - 1, 0)])

    def fetch(expert, slot):
        return pltpu.make_async_copy(w_hbm.at[layer, expert], wf_ref.at[slot], sem.at[slot])

    @pl.when(b == 0)
    def _():
        slot_ref[0] = 0
        fetch(e, 0).start()

    @pl.when(jnp.logical_and(first, b < nu_ref[0]))
    def _():
        s = slot_ref[0]
        fetch(e, s).wait()
        wb_ref[...] = wf_ref[s].astype(BF16)
        ne = ne_ref[b]

        @pl.when(ne != e)
        def _():
            fetch(ne, 1 - s).start()

        slot_ref[0] = 1 - s


def _ffn_up_body(be_ref, ne_ref, nu_ref, x_ref, w_hbm, a_ref, wf_ref, wb_ref, sem, slot_ref, *, layer):
    b = pl.program_id(0)
    _expert_weights(b, be_ref, ne_ref, nu_ref, w_hbm, wf_ref, wb_ref, sem, slot_ref, layer)

    @pl.when(b < nu_ref[0])
    def _():
        x = _unpack_rows(x_ref, MOE_MB, BF16)
        cw = FFN_CHUNK
        n_chunks = D_EXPERT // cw

        def gate_up(c):
            return (jnp.dot(x, wb_ref[:, c * cw:(c + 1) * cw], preferred_element_type=F32),
                    jnp.dot(x, wb_ref[:, D_EXPERT + c * cw:D_EXPERT + (c + 1) * cw], preferred_element_type=F32))

        nxt = gate_up(0)
        for c in range(n_chunks):
            g, u = nxt
            if c + 1 < n_chunks:
                nxt = gate_up(c + 1)
            a_ref[:, c * cw:(c + 1) * cw] = (g * jax.nn.sigmoid(g) * u).astype(BF16)

    @pl.when(b >= nu_ref[0])
    def _():
        a_ref[...] = jnp.zeros(a_ref.shape, BF16)


def _expert_scratch(rows, cols):
    return [pltpu.VMEM((2, rows, cols), F32), pltpu.VMEM((rows, cols), BF16),
            pltpu.SemaphoreType.DMA((2,)), pltpu.SMEM((1,), jnp.int32)]


def _ffn_up_call(block_e, next_e, n_used, xg, w_gu, layer):
    p = xg.shape[0] // XROW_TILE
    return pl.pallas_call(
        functools.partial(_ffn_up_body, layer=layer),
        grid_spec=pltpu.PrefetchScalarGridSpec(
            num_scalar_prefetch=3,
            grid=(p // MOE_MB,),
            in_specs=[pl.BlockSpec((MOE_MB * XROW_TILE, LANES),
                                   lambda b, be, ne, nu: (jnp.minimum(b, nu[0] - 1), 0)),
                      pl.BlockSpec(memory_space=pl.ANY)],
            out_specs=pl.BlockSpec((MOE_MB, D_EXPERT), lambda b, be, ne, nu: (b, 0)),
            scratch_shapes=_expert_scratch(D, 2 * D_EXPERT)),
        out_shape=jax.ShapeDtypeStruct((p, D_EXPERT), BF16),
        compiler_params=_cparams(48),
        name="moe_ffn_up",
    )(block_e, next_e, n_used, xg, w_gu)


def _ffn_down_body(be_ref, ne_ref, nu_ref, a_ref, w_hbm, y_ref, wf_ref, wb_ref, sem, slot_ref, *, layer):
    b = pl.program_id(0)
    _expert_weights(b, be_ref, ne_ref, nu_ref, w_hbm, wf_ref, wb_ref, sem, slot_ref, layer)

    @pl.when(b < nu_ref[0])
    def _():
        a = a_ref[...]
        cw = FFN_CHUNK
        n_chunks = D // 2 // cw

        def halves(c):
            return (jnp.dot(a, wb_ref[:, c * cw:(c + 1) * cw], preferred_element_type=F32),
                    jnp.dot(a, wb_ref[:, D // 2 + c * cw:D // 2 + (c + 1) * cw], preferred_element_type=F32))

        nxt = halves(0)
        for c in range(n_chunks):
            lo, hi = nxt
            if c + 1 < n_chunks:
                nxt = halves(c + 1)
            _pack_words(y_ref, lo, hi, c * (cw // LANES))

    @pl.when(b >= nu_ref[0])
    def _():
        y_ref[...] = jnp.zeros(y_ref.shape, jnp.uint32)


def _ffn_down_call(block_e, next_e, n_used, act, w_down, layer):
    p = act.shape[0]
    return pl.pallas_call(
        functools.partial(_ffn_down_body, layer=layer),
        grid_spec=pltpu.PrefetchScalarGridSpec(
            num_scalar_prefetch=3,
            grid=(p // MOE_MB,),
            in_specs=[pl.BlockSpec((MOE_MB, D_EXPERT), lambda b, be, ne, nu: (b, 0)),
                      pl.BlockSpec(memory_space=pl.ANY)],
            out_specs=pl.BlockSpec((MOE_MB * XROW_TILE, LANES), lambda b, be, ne, nu: (b, 0)),
            scratch_shapes=_expert_scratch(D_EXPERT, D)),
        out_shape=jax.ShapeDtypeStruct((p * XROW_TILE, LANES), jnp.uint32),
        compiler_params=_cparams(32),
        name="moe_ffn_down",
    )(block_e, next_e, n_used, act, w_down)


def _combine_body(dest_ref, h_ref, r_ref, mod_ref, g_ref, yb_ref, o_ref, buf_ref, sem, *, final):
    i = pl.program_id(0)
    slot = i % 2

    def gather(tile, dst_slot):
        base = tile * TM

        def issue(j, carry):
            for k in range(TOP_K):
                d = dest_ref[(base + j) * TOP_K + k]
                pltpu.make_async_copy(yb_ref.at[pl.ds(d * XROW_TILE, XROW_TILE)],
                                      buf_ref.at[dst_slot, k, pl.ds(j * XROW_TILE, XROW_TILE)],
                                      sem.at[dst_slot]).start(priority=k)
            return carry

        lax.fori_loop(0, TM, issue, 0, unroll=DMA_UNROLL)

    @pl.when(i == 0)
    def _():
        gather(0, 0)

    @pl.when(i + 1 < pl.num_programs(0))
    def _():
        gather(i + 1, 1 - slot)

    def drain(j, carry):
        pltpu.make_async_copy(yb_ref.at[pl.ds(0, XROW_TILE)], buf_ref.at[slot, 0, pl.ds(0, XROW_TILE)],
                              sem.at[slot]).wait()
        return carry

    lax.fori_loop(0, TM * TOP_K, drain, 0, unroll=DMA_UNROLL)
    r = r_ref[...]
    y = sum(r[:, TOP_K + k:TOP_K + k + 1] * _unpack_rows(buf_ref.at[slot, k], TM, F32) for k in range(TOP_K))
    h_new = h_ref[...] + mod_ref[0, 2:3, :] * y
    if final:
        ms = jnp.mean(h_new * h_new, axis=-1, keepdims=True)
        h_new = h_new * lax.rsqrt(ms + EPS) * g_ref[...]
    o_ref[...] = h_new


def _combine_call(dest, h, route, mod, yb, final_g=None):
    n = h.shape[0]
    final = final_g is not None
    first = CTX // TM
    out_rows = n - CTX if final else n
    out_map = (lambda i, d: (jnp.maximum(i - first, 0), 0)) if final else (lambda i, d: (i, 0))
    g = final_g if final else jnp.ones((D,), F32)
    return pl.pallas_call(
        functools.partial(_combine_body, final=final),
        grid_spec=pltpu.PrefetchScalarGridSpec(
            num_scalar_prefetch=1,
            grid=(n // TM,),
            in_specs=[pl.BlockSpec((TM, D), lambda i, d: (i, 0)),
                      pl.BlockSpec((TM, LANES), lambda i, d: (i, 0)),
                      pl.BlockSpec((1, 3, D), lambda i, d: (jnp.minimum(i, 1), 0, 0)),
                      pl.BlockSpec((1, D), lambda i, d: (0, 0)),
                      pl.BlockSpec(memory_space=pl.ANY)],
            out_specs=pl.BlockSpec((TM, D), out_map),
            scratch_shapes=[pltpu.VMEM((2, TOP_K, TM * XROW_TILE, LANES), jnp.uint32),
                            pltpu.SemaphoreType.DMA((2,))]),
        out_shape=jax.ShapeDtypeStruct((out_rows, D), F32),
        compiler_params=_cparams(32),
        name="moe_combine",
    )(dest, h, route, mod[:, 3:6], g.reshape(1, D), yb)


def _dispatch_plan(route, cnt, n):
    experts = jnp.arange(N_EXPERTS, dtype=jnp.int32)
    counts = cnt[0, N_GROUPS:N_GROUPS + N_EXPERTS].astype(jnp.int32)
    padded = (counts + MOE_MB - 1) // MOE_MB * MOE_MB
    end_pad = jnp.cumsum(padded)
    start_pad = end_pad - padded
    n_blocks = -(-(n * TOP_K + N_EXPERTS * (MOE_MB - 1)) // MOE_MB)
    p = n_blocks * MOE_MB
    blk_start = jnp.arange(n_blocks, dtype=jnp.int32) * MOE_MB
    block_e = jnp.minimum(jnp.sum(end_pad[None, :] <= blk_start[:, None], axis=1), N_EXPERTS - 1).astype(jnp.int32)
    n_used = (end_pad[-1:] // MOE_MB).astype(jnp.int32)
    group_end = end_pad[block_e] // MOE_MB
    next_e = jnp.where(group_end < n_used[0], block_e[jnp.minimum(group_end, n_blocks - 1)], block_e)
    e_idx = route[:, 0:TOP_K].astype(jnp.int32)
    rank = route[:, 2 * TOP_K:3 * TOP_K].astype(jnp.int32)
    start = jnp.sum(jnp.where(e_idx[..., None] == experts, start_pad, 0), axis=-1)
    dest = (start + rank).reshape(n * TOP_K)
    pad_lo = jnp.concatenate([start_pad + counts, end_pad[-1:]]).astype(jnp.int32)
    pad_n = jnp.concatenate([padded - counts, p - end_pad[-1:]]).astype(jnp.int32)
    return p, block_e, next_e, n_used, dest, pad_lo, pad_n


def _moe_layer(h, f2, route, cnt, mod, w_gu, w_down, layer, final_g=None):
    n = h.shape[0]
    p, block_e, next_e, n_used, dest, pad_lo, pad_n = _dispatch_plan(route, cnt, n)
    xg = _dispatch_call(dest, pad_lo, pad_n, f2, p)
    act = _ffn_up_call(block_e, next_e, n_used, xg, w_gu, layer)
    yb = _ffn_down_call(block_e, next_e, n_used, act, w_down, layer)
    return _combine_call(dest, h, route, mod, yb, final_g)


def _rope_tables(l):
    quarter = HEAD_DIM // 4
    inv = ROPE_BASE ** (-jnp.arange(quarter, dtype=F32) / quarter)
    n_rows = l // GRID_W
    ang_r = jnp.arange(n_rows, dtype=F32)[:, None] * inv
    ang_c = jnp.arange(GRID_W, dtype=F32)[:, None] * inv
    by_row = lambda t: jnp.broadcast_to(t[:, None, :], (n_rows, GRID_W, quarter)).reshape(l, quarter)
    by_col = lambda t: jnp.broadcast_to(t[None, :, :], (n_rows, GRID_W, quarter)).reshape(l, quarter)
    cr, sr, cc, sc = by_row(jnp.cos(ang_r)), by_row(jnp.sin(ang_r)), by_col(jnp.cos(ang_c)), by_col(jnp.sin(ang_c))
    cos = jnp.concatenate([cr, cr, cc, cc], axis=1)
    sin = jnp.concatenate([-sr, sr, -sc, sc], axis=1)
    cos = jnp.concatenate([jnp.ones((CTX, HEAD_DIM), F32), cos], axis=0)
    sin = jnp.concatenate([jnp.zeros((CTX, HEAD_DIM), F32), sin], axis=0)
    return jnp.tile(cos, (1, 2)), jnp.tile(sin, (1, 2))


def _attn_a_weights(w_qkv):
    nq = HEADS * HEAD_DIM
    nkv = A_KV_HEADS * HEAD_DIM
    wq = w_qkv[:, :nq]
    wk = w_qkv[:, nq:nq + nkv].reshape(D, A_KV_HEADS, 1, HEAD_DIM)
    wv = w_qkv[:, nq + nkv:].reshape(D, A_KV_HEADS, 1, HEAD_DIM)
    dup = lambda w: jnp.broadcast_to(w, (D, A_KV_HEADS, 2, HEAD_DIM)).reshape(D, 2 * nkv)
    return jnp.concatenate([wq, dup(wk), dup(wv)], axis=1).astype(BF16)


def kernel(x, c, ctx, c_ctx, ada_w, ada_b, norm_mix_g, norm_ffn_g, router_group_w, router_group_b,
           router_expert_w, router_expert_b, moe_w_gu, moe_w_down, attn_w_qkv, attn_w_o, attn_sink,
           na_w_qkv, na_w_o, na_rpb, rnn_w_in, rnn_conv_w, rnn_conv_b, rnn_wa, rnn_ba, rnn_wx, rnn_bx,
           rnn_lam, rnn_w_out, final_norm_g):
    batch, l, _ = x.shape
    assert batch == 1 and ctx.shape[1] == CTX and l % (QB * 2) == 0 and l // QB >= NA_KBLK
    h = (ctx[0], x[0])
    c2 = jnp.stack([c_ctx, c[0]], axis=1)
    mods = _ada_call(c2, ada_w, ada_b).reshape(DEPTH, 2, 6, D)
    rope = _rope_tables(l)
    pad_r = LANES - N_GROUPS - N_EXPERTS
    for i in range(DEPTH):
        kind, j = i % 3, i // 3
        mod = mods[i]
        wr = jnp.concatenate([router_group_w[i], router_expert_w[i], jnp.zeros((D, pad_r), F32)], axis=1)
        br = jnp.concatenate([router_group_b[i], router_expert_b[i], jnp.zeros((pad_r,), F32)]).reshape(1, LANES)
        if kind == 0:
            w = _attn_a_weights(attn_w_qkv[j])
            nq = HEADS * HEAD_DIM
            qkv3 = _proj_call(h, norm_mix_g[i], mod, w, mode="rope", rope=rope,
                              n_rope=nq + 2 * A_KV_HEADS * HEAD_DIM, n_q=nq, name="proj_window")
            mix, w_out, mode, name = _attn_a_call(qkv3, attn_sink[j]), attn_w_o[j], "attn", "oproj_window"
        elif kind == 1:
            qkv3 = _proj_call(h, norm_mix_g[i], mod, na_w_qkv[j].astype(BF16), mode="cols",
                              n_q=HEADS * HEAD_DIM, name="proj_neighbourhood")
            mix, w_out, mode, name = (_attn_b_call(qkv3, _na_table_call(na_rpb[j])), na_w_o[j], "attn",
                                      "oproj_neighbourhood")
        else:
            u = _proj_call(h, norm_mix_g[i], mod, rnn_w_in[j].astype(BF16), mode="plain", name="proj_rglru")
            hs = _lru_call(u, rnn_conv_w[j], rnn_conv_b[j], rnn_wa[j].astype(BF16), rnn_wx[j].astype(BF16),
                           rnn_ba[j], rnn_bx[j], rnn_lam[j])
            mix, w_out, mode, name = (u, hs), rnn_w_out[j], "rnn", "oproj_rglru"
        h, f2, route, cnt = _oproj_router_call(mix, w_out.astype(BF16), h, mod, norm_ffn_g[i], wr, br,
                                               mode=mode, name=name)
        h = _moe_layer(h, f2, route, cnt, mod, moe_w_gu, moe_w_down, i,
                       final_norm_g if i == DEPTH - 1 else None)
    return h[None]
```
